```python
import functools
import math
import jax
import jax.numpy as jnp
from jax import lax
import numpy as np

D_MODEL = 1024
BATCH = 8
SEQ = 2048
DEPTH = 2
DEC_BATCH = 32
DEC_SEQ = 8
PAST_LEN = 16384
PAGE_SIZE = 128

N_BRANCH = 4
HEAD_DIM = 64
BRANCH_WIDTH = D_MODEL // N_BRANCH
ML_HEADS = BRANCH_WIDTH // HEAD_DIM
MB_HEADS = BRANCH_WIDTH // HEAD_DIM
RW_HEADS = BRANCH_WIDTH // HEAD_DIM
CONV_WIDTH = 3
MB_BLOCK = 256
MB_TOPK = 3
MB_QBLK = 32
ROPE_DIMS = HEAD_DIM // 4
ROPE_THETA = 500000.0
RW_DECAY_RANK = 32
RW_A_RANK = 32
RW_GATE_RANK = 64
RW_COLS = 3 * BRANCH_WIDTH + RW_DECAY_RANK + RW_A_RANK + RW_GATE_RANK
D_FF = 11 * D_MODEL // 4
MLSTM_CHUNK = 64
NORM_EPS = 1e-6
GN_EPS = 64e-5
IN_SPLITS = (
    ('m_q', BRANCH_WIDTH), ('m_k', BRANCH_WIDTH), ('m_v', BRANCH_WIDTH), ('m_o', BRANCH_WIDTH),
    ('m_i', ML_HEADS), ('m_f', ML_HEADS),
    ('s_b', BRANCH_WIDTH), ('s_c', BRANCH_WIDTH), ('s_h', BRANCH_WIDTH),
    ('a_q', BRANCH_WIDTH), ('a_k', BRANCH_WIDTH), ('a_v', BRANCH_WIDTH),
    ('rwkv', RW_COLS),
    ('gates', N_BRANCH * D_MODEL),
)
N_IN = sum(width for _, width in IN_SPLITS)

kernel_name = 'hybrid_gated_mlstm_conv_moba_rwkv7_decoder_step'


def split_in(z):
    out, off = {}, 0
    for name, width in IN_SPLITS:
        out[name] = z[..., off:off + width]
        off += width
    return out


def rms_norm(x, g):
    xf = x.astype(jnp.float32)
    y = xf * lax.rsqrt(jnp.mean(xf * xf, axis=-1, keepdims=True) + NORM_EPS)
    return (y * g.astype(jnp.float32)).astype(x.dtype)


def head_rms_norm(h, g):
    B, L, H, dh = h.shape
    hf = h.astype(jnp.float32)
    hn = hf * lax.rsqrt(jnp.mean(hf * hf, axis=-1, keepdims=True) + NORM_EPS)
    return hn.reshape(B, L, H * dh) * g


def group_norm_heads(y, g):
    B, L, H, dh = y.shape
    mu = jnp.mean(y, axis=-1, keepdims=True)
    yc = y - mu
    yn = yc * lax.rsqrt(jnp.mean(yc * yc, axis=-1, keepdims=True) + GN_EPS)
    return yn.reshape(B, L, H * dh) * g


def rope_partial(x, pos):
    half = ROPE_DIMS // 2
    inv = jnp.exp(-math.log(ROPE_THETA) * jnp.arange(0, ROPE_DIMS, 2, dtype=jnp.float32) / ROPE_DIMS)
    ang = pos.astype(jnp.float32)[:, None] * inv[None, :]
    cos = jnp.cos(ang)[None, :, None, :]
    sin = jnp.sin(ang)[None, :, None, :]
    xf = x.astype(jnp.float32)
    x1, x2, rest = xf[..., :half], xf[..., half:ROPE_DIMS], xf[..., ROPE_DIMS:]
    return jnp.concatenate([x1 * cos - x2 * sin, x2 * cos + x1 * sin, rest], axis=-1).astype(x.dtype)


def causal_dwconv(u, w, buf):
    L = u.shape[1]
    ext = jnp.concatenate([buf.astype(u.dtype), u], axis=1)
    y = sum(w[j] * ext[:, j:j + L] for j in range(CONV_WIDTH))
    return y, ext[:, L:]


def token_shift(z, prev, mu):
    zs = jnp.concatenate([prev[:, None].astype(z.dtype), z[:, :-1]], axis=1)
    return z + mu * (zs - z), z[:, -1]


def mlstm_chunked(q, k, v, ig, fg, c0, n0, m0, chunk):
    B, L, H, dh = q.shape
    nc = L // chunk
    f32 = jnp.float32

    def to_chunks(a):
        a = a.astype(f32).reshape((B, nc, chunk) + a.shape[2:])
        return jnp.swapaxes(jnp.moveaxis(a, 1, 0), 2, 3)

    qc, kc, vc = to_chunks(q), to_chunks(k) * (dh ** -0.5), to_chunks(v)
    lic = to_chunks(ig)
    lfc = to_chunks(jax.nn.log_sigmoid(fg.astype(f32)))
    causal = jnp.tril(jnp.ones((chunk, chunk), dtype=bool))

    def step(carry, xs):
        c, n, m = carry
        qt, kt, vt, li, lf = xs
        b = jnp.cumsum(lf, axis=-1)
        d = jnp.where(causal, b[..., :, None] - b[..., None, :] + li[..., None, :], -jnp.inf)
        inter = b + m[..., None]
        mt = jnp.maximum(jnp.max(d, axis=-1), inter)
        w = jnp.exp(d - mt[..., None]) * jnp.einsum('bhtd,bhsd->bhts', qt, kt)
        a_int = jnp.exp(inter - mt)
        num = jnp.einsum('bhts,bhsd->bhtd', w, vt) + a_int[..., None] * jnp.einsum('bhtd,bhde->bhte', qt, c)
        den = jnp.sum(w, axis=-1) + a_int * jnp.einsum('bhtd,bhd->bht', qt, n)
        h = num / jnp.maximum(jnp.abs(den), jnp.exp(-mt))[..., None]
        b_end = b[..., -1]
        g = b_end[..., None] - b + li
        m_new = jnp.maximum(b_end + m, jnp.max(g, axis=-1))
        wk = jnp.exp(g - m_new[..., None])
        decay = jnp.exp(b_end + m - m_new)
        c = decay[..., None, None] * c + jnp.einsum('bhs,bhsd,bhse->bhde', wk, kt, vt)
        n = decay[..., None] * n + jnp.einsum('bhs,bhsd->bhd', wk, kt)
        return (c, n, m_new), h

    (c, n, m), h = lax.scan(step, (c0.astype(f32), n0.astype(f32), m0.astype(f32)), (qc, kc, vc, lic, lfc))
    h = jnp.transpose(h, (1, 0, 3, 2, 4)).reshape(B, L, H, dh)
    dt = q.dtype
    return h.astype(dt), c.astype(dt), n.astype(dt), m.astype(dt)


def rwkv7_scan(r, w, k, v, kk, a, s0):
    def step(s, xs):
        r_t, w_t, k_t, v_t, kk_t, a_t = xs
        s_kk = jnp.einsum('bhvk,bhk->bhv', s, kk_t)
        s = s * w_t[:, :, None, :] - s_kk[..., None] * (kk_t * a_t)[:, :, None, :] + v_t[..., None] * k_t[:, :, None, :]
        return s, jnp.einsum('bhvk,bhk->bhv', s, r_t)

    s, y = lax.scan(step, s0, tuple(jnp.moveaxis(t, 1, 0) for t in (r, w, k, v, kk, a)))
    return jnp.moveaxis(y, 0, 1), s


def rwkv7_branch(z, prev, s0, p):
    B, L, _ = z.shape
    W, H, dh = BRANCH_WIDTH, RW_HEADS, HEAD_DIM
    z, last = token_shift(z, prev, p['r_mu'])
    zf = z.astype(jnp.float32)
    r, k, v = zf[..., :W], zf[..., W:2 * W], zf[..., 2 * W:3 * W]
    o = 3 * W
    wd = zf[..., o:o + RW_DECAY_RANK]
    o += RW_DECAY_RANK
    ad = zf[..., o:o + RW_A_RANK]
    o += RW_A_RANK
    gd = zf[..., o:]
    decay = jnp.exp(-math.exp(-0.5) * jax.nn.sigmoid(p['r_w0'] + jnp.tanh(wd) @ p['r_wB']))
    a = jax.nn.sigmoid(p['r_a0'] + ad @ p['r_aB'])
    g = jax.nn.sigmoid(gd) @ p['r_gB']
    heads = lambda t: t.reshape(B, L, H, dh)
    kk = heads(k * p['r_kk'])
    kk = kk * lax.rsqrt(jnp.maximum(jnp.sum(kk * kk, axis=-1, keepdims=True), 1e-24))
    k = k * (1.0 + (a - 1.0) * p['r_ka'])
    rh, kh, vh = heads(r), heads(k), heads(v)
    y, s = rwkv7_scan(rh, heads(decay), kh, vh, kk, heads(a), s0.astype(jnp.float32))
    bonus = jnp.sum(rh * kh * p['r_rk'].reshape(H, dh), axis=-1, keepdims=True) * vh
    y = (group_norm_heads(y, p['r_norm_g']) + bonus.reshape(B, L, W)) * g
    return y.astype(z.dtype) @ p['r_proj'], s.astype(z.dtype), last


def moba_select(q, q_pos, means):
    nb = means.shape[2]
    own = q_pos // MB_BLOCK
    s = jnp.einsum('bhqd,bhnd->bhqn', q.astype(jnp.float32), means)
    is_past = jnp.arange(nb)[None, :] < own[:, None]
    s = jnp.where(is_past, s, -jnp.inf)
    _, top = lax.top_k(s, MB_TOPK)
    ok = top < own[:, None]
    own_b = jnp.broadcast_to(own[:, None], top.shape[:-1] + (1,)).astype(top.dtype)
    idx = jnp.concatenate([top, own_b], axis=-1)
    ok = jnp.concatenate([ok, jnp.ones_like(ok[..., :1])], axis=-1)
    return idx, ok


def moba_attend(q, q_pos, idx, ok, k_sel, v_sel):
    key_pos = idx[..., None] * MB_BLOCK + jnp.arange(MB_BLOCK)
    mask = ok[..., None] & (key_pos <= q_pos[:, None, None])
    s = jnp.einsum('bhqd,bhqnkd->bhqnk', q.astype(jnp.float32), k_sel.astype(jnp.float32)) * (HEAD_DIM ** -0.5)
    s = jnp.where(mask, s, -jnp.inf)
    B, H, Q, NS, K = s.shape
    prob = jax.nn.softmax(s.reshape(B, H, Q, NS * K), axis=-1).reshape(s.shape)
    out = jnp.einsum('bhqnk,bhqnkd->bhqd', prob, v_sel.astype(jnp.float32))
    return out.astype(q.dtype)


def moba_prompt(q, k, v):
    B, L, H, dh = q.shape
    nb = max(-(-L // MB_BLOCK), MB_TOPK + 1)
    pad = ((0, 0), (0, nb * MB_BLOCK - L), (0, 0), (0, 0))
    blocks = lambda t: jnp.pad(t, pad).reshape(B, nb, MB_BLOCK, H, dh).transpose(0, 3, 1, 2, 4)
    kb, vb = blocks(k), blocks(v)
    means = jnp.mean(kb.astype(jnp.float32), axis=3)
    nq = L // MB_QBLK
    q_blocks = q.transpose(0, 2, 1, 3).reshape(B, H, nq, MB_QBLK, dh).transpose(2, 0, 1, 3, 4)
    q_pos = jnp.arange(L, dtype=jnp.int32).reshape(nq, MB_QBLK)
    b_idx = jnp.arange(B)[:, None, None, None]
    h_idx = jnp.arange(H)[None, :, None, None]

    def one_block(args):
        qb, pb = args
        idx, ok = moba_select(qb, pb, means)
        return moba_attend(qb, pb, idx, ok, kb[b_idx, h_idx, idx], vb[b_idx, h_idx, idx])

    out = lax.map(one_block, (q_blocks, q_pos))
    return out.transpose(1, 0, 3, 2, 4).reshape(B, L, H, dh)


def gather_block_rows(pool, layer, new_rows, page_table, idx):
    B, H = idx.shape[:2]
    n_pages = page_table.shape[1]
    past = n_pages * PAGE_SIZE
    pos = idx[..., None] * MB_BLOCK + jnp.arange(MB_BLOCK)
    b_idx = jnp.arange(B)[:, None, None, None, None]
    h_idx = jnp.arange(H)[None, :, None, None, None]
    phys = page_table[b_idx, jnp.minimum(pos // PAGE_SIZE, n_pages - 1)]
    from_past = pool[layer, phys, pos % PAGE_SIZE, h_idx]
    from_new = new_rows[b_idx, jnp.clip(pos - past, 0, new_rows.shape[1] - 1), h_idx]
    return jnp.where((pos < past)[..., None], from_past, from_new)


def moba_sample(q, k, v, pool_k, pool_v, page_table, layer):
    B, L, H, dh = q.shape
    past = page_table.shape[1] * PAGE_SIZE
    k_past = pool_k[layer, page_table].reshape(B, past, H, dh).astype(k.dtype)
    nb = max(-(-(past + L) // MB_BLOCK), MB_TOPK + 1)
    k_all = jnp.pad(jnp.concatenate([k_past, k], axis=1), ((0, 0), (0, nb * MB_BLOCK - past - L), (0, 0), (0, 0)))
    means = jnp.mean(k_all.reshape(B, nb, MB_BLOCK, H, dh).astype(jnp.float32), axis=2).transpose(0, 2, 1, 3)
    qh = q.transpose(0, 2, 1, 3)
    q_pos = past + jnp.arange(L, dtype=jnp.int32)
    idx, ok = moba_select(qh, q_pos, means)
    k_sel = gather_block_rows(pool_k, layer, k, page_table, idx)
    v_sel = gather_block_rows(pool_v, layer, v, page_table, idx)
    return moba_attend(qh, q_pos, idx, ok, k_sel, v_sel).transpose(0, 2, 1, 3)


def layer_forward(x, c, p, state, pos, chunk, attend):
    mc, mn, mm, sbuf, rs, rshift, fbuf = state
    B, L, _ = x.shape
    dh = HEAD_DIM
    f32 = jnp.float32
    mod = (jax.nn.silu(c) @ p['w_ada'] + p['b_ada'])[:, None, :]
    sh_m, sc_m, gt_m, sh_f, sc_f, gt_f = jnp.split(mod, 6, axis=-1)
    h = rms_norm(x, p['norm_mix_g']) * (1.0 + sc_m) + sh_m
    sp = split_in(h @ p['w_in'])
    hd = lambda t: t.reshape(B, L, -1, dh)
    ml, mc, mn, mm = mlstm_chunked(hd(sp['m_q']), hd(sp['m_k']), hd(sp['m_v']),
                                   sp['m_i'] + p['m_bi'], sp['m_f'] + p['m_bf'], mc, mn, mm, chunk)
    y_ml = (head_rms_norm(ml, p['m_norm_g']) * jax.nn.sigmoid(sp['m_o'].astype(f32))).astype(x.dtype) @ p['m_proj']
    u, sbuf = causal_dwconv(sp['s_c'] * sp['s_h'], p['s_conv'], sbuf)
    y_sc = (sp['s_b'] * u) @ p['s_proj']
    q = rope_partial(rms_norm(hd(sp['a_q']), p['a_qnorm']), pos)
    k = rope_partial(rms_norm(hd(sp['a_k']), p['a_knorm']), pos)
    v = hd(sp['a_v'])
    y_mb = attend(q, k, v).reshape(B, L, BRANCH_WIDTH) @ p['a_proj']
    y_rw, rs, rshift = rwkv7_branch(sp['rwkv'], rshift, rs, p)
    g = jax.nn.sigmoid(sp['gates'].astype(f32)).astype(x.dtype).reshape(B, L, N_BRANCH, D_MODEL)
    merged = g[:, :, 0] * y_ml + g[:, :, 1] * y_sc + g[:, :, 2] * y_mb + g[:, :, 3] * y_rw
    x = x + gt_m * (merged @ p['w_out'])
    h = rms_norm(x, p['norm_ffn_g']) * (1.0 + sc_f) + sh_f
    ua, ul = jnp.split(h @ p['f_up'], 2, axis=-1)
    ua, fbuf = causal_dwconv(ua, p['f_conv'], fbuf)
    x = x + gt_f * ((jax.nn.silu(ua) * ul) @ p['f_down'])
    return x, (k, v), (mc, mn, mm, sbuf, rs, rshift, fbuf)


def setup_inputs(seed: int = 0) -> dict:
    key = jax.random.key(seed)
    keys = jax.random.split(key, 64)
    counter = iter(range(64))
    f32 = jnp.float32

    def nrm(shape, scale):
        return jax.random.normal(keys[next(counter)], shape, f32) * scale

    def near_one(shape):
        return 1.0 + nrm(shape, 0.05)

    n_pages = PAST_LEN // PAGE_SIZE
    n_pool = (5 * DEC_BATCH * n_pages) // 4
    perm = jax.random.permutation(keys[next(counter)], n_pool)
    page_table = perm[:DEC_BATCH * n_pages].reshape(DEC_BATCH, n_pages).astype(jnp.int32)
    Ld, D, W, dh = DEPTH, D_MODEL, BRANCH_WIDTH, HEAD_DIM
    return {
        'x_prompt': nrm((BATCH, SEQ, D), 1.0),
        'x_sample': nrm((DEC_BATCH, DEC_SEQ, D), 1.0),
        'c_prompt': nrm((BATCH, D), 1.0),
        'c_sample': nrm((DEC_BATCH, D), 1.0),
        'cache_k': nrm((Ld, n_pool, PAGE_SIZE, MB_HEADS, dh), 1.0),
        'cache_v': nrm((Ld, n_pool, PAGE_SIZE, MB_HEADS, dh), 1.0),
        'page_table': page_table,
        'state_mlstm_c': nrm((Ld, DEC_BATCH, ML_HEADS, dh, dh), 0.5),
        'state_mlstm_n': nrm((Ld, DEC_BATCH, ML_HEADS, dh), 0.5),
        'state_mlstm_m': jax.random.uniform(keys[next(counter)], (Ld, DEC_BATCH, ML_HEADS), f32, 0.0, 2.0),
        'state_conv': nrm((Ld, DEC_BATCH, CONV_WIDTH - 1, W), 0.5),
        'state_rwkv': nrm((Ld, DEC_BATCH, RW_HEADS, dh, dh), 0.3),
        'state_rwkv_shift': nrm((Ld, DEC_BATCH, RW_COLS), 1.0),
        'state_ffn_conv': nrm((Ld, DEC_BATCH, CONV_WIDTH - 1, D_FF), 1.0),
        'norm_mix_g': near_one((Ld, D)),
        'norm_ffn_g': near_one((Ld, D)),
        'w_ada': nrm((Ld, D, 6 * D), 0.3 * D ** -0.5),
        'b_ada': nrm((Ld, 6 * D), 0.05),
        'w_in': nrm((Ld, D, N_IN), D ** -0.5),
        'm_bi': nrm((Ld, ML_HEADS), 0.5) - 1.0,
        'm_bf': nrm((Ld, ML_HEADS), 0.5) + 3.0,
        'm_norm_g': near_one((Ld, W)),
        'm_proj': nrm((Ld, W, D), W ** -0.5),
        's_conv': nrm((Ld, CONV_WIDTH, W), CONV_WIDTH ** -0.5),
        's_proj': nrm((Ld, W, D), W ** -0.5),
        'a_qnorm': near_one((Ld, dh)),
        'a_knorm': near_one((Ld, dh)),
        'a_proj': nrm((Ld, W, D), W ** -0.5),
        'r_mu': jax.random.uniform(keys[next(counter)], (Ld, RW_COLS), f32, 0.0, 1.0),
        'r_w0': nrm((Ld, W), 0.5),
        'r_wB': nrm((Ld, RW_DECAY_RANK, W), RW_DECAY_RANK ** -0.5),
        'r_a0': nrm((Ld, W), 0.5),
        'r_aB': nrm((Ld, RW_A_RANK, W), RW_A_RANK ** -0.5),
        'r_gB': nrm((Ld, RW_GATE_RANK, W), RW_GATE_RANK ** -0.5),
        'r_kk': 1.0 + nrm((Ld, W), 0.1),
        'r_ka': 1.0 + nrm((Ld, W), 0.1),
        'r_rk': nrm((Ld, W), 0.1),
        'r_norm_g': near_one((Ld, W)),
        'r_proj': nrm((Ld, W, D), W ** -0.5),
        'w_out': nrm((Ld, D, D), D ** -0.5),
        'f_up': nrm((Ld, D, 2 * D_FF), D ** -0.5),
        'f_conv': nrm((Ld, CONV_WIDTH, D_FF), CONV_WIDTH ** -0.5),
        'f_down': nrm((Ld, D_FF, D), D_FF ** -0.5),
    }


def reference(x_prompt, x_sample, c_prompt, c_sample, cache_k, cache_v, page_table,
              state_mlstm_c, state_mlstm_n, state_mlstm_m, state_conv, state_rwkv, state_rwkv_shift,
              state_ffn_conv, norm_mix_g, norm_ffn_g, w_ada, b_ada, w_in, m_bi, m_bf, m_norm_g, m_proj,
              s_conv, s_proj, a_qnorm, a_knorm, a_proj, r_mu, r_w0, r_wB, r_a0, r_aB, r_gB, r_kk, r_ka,
              r_rk, r_norm_g, r_proj, w_out, f_up, f_conv, f_down):
    params = dict(norm_mix_g=norm_mix_g, norm_ffn_g=norm_ffn_g, w_ada=w_ada, b_ada=b_ada, w_in=w_in,
                  m_bi=m_bi, m_bf=m_bf, m_norm_g=m_norm_g, m_proj=m_proj, s_conv=s_conv, s_proj=s_proj,
                  a_qnorm=a_qnorm, a_knorm=a_knorm, a_proj=a_proj, r_mu=r_mu, r_w0=r_w0, r_wB=r_wB,
                  r_a0=r_a0, r_aB=r_aB, r_gB=r_gB, r_kk=r_kk, r_ka=r_ka, r_rk=r_rk, r_norm_g=r_norm_g,
                  r_proj=r_proj, w_out=w_out, f_up=f_up, f_conv=f_conv, f_down=f_down)
    B, S, _ = x_prompt.shape
    dt = x_prompt.dtype
    zero_state = (jnp.zeros((B, ML_HEADS, HEAD_DIM, HEAD_DIM), dt), jnp.zeros((B, ML_HEADS, HEAD_DIM), dt),
                  jnp.zeros((B, ML_HEADS), dt), jnp.zeros((B, CONV_WIDTH - 1, BRANCH_WIDTH), dt),
                  jnp.zeros((B, RW_HEADS, HEAD_DIM, HEAD_DIM), dt), jnp.zeros((B, RW_COLS), dt),
                  jnp.zeros((B, CONV_WIDTH - 1, D_FF), dt))
    dec_len = x_sample.shape[1]
    pos_p = jnp.arange(S, dtype=jnp.int32)
    pos_s = page_table.shape[1] * PAGE_SIZE + jnp.arange(dec_len, dtype=jnp.int32)
    hp, hs = x_prompt, x_sample
    kv_p, kv_s, st_p, st_s = [], [], [], []
    for l in range(DEPTH):
        pl = {name: w[l] for name, w in params.items()}
        hp, kv, st = layer_forward(hp, c_prompt, pl, zero_state, pos_p, min(MLSTM_CHUNK, S), moba_prompt)
        kv_p.append(kv)
        st_p.append(st)
        prev = (state_mlstm_c[l], state_mlstm_n[l], state_mlstm_m[l], state_conv[l], state_rwkv[l],
                state_rwkv_shift[l], state_ffn_conv[l])
        attend = functools.partial(moba_sample, pool_k=cache_k, pool_v=cache_v, page_table=page_table, layer=l)
        hs, kv, st = layer_forward(hs, c_sample, pl, prev, pos_s, dec_len, attend)
        kv_s.append(kv)
        st_s.append(st)
    k_prompt, v_prompt = [jnp.stack(t) for t in zip(*kv_p)]
    k_sample, v_sample = [jnp.stack(t) for t in zip(*kv_s)]
    mc_p, mn_p, mm_p, cb_p, rs_p, rh_p, fb_p = [jnp.stack(t) for t in zip(*st_p)]
    mc_s, mn_s, mm_s, cb_s, rs_s, rh_s, fb_s = [jnp.stack(t) for t in zip(*st_s)]
    return (hp, hs, k_prompt, v_prompt, k_sample, v_sample,
            mc_p, mn_p, mm_p, cb_p, rs_p, rh_p, fb_p,
            mc_s, mn_s, mm_s, cb_s, rs_s, rh_s, fb_s)
```

```python
import functools
import math

import jax
import jax.numpy as jnp
from jax import lax
from jax.experimental import pallas as pl
from jax.experimental.pallas import tpu as pltpu

F32 = jnp.float32
BF16 = jnp.bfloat16
HIGHEST = lax.Precision.HIGHEST

D_MODEL = 1024
HEAD_DIM = 64
N_HEADS = 4
BRANCH = N_HEADS * HEAD_DIM
CONV_WIDTH = 3
MB_BLOCK = 256
MB_TOPK = 3
PAGE_SIZE = 128
ROPE_DIMS = HEAD_DIM // 4
ROPE_THETA = 500000.0
RW_LORA = 128
RW_COLS = 3 * BRANCH + RW_LORA
D_FF = 11 * D_MODEL // 4
MLSTM_CHUNK = 64
NORM_EPS = 1e-6
GN_EPS = 64e-5
LANES = 128

Z_GATES = 0
Z_MLSTM = 4096
Z_SCONV = 5120
Z_MOBA = 5888
Z_MIF = 6656
Z_RWKV = 7168
Z_WIDTH = 8192

VMEM_LIMIT = 48 * 1024 * 1024


def _cparams(*sem):
    return pltpu.CompilerParams(dimension_semantics=sem, vmem_limit_bytes=VMEM_LIMIT)


def _sigmoid(x):
    return 1.0 / (1.0 + jnp.exp(-x))


def _head_blockdiag(n):
    r = lax.broadcasted_iota(jnp.int32, (n, n), 0) // HEAD_DIM
    c = lax.broadcasted_iota(jnp.int32, (n, n), 1) // HEAD_DIM
    return (r == c).astype(F32)


def _head_sum(x, bd):
    return jnp.dot(x, bd, precision=HIGHEST, preferred_element_type=F32)


def _dot_nt(a, b):
    return lax.dot_general(a, b, (((1,), (1,)), ((), ())), preferred_element_type=F32)


def _dot_tn(a, b):
    return lax.dot_general(a, b, (((0,), (0,)), ((), ())), preferred_element_type=F32)


def _pick_tile(n, cap):
    t = cap
    while n % t:
        t //= 2
    return t


def _ada_kernel(c_ref, w_ref, b_ref, o_ref):
    c = c_ref[...]
    a = (c * _sigmoid(c)).astype(BF16)
    o_ref[...] = jnp.dot(a, w_ref[...], preferred_element_type=F32) + b_ref[...]


def _ada(c_all, w, b):
    m, d = c_all.shape
    n = w.shape[1]
    tn = 1536
    return pl.pallas_call(
        _ada_kernel,
        grid=(n // tn,),
        in_specs=[pl.BlockSpec((m, d), lambda j: (0, 0)),
                  pl.BlockSpec((d, tn), lambda j: (0, j)),
                  pl.BlockSpec((1, tn), lambda j: (0, j))],
        out_specs=pl.BlockSpec((m, tn), lambda j: (0, j)),
        out_shape=jax.ShapeDtypeStruct((m, n), F32),
        compiler_params=_cparams("parallel"),
        name="ada",
    )(c_all, w, b)


def _nmm_kernel(x_ref, g_ref, sc_ref, sh_ref, w_ref, o_ref, h_sc):
    @pl.when(pl.program_id(1) == 0)
    def _():
        x = x_ref[...]
        y = x * lax.rsqrt(jnp.mean(x * x, axis=-1, keepdims=True) + NORM_EPS) * g_ref[...]
        h_sc[...] = (y * (1.0 + sc_ref[...]) + sh_ref[...]).astype(BF16)

    o_ref[...] = jnp.dot(h_sc[...], w_ref[...], preferred_element_type=F32)


def _norm_mod_matmul(x, g, sc, sh, w, grp, tn):
    n, d = x.shape
    nout = w.shape[1]
    tm, tpg, r = grp["tm"], grp["tpg"], sc.shape[1]
    return pl.pallas_call(
        _nmm_kernel,
        grid=(n // tm, nout // tn),
        in_specs=[pl.BlockSpec((tm, d), lambda i, j: (i, 0)),
                  pl.BlockSpec((1, d), lambda i, j: (0, 0)),
                  pl.BlockSpec((None, r, d), lambda i, j: (i // tpg, 0, 0)),
                  pl.BlockSpec((None, r, d), lambda i, j: (i // tpg, 0, 0)),
                  pl.BlockSpec((d, tn), lambda i, j: (0, j))],
        out_specs=pl.BlockSpec((tm, tn), lambda i, j: (i, j)),
        out_shape=jax.ShapeDtypeStruct((n, nout), F32),
        scratch_shapes=[pltpu.VMEM((tm, d), BF16)],
        compiler_params=_cparams("parallel", "arbitrary"),
        name="norm_mod_matmul",
    )(x, g, sc, sh, w)


def _log_sigmoid(x):
    return jnp.minimum(x, 0.0) - jnp.log(1.0 + jnp.exp(-jnp.abs(x)))


def _mlstm_kernel(zq_ref, zif_ref, bias_ref, g_ref, c0_ref, n0_ref, m0_ref,
                  act_ref, c_out, n_out, m_out, c_sc, n_sc, m_sc, *, chunk):
    t_ = chunk
    ci = pl.program_id(1)

    @pl.when(ci == 0)
    def _():
        c_sc[...] = c0_ref[0]
        n_sc[...] = n0_ref[0]
        m_sc[...] = m0_ref[0]

    gates = zif_ref[...] + bias_ref[...]
    logf = _log_sigmoid(gates)
    row = lax.broadcasted_iota(jnp.int32, (t_, t_), 0)
    col = lax.broadcasted_iota(jnp.int32, (t_, t_), 1)
    causal = col <= row
    eye = row == col
    row1 = lax.broadcasted_iota(jnp.int32, (t_, 1), 0)
    for h in range(N_HEADS):
        lo = h * HEAD_DIM
        q = zq_ref[:, lo:lo + HEAD_DIM]
        k = zq_ref[:, BRANCH + lo:BRANCH + lo + HEAD_DIM] * (HEAD_DIM ** -0.5)
        v = zq_ref[:, 2 * BRANCH + lo:2 * BRANCH + lo + HEAD_DIM]
        o = zq_ref[:, 3 * BRANCH + lo:3 * BRANCH + lo + HEAD_DIM]
        li_col = gates[:, h:h + 1]
        lf_col = logf[:, N_HEADS + h:N_HEADS + h + 1]
        li_row = jnp.sum(jnp.where(eye, li_col, 0.0), axis=0, keepdims=True)
        lf_row = jnp.sum(jnp.where(eye, lf_col, 0.0), axis=0, keepdims=True)
        b_col = jnp.sum(jnp.where(causal, lf_row, 0.0), axis=1, keepdims=True)
        b_row = jnp.sum(jnp.where(row <= col, lf_col, 0.0), axis=0, keepdims=True)
        m_prev = m_sc[h][:, 0:1]
        d = jnp.where(causal, b_col - b_row + li_row, -jnp.inf)
        inter = b_col + m_prev
        mt = jnp.maximum(jnp.max(d, axis=1, keepdims=True), inter)
        qb, kb, vb = q.astype(BF16), k.astype(BF16), v.astype(BF16)
        w = jnp.exp(d - mt) * _dot_nt(qb, kb)
        a_int = jnp.exp(inter - mt)
        c = c_sc[h]
        n = n_sc[h]
        num = jnp.dot(w.astype(BF16), vb, preferred_element_type=F32) \
            + a_int * jnp.dot(qb, c.astype(BF16), preferred_element_type=F32)
        den = jnp.sum(w, axis=1, keepdims=True) + a_int * jnp.sum(q * n, axis=1, keepdims=True)
        hh = num / jnp.maximum(jnp.abs(den), jnp.exp(-mt))
        hn = hh * lax.rsqrt(jnp.mean(hh * hh, axis=-1, keepdims=True) + NORM_EPS)
        act_ref[:, lo:lo + HEAD_DIM] = hn * g_ref[:, lo:lo + HEAD_DIM] * _sigmoid(o)
        b_end = jnp.sum(jnp.where(row1 == t_ - 1, b_col, 0.0), axis=0, keepdims=True)
        g_col = b_end - b_col + li_col
        m_new = jnp.maximum(b_end + m_prev, jnp.max(g_col, axis=0, keepdims=True))
        wk = jnp.exp(g_col - m_new) * k
        decay = jnp.exp(b_end + m_prev - m_new)
        c_sc[h] = decay * c + _dot_tn(wk.astype(BF16), vb)
        n_sc[h] = decay * n + jnp.sum(wk, axis=0, keepdims=True)
        m_sc[h] = jnp.broadcast_to(m_new, (1, LANES))

    @pl.when(ci == pl.num_programs(1) - 1)
    def _():
        c_out[0] = c_sc[...]
        n_out[0] = n_sc[...]
        m_out[0] = m_sc[...]


def _mlstm(z, bias_if, g, c0, n0, m0, grp):
    b_, l_ = grp["B"], grp["L"]
    t_ = min(MLSTM_CHUNK, l_)
    nc = l_ // t_
    h_ = N_HEADS
    n0 = n0.reshape(b_, h_, 1, HEAD_DIM)
    m0 = jnp.broadcast_to(m0.reshape(b_, h_, 1, 1), (b_, h_, 1, LANES))
    state_spec = lambda shp: pl.BlockSpec((1,) + shp, lambda b, c: (b, 0, 0, 0))
    act, c1, n1, m1 = pl.pallas_call(
        functools.partial(_mlstm_kernel, chunk=t_),
        grid=(b_, nc),
        in_specs=[pl.BlockSpec((t_, 4 * BRANCH), lambda b, c: (b * nc + c, Z_MLSTM // (4 * BRANCH))),
                  pl.BlockSpec((t_, LANES), lambda b, c: (b * nc + c, Z_MIF // LANES)),
                  pl.BlockSpec((1, LANES), lambda b, c: (0, 0)),
                  pl.BlockSpec((1, BRANCH), lambda b, c: (0, 0)),
                  state_spec((h_, HEAD_DIM, HEAD_DIM)),
                  state_spec((h_, 1, HEAD_DIM)),
                  state_spec((h_, 1, LANES))],
        out_specs=[pl.BlockSpec((t_, BRANCH), lambda b, c: (b * nc + c, 0)),
                   state_spec((h_, HEAD_DIM, HEAD_DIM)),
                   state_spec((h_, 1, HEAD_DIM)),
                   state_spec((h_, 1, LANES))],
        out_shape=[jax.ShapeDtypeStruct((b_ * l_, BRANCH), F32),
                   jax.ShapeDtypeStruct((b_, h_, HEAD_DIM, HEAD_DIM), F32),
                   jax.ShapeDtypeStruct((b_, h_, 1, HEAD_DIM), F32),
                   jax.ShapeDtypeStruct((b_, h_, 1, LANES), F32)],
        scratch_shapes=[pltpu.VMEM((h_, HEAD_DIM, HEAD_DIM), F32),
                        pltpu.VMEM((h_, 1, HEAD_DIM), F32),
                        pltpu.VMEM((h_, 1, LANES), F32)],
        compiler_params=_cparams("parallel", "arbitrary"),
        name="mlstm",
    )(z, z, bias_if, g, c0, n0, m0)
    return act, c1, n1.reshape(b_, h_, HEAD_DIM), m1[:, :, 0, 0]


def _conv3(p, prev2, prev1, w_ref):
    row = lax.broadcasted_iota(jnp.int32, p.shape, 0)
    p1 = jnp.where(row == 0, prev1, pltpu.roll(p, 1, axis=0))
    p2 = jnp.where(row == 0, prev2, jnp.where(row == 1, prev1, pltpu.roll(p, 2, axis=0)))
    return w_ref[0:1, :] * p2 + w_ref[1:2, :] * p1 + w_ref[2:3, :] * p


def _sconv_kernel(sb_ref, sc_ref, sh_ref, w_ref, buf_ref, out_ref, st_ref, carry, *, tm):
    @pl.when(pl.program_id(1) == 0)
    def _():
        carry[...] = buf_ref[0]

    p = sc_ref[...] * sh_ref[...]
    u = _conv3(p, carry[0:1, :], carry[1:2, :], w_ref)
    out_ref[...] = sb_ref[...] * u
    new = sc_ref[tm - 2:tm, :] * sh_ref[tm - 2:tm, :]
    carry[...] = new
    st_ref[0] = new


def _sconv(z, w, buf, grp):
    b_, l_ = grp["B"], grp["L"]
    tm = _pick_tile(l_, 512)
    nt = l_ // tm
    cb = Z_SCONV // BRANCH
    zspec = lambda j: pl.BlockSpec((tm, BRANCH), lambda b, i: (b * nt + i, cb + j))
    return pl.pallas_call(
        functools.partial(_sconv_kernel, tm=tm),
        grid=(b_, nt),
        in_specs=[zspec(0), zspec(1), zspec(2),
                  pl.BlockSpec((CONV_WIDTH, BRANCH), lambda b, i: (0, 0)),
                  pl.BlockSpec((1, CONV_WIDTH - 1, BRANCH), lambda b, i: (b, 0, 0))],
        out_specs=[pl.BlockSpec((tm, BRANCH), lambda b, i: (b * nt + i, 0)),
                   pl.BlockSpec((1, CONV_WIDTH - 1, BRANCH), lambda b, i: (b, 0, 0))],
        out_shape=[jax.ShapeDtypeStruct((b_ * l_, BRANCH), F32),
                   jax.ShapeDtypeStruct((b_, CONV_WIDTH - 1, BRANCH), F32)],
        scratch_shapes=[pltpu.VMEM((CONV_WIDTH - 1, BRANCH), F32)],
        compiler_params=_cparams("parallel", "arbitrary"),
        name="sconv",
    )(z, z, z, w, buf)


def _qk_prep_kernel(q_ref, k_ref, qg_ref, kg_ref, cos_ref, sa_ref, sb_ref, qo_ref, ko_ref):
    bd = _head_blockdiag(BRANCH)
    cos, sa, sb = cos_ref[...], sa_ref[...], sb_ref[...]
    half = ROPE_DIMS // 2

    def prep(x, g):
        y = x * lax.rsqrt(_head_sum(x * x, bd) * (1.0 / HEAD_DIM) + NORM_EPS) * g
        return y * cos + pltpu.roll(y, BRANCH - half, axis=1) * sa + pltpu.roll(y, half, axis=1) * sb

    qo_ref[...] = prep(q_ref[...], qg_ref[...])
    ko_ref[...] = prep(k_ref[...], kg_ref[...])


def _rope_tables(pos):
    half = ROPE_DIMS // 2
    inv = jnp.exp(-math.log(ROPE_THETA) * jnp.arange(0, ROPE_DIMS, 2, dtype=F32) / ROPE_DIMS)
    ang = pos.astype(F32)[:, None] * inv[None, :]
    cos, sin = jnp.cos(ang), jnp.sin(ang)
    n = pos.shape[0]
    rest = HEAD_DIM - ROPE_DIMS
    c_h = jnp.concatenate([cos, cos, jnp.ones((n, rest), F32)], axis=1)
    sa_h = jnp.concatenate([-sin, jnp.zeros((n, half + rest), F32)], axis=1)
    sb_h = jnp.concatenate([jnp.zeros((n, half), F32), sin, jnp.zeros((n, rest), F32)], axis=1)
    tile = lambda t: jnp.tile(t, (1, N_HEADS))
    return tile(c_h), tile(sa_h), tile(sb_h)


def _qk_prep(z, qg, kg, tables, grp):
    n = grp["B"] * grp["L"]
    ltab = tables[0].shape[0]
    tm = _pick_tile(ltab, 512)
    npos = ltab // tm
    cb = Z_MOBA // BRANCH
    tspec = pl.BlockSpec((tm, BRANCH), lambda i: (i % npos, 0))
    gspec = pl.BlockSpec((1, BRANCH), lambda i: (0, 0))
    ospec = pl.BlockSpec((tm, BRANCH), lambda i: (i, 0))
    return pl.pallas_call(
        _qk_prep_kernel,
        grid=(n // tm,),
        in_specs=[pl.BlockSpec((tm, BRANCH), lambda i: (i, cb)),
                  pl.BlockSpec((tm, BRANCH), lambda i: (i, cb + 1)),
                  gspec, gspec, tspec, tspec, tspec],
        out_specs=[ospec, ospec],
        out_shape=[jax.ShapeDtypeStruct((n, BRANCH), F32)] * 2,
        compiler_params=_cparams("parallel"),
        name="qk_prep",
    )(z, z, qg, kg, *tables)


def _moba_prompt_kernel(q_ref, k_ref, v_ref, o_ref, kmean, *, nb):
    qi = pl.program_id(1)
    tq = MB_BLOCK

    @pl.when(qi == 0)
    def _():
        kmean[...] = jnp.zeros_like(kmean)
        for n in range(nb):
            kmean[n:n + 1, :] = jnp.mean(k_ref[n * MB_BLOCK:(n + 1) * MB_BLOCK, :], axis=0, keepdims=True)

    lane = lax.broadcasted_iota(jnp.int32, (tq, LANES), 1)
    row = lax.broadcasted_iota(jnp.int32, (tq, tq), 0)
    col = lax.broadcasted_iota(jnp.int32, (tq, tq), 1)
    scale = HEAD_DIM ** -0.5
    own0 = pl.multiple_of(qi * MB_BLOCK, MB_BLOCK)
    for h in range(N_HEADS):
        hs = slice(h * HEAD_DIM, (h + 1) * HEAD_DIM)
        qh = q_ref[:, hs]
        qb = qh.astype(BF16)
        s = lax.dot_general(qh, kmean[:, hs], (((1,), (1,)), ((), ())),
                            precision=HIGHEST, preferred_element_type=F32)
        valid = lane < qi
        s = jnp.where(valid, s, -jnp.inf)
        rank = jnp.zeros((tq, LANES), jnp.int32)
        for m in range(nb):
            sm = s[:, m:m + 1]
            rank += ((sm > s) | ((sm == s) & (m < lane))).astype(jnp.int32)
        sel = (valid & (rank < MB_TOPK)).astype(F32)
        kb = k_ref[pl.ds(own0, MB_BLOCK), hs].astype(BF16)
        vb = v_ref[pl.ds(own0, MB_BLOCK), hs].astype(BF16)
        sc = jnp.where(col <= row, _dot_nt(qb, kb) * scale, -jnp.inf)
        m_i = jnp.max(sc, axis=1, keepdims=True)
        p = jnp.exp(sc - m_i)
        l_i = jnp.sum(p, axis=1, keepdims=True)
        acc = jnp.dot(p.astype(BF16), vb, preferred_element_type=F32)

        def body(n, carry):
            m_i, l_i, acc = carry
            k0 = pl.multiple_of(n * MB_BLOCK, MB_BLOCK)
            kb = k_ref[pl.ds(k0, MB_BLOCK), hs].astype(BF16)
            vb = v_ref[pl.ds(k0, MB_BLOCK), hs].astype(BF16)
            seln = jnp.sum(jnp.where(lane == n, sel, 0.0), axis=1, keepdims=True) > 0.0
            sc = jnp.where(seln, _dot_nt(qb, kb) * scale, -jnp.inf)
            m_new = jnp.maximum(m_i, jnp.max(sc, axis=1, keepdims=True))
            alpha = jnp.exp(m_i - m_new)
            p = jnp.exp(sc - m_new)
            l_new = alpha * l_i + jnp.sum(p, axis=1, keepdims=True)
            acc_new = alpha * acc + jnp.dot(p.astype(BF16), vb, preferred_element_type=F32)
            return m_new, l_new, acc_new

        m_i, l_i, acc = lax.fori_loop(0, qi, body, (m_i, l_i, acc))
        o_ref[:, hs] = acc / l_i


def _moba_prompt(q, k, z, grp):
    b_, l_ = grp["B"], grp["L"]
    assert l_ % MB_BLOCK == 0 and l_ // MB_BLOCK <= LANES
    nb = l_ // MB_BLOCK
    return pl.pallas_call(
        functools.partial(_moba_prompt_kernel, nb=nb),
        grid=(b_, nb),
        in_specs=[pl.BlockSpec((MB_BLOCK, BRANCH), lambda b, i: (b * nb + i, 0)),
                  pl.BlockSpec((l_, BRANCH), lambda b, i: (b, 0)),
                  pl.BlockSpec((l_, BRANCH), lambda b, i: (b, Z_MOBA // BRANCH + 2))],
        out_specs=pl.BlockSpec((MB_BLOCK, BRANCH), lambda b, i: (b * nb + i, 0)),
        out_shape=jax.ShapeDtypeStruct((b_ * l_, BRANCH), F32),
        scratch_shapes=[pltpu.VMEM((LANES, BRANCH), F32)],
        compiler_params=_cparams("parallel", "arbitrary"),
        name="moba_prompt",
    )(q, k, z)


def _moba_sample_kernel(pt_ref, q_ref, kn_ref, vn_ref, k0_ref, k1_ref, v0_ref, v1_ref, o_ref,
                        kmean_s, m_s, l_s, o_s, *, nblk, dec):
    n = pl.program_id(1)
    scale = HEAD_DIM ** -0.5
    lane = lax.broadcasted_iota(jnp.int32, (dec, LANES), 1)
    lane_d = lax.broadcasted_iota(jnp.int32, (HEAD_DIM, LANES), 1)

    @pl.when(n == 0)
    def _():
        kmean_s[...] = jnp.zeros_like(kmean_s)
        m_s[...] = jnp.full_like(m_s, -jnp.inf)
        l_s[...] = jnp.zeros_like(l_s)

    q = q_ref[...]
    hit = lane == n
    for h in range(N_HEADS):
        hs = slice(h * HEAD_DIM, (h + 1) * HEAD_DIM)
        kt = jnp.concatenate([k0_ref[h], k1_ref[h]], axis=1)
        vt = jnp.concatenate([v0_ref[h], v1_ref[h]], axis=1)
        s = jnp.dot(q[:, hs].astype(BF16), kt.astype(BF16), preferred_element_type=F32) * scale
        mx = jnp.max(s, axis=1, keepdims=True)
        p = jnp.exp(s - mx)
        l = jnp.sum(p, axis=1, keepdims=True)
        o_s[h, n] = _dot_nt(p.astype(BF16), vt.astype(BF16))
        kmean_s[h] = jnp.where(lane_d == n, jnp.mean(kt, axis=1, keepdims=True), kmean_s[h])
        m_s[h] = jnp.where(hit, mx, m_s[h])
        l_s[h] = jnp.where(hit, l, l_s[h])

    @pl.when(n == nblk - 1)
    def _():
        zpad = jnp.zeros((LANES - dec, BRANCH), F32)
        knew = jnp.concatenate([kn_ref[...], zpad], axis=0)
        vnew = jnp.concatenate([vn_ref[...], zpad], axis=0)
        rowq = lax.broadcasted_iota(jnp.int32, (dec, LANES), 0)
        lane_f = lane.astype(F32)
        for h in range(N_HEADS):
            hs = slice(h * HEAD_DIM, (h + 1) * HEAD_DIM)
            qh = q[:, hs]
            s = jnp.dot(qh, kmean_s[h], precision=HIGHEST, preferred_element_type=F32)
            s = jnp.where(lane < nblk, s, -jnp.inf)
            sel = jnp.zeros((dec, LANES), jnp.bool_)
            for _ in range(MB_TOPK):
                mxv = jnp.max(s, axis=1, keepdims=True)
                idx = jnp.min(jnp.where(s == mxv, lane_f, 2.0 * LANES), axis=1, keepdims=True)
                pick = lane_f == idx
                sel = sel | pick
                s = jnp.where(pick, -jnp.inf, s)
            sel = sel & (lane < nblk)
            s_own = jnp.where(lane <= rowq, _dot_nt(qh.astype(BF16), knew[:, hs].astype(BF16)) * scale, -jnp.inf)
            m_own = jnp.max(s_own, axis=1, keepdims=True)
            p_own = jnp.exp(s_own - m_own)
            l_own = jnp.sum(p_own, axis=1, keepdims=True)
            o_own = jnp.dot(p_own.astype(BF16), vnew[:, hs].astype(BF16), preferred_element_type=F32)
            mb = m_s[h]
            mtot = jnp.maximum(jnp.max(jnp.where(sel, mb, -jnp.inf), axis=1, keepdims=True), m_own)
            wgt = jnp.where(sel, jnp.exp(mb - mtot), 0.0)
            a_own = jnp.exp(m_own - mtot)
            ltot = jnp.sum(wgt * l_s[h], axis=1, keepdims=True) + l_own * a_own
            acc = o_own * a_own
            for j in range(nblk):
                acc = acc + wgt[:, j:j + 1] * o_s[h, j]
            o_ref[:, hs] = acc / ltot


def _moba_sample(q, k, v, pool_k, pool_v, page_table, layer, grp):
    b_, dec = grp["B"], grp["L"]
    n_pages = page_table.shape[1]
    ppb = MB_BLOCK // PAGE_SIZE
    assert n_pages % ppb == 0 and ppb == 2
    nblk = n_pages // ppb
    assert MB_TOPK <= nblk <= LANES and dec <= MB_BLOCK
    pool_k = pool_k.transpose(0, 1, 3, 4, 2)
    pool_v = pool_v.transpose(0, 1, 3, 4, 2)
    pt = page_table.reshape(-1)
    page = lambda j: pl.BlockSpec((None, None, N_HEADS, HEAD_DIM, PAGE_SIZE),
                                  lambda b, n, pt: (layer, pt[b * n_pages + ppb * n + j], 0, 0, 0))
    new = pl.BlockSpec((dec, BRANCH), lambda b, n, pt: (b, 0))
    return pl.pallas_call(
        functools.partial(_moba_sample_kernel, nblk=nblk, dec=dec),
        grid_spec=pltpu.PrefetchScalarGridSpec(
            num_scalar_prefetch=1,
            grid=(b_, nblk),
            in_specs=[new, new, new, page(0), page(1), page(0), page(1)],
            out_specs=new,
            scratch_shapes=[pltpu.VMEM((N_HEADS, HEAD_DIM, LANES), F32),
                            pltpu.VMEM((N_HEADS, dec, LANES), F32),
                            pltpu.VMEM((N_HEADS, dec, LANES), F32),
                            pltpu.VMEM((N_HEADS, nblk, dec, HEAD_DIM), F32)]),
        out_shape=jax.ShapeDtypeStruct((b_ * dec, BRANCH), F32),
        compiler_params=_cparams("parallel", "arbitrary"),
        name="moba_sample",
    )(pt, q, k, v, pool_k, pool_k, pool_v, pool_v)


def _rwkv_prep_kernel(z_ref, prev_ref, mu_ref, wlr_ref, w0_ref, a0_ref, kks_ref, kas_ref, rk_ref,
                      r_o, w_o, k_o, v_o, kk_o, kka_o, bonus_o, g_o, carry, *, tm):
    @pl.when(pl.program_id(1) == 0)
    def _():
        carry[...] = prev_ref[0]

    z = z_ref[...]
    row = lax.broadcasted_iota(jnp.int32, z.shape, 0)
    zs = jnp.where(row == 0, carry[...], pltpu.roll(z, 1, axis=0))
    carry[...] = z_ref[tm - 1:tm, :]
    zz = z + mu_ref[...] * (zs - z)
    r = zz[:, 0:BRANCH]
    k = zz[:, BRANCH:2 * BRANCH]
    v = zz[:, 2 * BRANCH:3 * BRANCH]
    lr = zz[:, 3 * BRANCH:3 * BRANCH + RW_LORA]
    lane = lax.broadcasted_iota(jnp.int32, lr.shape, 1)
    lr_in = jnp.where(lane < 32, jnp.tanh(lr), jnp.where(lane < 64, lr, _sigmoid(lr)))
    lo = jnp.dot(lr_in.astype(BF16), wlr_ref[...], preferred_element_type=F32)
    decay = jnp.exp(-math.exp(-0.5) * _sigmoid(w0_ref[...] + lo[:, 0:BRANCH]))
    a = _sigmoid(a0_ref[...] + lo[:, BRANCH:2 * BRANCH])
    bd = _head_blockdiag(BRANCH)
    kk = k * kks_ref[...]
    kk = kk * lax.rsqrt(jnp.maximum(_head_sum(kk * kk, bd), 1e-24))
    k2 = k * (1.0 + (a - 1.0) * kas_ref[...])
    r_o[...] = r
    w_o[...] = decay
    k_o[...] = k2
    v_o[...] = v
    kk_o[...] = kk
    kka_o[...] = kk * a
    bonus_o[...] = _head_sum(r * k2 * rk_ref[...], bd) * v
    g_o[...] = lo[:, 2 * BRANCH:3 * BRANCH]


def _rwkv_prep(z, prev, p, grp):
    b_, l_ = grp["B"], grp["L"]
    tm = _pick_tile(l_, 256)
    nt = l_ // tm
    zw = Z_WIDTH - Z_RWKV
    vec = pl.BlockSpec((1, BRANCH), lambda b, i: (0, 0))
    ospec = pl.BlockSpec((tm, BRANCH), lambda b, i: (b * nt + i, 0))
    return pl.pallas_call(
        functools.partial(_rwkv_prep_kernel, tm=tm),
        grid=(b_, nt),
        in_specs=[pl.BlockSpec((tm, zw), lambda b, i: (b * nt + i, Z_RWKV // zw)),
                  pl.BlockSpec((1, 1, zw), lambda b, i: (b, 0, 0)),
                  pl.BlockSpec((1, zw), lambda b, i: (0, 0)),
                  pl.BlockSpec((RW_LORA, 3 * BRANCH), lambda b, i: (0, 0)),
                  vec, vec, vec, vec, vec],
        out_specs=[ospec] * 8,
        out_shape=[jax.ShapeDtypeStruct((b_ * l_, BRANCH), F32)] * 8,
        scratch_shapes=[pltpu.VMEM((1, zw), F32)],
        compiler_params=_cparams("parallel", "arbitrary"),
        name="rwkv_prep",
    )(z, prev, p["mu"], p["wlr"], p["w0"], p["a0"], p["kk"], p["ka"], p["rk"])


def _rwkv_scan_kernel(r_ref, w_ref, k_ref, v_ref, kk_ref, kka_ref, s0_ref, y_ref, s_out, s_sc, *, bb, chunk):
    ci = pl.program_id(1)

    @pl.when(ci == 0)
    def _():
        s_sc[...] = s0_ref[...]

    eye = (lax.broadcasted_iota(jnp.int32, (HEAD_DIM, HEAD_DIM), 0)
           == lax.broadcasted_iota(jnp.int32, (HEAD_DIM, HEAD_DIM), 1))

    sub = 8
    row8 = lax.broadcasted_iota(jnp.int32, (sub, HEAD_DIM), 0)

    def step(t8, carry):
        t0 = pl.multiple_of(t8 * sub, sub)
        for b in range(bb):
            for h in range(N_HEADS):
                hs = slice(h * HEAD_DIM, (h + 1) * HEAD_DIM)
                tile = lambda ref: ref[b, pl.ds(t0, sub), hs]
                r8, w8, k8, v8, kk8, kka8 = (tile(ref) for ref in (r_ref, w_ref, k_ref, v_ref, kk_ref, kka_ref))
                s = s_sc[b, h]
                y8 = jnp.zeros((sub, HEAD_DIM), F32)
                for j in range(sub):
                    row = lambda t: t[j:j + 1, :]
                    s_kk = jnp.sum(s * row(kk8), axis=1, keepdims=True)
                    v_col = jnp.sum(jnp.where(eye, row(v8), 0.0), axis=1, keepdims=True)
                    s = s * row(w8) - s_kk * row(kka8) + v_col * row(k8)
                    y_col = jnp.sum(s * row(r8), axis=1, keepdims=True)
                    y_row = jnp.sum(jnp.where(eye, y_col, 0.0), axis=0, keepdims=True)
                    y8 = jnp.where(row8 == j, y_row, y8)
                s_sc[b, h] = s
                y_ref[b, pl.ds(t0, sub), hs] = y8
        return carry

    lax.fori_loop(0, chunk // sub, step, 0)

    @pl.when(ci == pl.num_programs(1) - 1)
    def _():
        s_out[...] = s_sc[...]


def _rwkv_scan(r, w, k, v, kk, kka, s0, grp):
    b_, l_ = grp["B"], grp["L"]
    bb = min(8, b_)
    chunk = _pick_tile(l_, 128)
    nc = l_ // chunk
    seq = pl.BlockSpec((bb, chunk, BRANCH), lambda g, c: (g, c, 0))
    st = pl.BlockSpec((bb, N_HEADS, HEAD_DIM, HEAD_DIM), lambda g, c: (g, 0, 0, 0))
    as3 = lambda t: t.reshape(b_, l_, BRANCH)
    y, s1 = pl.pallas_call(
        functools.partial(_rwkv_scan_kernel, bb=bb, chunk=chunk),
        grid=(b_ // bb, nc),
        in_specs=[seq] * 6 + [st],
        out_specs=[seq, st],
        out_shape=[jax.ShapeDtypeStruct((b_, l_, BRANCH), F32),
                   jax.ShapeDtypeStruct((b_, N_HEADS, HEAD_DIM, HEAD_DIM), F32)],
        scratch_shapes=[pltpu.VMEM((bb, N_HEADS, HEAD_DIM, HEAD_DIM), F32)],
        compiler_params=_cparams("parallel", "arbitrary"),
        name="rwkv_scan",
    )(as3(r), as3(w), as3(k), as3(v), as3(kk), as3(kka), s0)
    return y.reshape(b_ * l_, BRANCH), s1


def _merge_kernel(ml_ref, sc_ref, mb_ref, y_ref, bonus_ref, g_ref, gn_ref, gates_ref, x_ref, gt_ref,
                  mp_ref, sp_ref, ap_ref, rp_ref, wo_ref, o_ref):
    bd = _head_blockdiag(BRANCH)
    y = y_ref[...]
    yc = y - _head_sum(y, bd) * (1.0 / HEAD_DIM)
    yn = yc * lax.rsqrt(_head_sum(yc * yc, bd) * (1.0 / HEAD_DIM) + GN_EPS)
    rw = (yn * gn_ref[...] + bonus_ref[...]) * g_ref[...]
    acts = (ml_ref[...], sc_ref[...], mb_ref[...], rw)
    projs = (mp_ref, sp_ref, ap_ref, rp_ref)
    merged = None
    for j in range(4):
        br = jnp.dot(acts[j].astype(BF16), projs[j][...], preferred_element_type=F32)
        term = _sigmoid(gates_ref[:, j * D_MODEL:(j + 1) * D_MODEL]) * br
        merged = term if merged is None else merged + term
    o_ref[...] = x_ref[...] + gt_ref[...] * jnp.dot(merged.astype(BF16), wo_ref[...], preferred_element_type=F32)


def _merge(ml, sc, mb, y, bonus, g, gn, z, x, gt, projs, w_out, grp):
    n, d = x.shape
    tm = min(256, grp["tm"])
    tpg = grp["tpg"] * (grp["tm"] // tm)
    r = gt.shape[1]
    if r > 1:
        assert r == grp["tm"] and tpg == grp["tm"] // tm
        gt = gt.reshape(r // tm, tm, d)
        gt_spec = pl.BlockSpec((None, tm, d), lambda i: (i, 0, 0))
    else:
        gt_spec = pl.BlockSpec((None, 1, d), lambda i: (i // tpg, 0, 0))
    act = pl.BlockSpec((tm, BRANCH), lambda i: (i, 0))
    proj = pl.BlockSpec((BRANCH, d), lambda i: (0, 0))
    return pl.pallas_call(
        _merge_kernel,
        grid=(n // tm,),
        in_specs=[act] * 6 + [pl.BlockSpec((1, BRANCH), lambda i: (0, 0)),
                              pl.BlockSpec((tm, 4 * d), lambda i: (i, 0)),
                              pl.BlockSpec((tm, d), lambda i: (i, 0)),
                              gt_spec, proj, proj, proj, proj,
                              pl.BlockSpec((d, d), lambda i: (0, 0))],
        out_specs=pl.BlockSpec((tm, d), lambda i: (i, 0)),
        out_shape=jax.ShapeDtypeStruct((n, d), F32),
        compiler_params=_cparams("parallel"),
        name="merge",
    )(ml, sc, mb, y, bonus, g, gn, z, x, gt, *projs, w_out)


def _ffn_act_kernel(ua_ref, ul_ref, w_ref, buf_ref, o_ref, carry, *, tm):
    @pl.when(pl.program_id(1) == 0)
    def _():
        carry[...] = buf_ref[0]

    ua = ua_ref[...]
    u = _conv3(ua, carry[0:1, :], carry[1:2, :], w_ref)
    carry[...] = ua_ref[tm - 2:tm, :]
    o_ref[...] = (u * _sigmoid(u) * ul_ref[...]).astype(BF16)


def _ffn_act(up, w, buf, grp):
    b_, l_ = grp["B"], grp["L"]
    tm = _pick_tile(l_, 256)
    nt = l_ // tm
    half = lambda j: pl.BlockSpec((tm, D_FF), lambda b, i: (b * nt + i, j))
    return pl.pallas_call(
        functools.partial(_ffn_act_kernel, tm=tm),
        grid=(b_, nt),
        in_specs=[half(0), half(1),
                  pl.BlockSpec((CONV_WIDTH, D_FF), lambda b, i: (0, 0)),
                  pl.BlockSpec((1, CONV_WIDTH - 1, D_FF), lambda b, i: (b, 0, 0))],
        out_specs=pl.BlockSpec((tm, D_FF), lambda b, i: (b * nt + i, 0)),
        out_shape=jax.ShapeDtypeStruct((b_ * l_, D_FF), BF16),
        scratch_shapes=[pltpu.VMEM((CONV_WIDTH - 1, D_FF), F32)],
        compiler_params=_cparams("parallel", "arbitrary"),
        name="ffn_act",
    )(up, up, w, buf)


def _mm_res_kernel(a_ref, w_ref, x_ref, gt_ref, o_ref):
    o_ref[...] = x_ref[...] + gt_ref[...] * jnp.dot(a_ref[...], w_ref[...], preferred_element_type=F32)


def _mm_res(a, w, x, gt, grp):
    n, d = x.shape
    kdim = a.shape[1]
    tm, tpg, r = min(512, grp["tm"]), grp["tpg"] * (grp["tm"] // min(512, grp["tm"])), gt.shape[1]
    if r > 1:
        gt = gt.reshape(r // tm, tm, d)
        gt_spec = pl.BlockSpec((None, tm, d), lambda i: (i, 0, 0))
    else:
        gt_spec = pl.BlockSpec((None, 1, d), lambda i: (i // tpg, 0, 0))
    return pl.pallas_call(
        _mm_res_kernel,
        grid=(n // tm,),
        in_specs=[pl.BlockSpec((tm, kdim), lambda i: (i, 0)),
                  pl.BlockSpec((kdim, d), lambda i: (0, 0)),
                  pl.BlockSpec((tm, d), lambda i: (i, 0)),
                  gt_spec],
        out_specs=pl.BlockSpec((tm, d), lambda i: (i, 0)),
        out_shape=jax.ShapeDtypeStruct((n, d), F32),
        compiler_params=_cparams("parallel"),
        name="mm_res",
    )(a, w, x, gt)


def _prep_layer(l, P):
    d = D_MODEL
    w_in = P["w_in"][l]
    o_mif = 4 * BRANCH
    o_sb = o_mif + 2 * N_HEADS
    o_aq = o_sb + 3 * BRANCH
    o_rw = o_aq + 3 * BRANCH
    o_g = o_rw + RW_COLS
    w_in_p = jnp.concatenate([
        w_in[:, o_g:o_g + 4 * d], w_in[:, 0:o_mif], w_in[:, o_sb:o_aq], w_in[:, o_aq:o_rw],
        w_in[:, o_mif:o_sb], jnp.zeros((d, Z_RWKV - Z_MIF - 2 * N_HEADS), F32),
        w_in[:, o_rw:o_g], jnp.zeros((d, Z_WIDTH - Z_RWKV - RW_COLS), F32)], axis=1).astype(BF16)
    bias_if = jnp.concatenate([P["m_bi"][l], P["m_bf"][l], jnp.zeros((LANES - 2 * N_HEADS,), F32)])[None, :]
    zw = Z_WIDTH - Z_RWKV
    wlr = jnp.zeros((RW_LORA, 3 * BRANCH), F32)
    wlr = wlr.at[0:32, 0:BRANCH].set(P["r_wB"][l])
    wlr = wlr.at[32:64, BRANCH:2 * BRANCH].set(P["r_aB"][l])
    wlr = wlr.at[64:128, 2 * BRANCH:].set(P["r_gB"][l])
    row = lambda t: t[None, :]
    rw = dict(mu=jnp.pad(P["r_mu"][l], (0, zw - RW_COLS))[None, :], wlr=wlr.astype(BF16),
              w0=row(P["r_w0"][l]), a0=row(P["r_a0"][l]), kk=row(P["r_kk"][l]), ka=row(P["r_ka"][l]),
              rk=row(P["r_rk"][l]))
    bf = lambda name: P[name][l].astype(BF16)
    return dict(
        w_ada=bf("w_ada"), b_ada=row(P["b_ada"][l]), w_in=w_in_p, bias_if=bias_if,
        norm_mix_g=row(P["norm_mix_g"][l]), norm_ffn_g=row(P["norm_ffn_g"][l]),
        m_norm_g=row(P["m_norm_g"][l]), s_conv=P["s_conv"][l],
        qg=row(jnp.tile(P["a_qnorm"][l], N_HEADS)), kg=row(jnp.tile(P["a_knorm"][l], N_HEADS)),
        rw=rw, r_norm_g=row(P["r_norm_g"][l]),
        projs=(bf("m_proj"), bf("s_proj"), bf("a_proj"), bf("r_proj")), w_out=bf("w_out"),
        f_up=bf("f_up"), f_conv=P["f_conv"][l], f_down=bf("f_down"))


def _layer(x, mod, W, state, tables, attend, grp):
    mc, mn, mm, sbuf, rs, rshift, fbuf = state
    b_, l_ = grp["B"], grp["L"]
    sh_m, sc_m, gt_m, sh_f, sc_f, gt_f = mod
    z = _norm_mod_matmul(x, W["norm_mix_g"], sc_m, sh_m, W["w_in"], grp, 1024)
    ml, mc, mn, mm = _mlstm(z, W["bias_if"], W["m_norm_g"], mc, mn, mm, grp)
    sc, sbuf = _sconv(z, W["s_conv"], sbuf, grp)
    q, k = _qk_prep(z, W["qg"], W["kg"], tables, grp)
    v = z[:, Z_MOBA + 2 * BRANCH:Z_MOBA + 3 * BRANCH]
    mb = attend(q, k, z, v)
    zw = Z_WIDTH - Z_RWKV
    prev = jnp.pad(rshift, ((0, 0), (0, zw - RW_COLS)))[:, None, :]
    r, w, k2, vv, kk, kka, bonus, g = _rwkv_prep(z, prev, W["rw"], grp)
    y, rs_t = _rwkv_scan(r, w, k2, vv, kk, kka, rs, grp)
    rshift = z.reshape(b_, l_, Z_WIDTH)[:, -1, Z_RWKV:Z_RWKV + RW_COLS]
    x = _merge(ml, sc, mb, y, bonus, g, W["r_norm_g"], z, x, gt_m, W["projs"], W["w_out"], grp)
    up = _norm_mod_matmul(x, W["norm_ffn_g"], sc_f, sh_f, W["f_up"], grp, 1408)
    act = _ffn_act(up, W["f_conv"], fbuf, grp)
    fbuf = up.reshape(b_, l_, 2 * D_FF)[:, l_ - (CONV_WIDTH - 1):, :D_FF]
    x = _mm_res(act, W["f_down"], x, gt_f, grp)
    return x, (k, v), (mc, mn, mm, sbuf, rs_t, rshift, fbuf)


def kernel(x_prompt, x_sample, c_prompt, c_sample, cache_k, cache_v, page_table, state_mlstm_c, state_mlstm_n, state_mlstm_m, state_conv, state_rwkv, state_rwkv_shift, state_ffn_conv, norm_mix_g, norm_ffn_g, w_ada, b_ada, w_in, m_bi, m_bf, m_norm_g, m_proj, s_conv, s_proj, a_qnorm, a_knorm, a_proj, r_mu, r_w0, r_wB, r_a0, r_aB, r_gB, r_kk, r_ka, r_rk, r_norm_g, r_proj, w_out, f_up, f_conv, f_down):
    P = dict(norm_mix_g=norm_mix_g, norm_ffn_g=norm_ffn_g, w_ada=w_ada, b_ada=b_ada, w_in=w_in,
             m_bi=m_bi, m_bf=m_bf, m_norm_g=m_norm_g, m_proj=m_proj, s_conv=s_conv, s_proj=s_proj,
             a_qnorm=a_qnorm, a_knorm=a_knorm, a_proj=a_proj, r_mu=r_mu, r_w0=r_w0, r_wB=r_wB,
             r_a0=r_a0, r_aB=r_aB, r_gB=r_gB, r_kk=r_kk, r_ka=r_ka, r_rk=r_rk, r_norm_g=r_norm_g,
             r_proj=r_proj, w_out=w_out, f_up=f_up, f_conv=f_conv, f_down=f_down)
    depth = w_in.shape[0]
    bp, lp, d = x_prompt.shape
    bs, ls, _ = x_sample.shape
    n_s = bs * ls
    past = page_table.shape[1] * PAGE_SIZE
    tm_p = _pick_tile(lp, 1024)
    grp_p = dict(B=bp, L=lp, tm=tm_p, tpg=lp // tm_p)
    grp_s = dict(B=bs, L=ls, tm=n_s, tpg=1)
    assert n_s % 8 == 0 and n_s <= 1024

    zeros = lambda *s: jnp.zeros(s, F32)
    st_p = (zeros(bp, N_HEADS, HEAD_DIM, HEAD_DIM), zeros(bp, N_HEADS, HEAD_DIM), zeros(bp, N_HEADS),
            zeros(bp, CONV_WIDTH - 1, BRANCH), zeros(bp, N_HEADS, HEAD_DIM, HEAD_DIM), zeros(bp, RW_COLS),
            zeros(bp, CONV_WIDTH - 1, D_FF))
    tab_p = _rope_tables(jnp.arange(lp, dtype=jnp.int32))
    tab_s = tuple(jnp.tile(t, (bs, 1)) for t in _rope_tables(past + jnp.arange(ls, dtype=jnp.int32)))

    c_all = jnp.concatenate([c_prompt, c_sample], axis=0)
    hp = x_prompt.reshape(bp * lp, d)
    hs = x_sample.reshape(n_s, d)
    kv_p, kv_s, sts_p, sts_s = [], [], [], []
    for l in range(depth):
        W = _prep_layer(l, P)
        mod = _ada(c_all, W["w_ada"], W["b_ada"])
        mods = [mod[:, j * d:(j + 1) * d] for j in range(6)]
        mod_p = [m[:bp][:, None, :] for m in mods]
        mod_s = [jnp.repeat(m[bp:], ls, axis=0)[None] for m in mods]

        attend_p = lambda q, k, z, v: _moba_prompt(q, k, z, grp_p)
        hp, kv, st = _layer(hp, mod_p, W, st_p, tab_p, attend_p, grp_p)
        kv_p.append(kv)
        sts_p.append(st)

        prev = (state_mlstm_c[l], state_mlstm_n[l], state_mlstm_m[l], state_conv[l], state_rwkv[l],
                state_rwkv_shift[l], state_ffn_conv[l])
        attend_s = lambda q, k, z, v: _moba_sample(q, k, v, cache_k, cache_v, page_table, l, grp_s)
        hs, kv, st = _layer(hs, mod_s, W, prev, tab_s, attend_s, grp_s)
        kv_s.append(kv)
        sts_s.append(st)

    heads = lambda t, b_, l_: t.reshape(b_, l_, N_HEADS, HEAD_DIM)
    k_prompt = jnp.stack([heads(k, bp, lp) for k, _ in kv_p])
    v_prompt = jnp.stack([heads(v, bp, lp) for _, v in kv_p])
    k_sample = jnp.stack([heads(k, bs, ls) for k, _ in kv_s])
    v_sample = jnp.stack([heads(v, bs, ls) for _, v in kv_s])
    stack = lambda sts: [jnp.stack(t) for t in zip(*sts)]
    return (hp.reshape(bp, lp, d), hs.reshape(bs, ls, d), k_prompt, v_prompt, k_sample, v_sample,
            *stack(sts_p), *stack(sts_s))
```

```python
import functools
import math

import jax
import jax.numpy as jnp
from jax import lax
from jax.experimental import pallas as pl
from jax.experimental.pallas import tpu as pltpu

F32 = jnp.float32
BF16 = jnp.bfloat16
HIGHEST = lax.Precision.HIGHEST

D_MODEL = 1024
HEAD_DIM = 64
N_HEADS = 4
BRANCH = N_HEADS * HEAD_DIM
CONV_WIDTH = 3
MB_BLOCK = 256
MB_TOPK = 3
PAGE_SIZE = 128
ROPE_DIMS = HEAD_DIM // 4
ROPE_THETA = 500000.0
RW_LORA = 128
RW_COLS = 3 * BRANCH + RW_LORA
D_FF = 11 * D_MODEL // 4
MLSTM_CHUNK = 64
RWKV_CHUNK = 64
NORM_EPS = 1e-6
GN_EPS = 64e-5
LANES = 128

Z_GATES = 0
Z_MLSTM = 4096
Z_SCONV = 5120
Z_MOBA = 5888
Z_MIF = 6656
Z_RWKV = 7168
Z_WIDTH = 8192

VMEM_LIMIT = 48 * 1024 * 1024


def _cparams(*sem):
    return pltpu.CompilerParams(dimension_semantics=sem, vmem_limit_bytes=VMEM_LIMIT)


def _sigmoid(x):
    return 1.0 / (1.0 + jnp.exp(-x))


def _head_blockdiag(n):
    r = lax.broadcasted_iota(jnp.int32, (n, n), 0) // HEAD_DIM
    c = lax.broadcasted_iota(jnp.int32, (n, n), 1) // HEAD_DIM
    return (r == c).astype(F32)


def _head_sum(x, bd):
    return jnp.dot(x, bd, precision=HIGHEST, preferred_element_type=F32)


def _dot_nt(a, b):
    return lax.dot_general(a, b, (((1,), (1,)), ((), ())), preferred_element_type=F32)


def _dot_tn(a, b):
    return lax.dot_general(a, b, (((0,), (0,)), ((), ())), preferred_element_type=F32)


def _pick_tile(n, cap):
    t = cap
    while n % t:
        t //= 2
    return t


def _ada_kernel(c_ref, w_ref, b_ref, o_ref):
    c = c_ref[...]
    a = (c * _sigmoid(c)).astype(BF16)
    o_ref[...] = jnp.dot(a, w_ref[...], preferred_element_type=F32) + b_ref[...]


def _ada(c_all, w, b):
    m, d = c_all.shape
    n = w.shape[1]
    tn = 1536
    return pl.pallas_call(
        _ada_kernel,
        grid=(n // tn,),
        in_specs=[pl.BlockSpec((m, d), lambda j: (0, 0)),
                  pl.BlockSpec((d, tn), lambda j: (0, j)),
                  pl.BlockSpec((1, tn), lambda j: (0, j))],
        out_specs=pl.BlockSpec((m, tn), lambda j: (0, j)),
        out_shape=jax.ShapeDtypeStruct((m, n), F32),
        compiler_params=_cparams("parallel"),
        name="ada",
    )(c_all, w, b)


def _nmm_kernel(x_ref, g_ref, sc_ref, sh_ref, w_ref, o_ref, h_sc):
    @pl.when(pl.program_id(1) == 0)
    def _():
        x = x_ref[...]
        y = x * lax.rsqrt(jnp.mean(x * x, axis=-1, keepdims=True) + NORM_EPS) * g_ref[...]
        h_sc[...] = (y * (1.0 + sc_ref[...]) + sh_ref[...]).astype(BF16)

    o_ref[...] = jnp.dot(h_sc[...], w_ref[...], preferred_element_type=F32)


def _norm_mod_matmul(x, g, sc, sh, w, grp, tn):
    n, d = x.shape
    nout = w.shape[1]
    tm, tpg, r = grp["tm"], grp["tpg"], sc.shape[1]
    return pl.pallas_call(
        _nmm_kernel,
        grid=(n // tm, nout // tn),
        in_specs=[pl.BlockSpec((tm, d), lambda i, j: (i, 0)),
                  pl.BlockSpec((1, d), lambda i, j: (0, 0)),
                  pl.BlockSpec((None, r, d), lambda i, j: (i // tpg, 0, 0)),
                  pl.BlockSpec((None, r, d), lambda i, j: (i // tpg, 0, 0)),
                  pl.BlockSpec((d, tn), lambda i, j: (0, j))],
        out_specs=pl.BlockSpec((tm, tn), lambda i, j: (i, j)),
        out_shape=jax.ShapeDtypeStruct((n, nout), F32),
        scratch_shapes=[pltpu.VMEM((tm, d), BF16)],
        compiler_params=_cparams("parallel", "arbitrary"),
        name="norm_mod_matmul",
    )(x, g, sc, sh, w)


def _log_sigmoid(x):
    return jnp.minimum(x, 0.0) - jnp.log(1.0 + jnp.exp(-jnp.abs(x)))


def _mlstm_kernel(zq_ref, zif_ref, bias_ref, g_ref, c0_ref, n0_ref, m0_ref,
                  act_ref, c_out, n_out, m_out, c_sc, n_sc, m_sc, *, chunk):
    t_ = chunk
    ci = pl.program_id(1)

    @pl.when(ci == 0)
    def _():
        c_sc[...] = c0_ref[0]
        n_sc[...] = n0_ref[0]
        m_sc[...] = m0_ref[0]

    gates = zif_ref[...] + bias_ref[...]
    logf = _log_sigmoid(gates)
    row = lax.broadcasted_iota(jnp.int32, (t_, t_), 0)
    col = lax.broadcasted_iota(jnp.int32, (t_, t_), 1)
    causal = col <= row
    eye = row == col
    row1 = lax.broadcasted_iota(jnp.int32, (t_, 1), 0)
    for h in range(N_HEADS):
        lo = h * HEAD_DIM
        q = zq_ref[:, lo:lo + HEAD_DIM]
        k = zq_ref[:, BRANCH + lo:BRANCH + lo + HEAD_DIM] * (HEAD_DIM ** -0.5)
        v = zq_ref[:, 2 * BRANCH + lo:2 * BRANCH + lo + HEAD_DIM]
        o = zq_ref[:, 3 * BRANCH + lo:3 * BRANCH + lo + HEAD_DIM]
        li_col = gates[:, h:h + 1]
        lf_col = logf[:, N_HEADS + h:N_HEADS + h + 1]
        li_row = jnp.sum(jnp.where(eye, li_col, 0.0), axis=0, keepdims=True)
        lf_row = jnp.sum(jnp.where(eye, lf_col, 0.0), axis=0, keepdims=True)
        b_col = jnp.sum(jnp.where(causal, lf_row, 0.0), axis=1, keepdims=True)
        b_row = jnp.sum(jnp.where(row <= col, lf_col, 0.0), axis=0, keepdims=True)
        m_prev = m_sc[h][:, 0:1]
        d = jnp.where(causal, b_col - b_row + li_row, -jnp.inf)
        inter = b_col + m_prev
        mt = jnp.maximum(jnp.max(d, axis=1, keepdims=True), inter)
        qb, kb, vb = q.astype(BF16), k.astype(BF16), v.astype(BF16)
        w = jnp.exp(d - mt) * _dot_nt(qb, kb)
        a_int = jnp.exp(inter - mt)
        c = c_sc[h]
        n = n_sc[h]
        num = jnp.dot(w.astype(BF16), vb, preferred_element_type=F32) \
            + a_int * jnp.dot(qb, c.astype(BF16), preferred_element_type=F32)
        den = jnp.sum(w, axis=1, keepdims=True) + a_int * jnp.sum(q * n, axis=1, keepdims=True)
        hh = num / jnp.maximum(jnp.abs(den), jnp.exp(-mt))
        hn = hh * lax.rsqrt(jnp.mean(hh * hh, axis=-1, keepdims=True) + NORM_EPS)
        act_ref[:, lo:lo + HEAD_DIM] = hn * g_ref[:, lo:lo + HEAD_DIM] * _sigmoid(o)
        b_end = jnp.sum(jnp.where(row1 == t_ - 1, b_col, 0.0), axis=0, keepdims=True)
        g_col = b_end - b_col + li_col
        m_new = jnp.maximum(b_end + m_prev, jnp.max(g_col, axis=0, keepdims=True))
        wk = jnp.exp(g_col - m_new) * k
        decay = jnp.exp(b_end + m_prev - m_new)
        c_sc[h] = decay * c + _dot_tn(wk.astype(BF16), vb)
        n_sc[h] = decay * n + jnp.sum(wk, axis=0, keepdims=True)
        m_sc[h] = jnp.broadcast_to(m_new, (1, LANES))

    @pl.when(ci == pl.num_programs(1) - 1)
    def _():
        c_out[0] = c_sc[...]
        n_out[0] = n_sc[...]
        m_out[0] = m_sc[...]


def _mlstm(z, bias_if, g, c0, n0, m0, grp):
    b_, l_ = grp["B"], grp["L"]
    t_ = min(MLSTM_CHUNK, l_)
    nc = l_ // t_
    h_ = N_HEADS
    n0 = n0.reshape(b_, h_, 1, HEAD_DIM)
    m0 = jnp.broadcast_to(m0.reshape(b_, h_, 1, 1), (b_, h_, 1, LANES))
    state_spec = lambda shp: pl.BlockSpec((1,) + shp, lambda b, c: (b, 0, 0, 0))
    act, c1, n1, m1 = pl.pallas_call(
        functools.partial(_mlstm_kernel, chunk=t_),
        grid=(b_, nc),
        in_specs=[pl.BlockSpec((t_, 4 * BRANCH), lambda b, c: (b * nc + c, Z_MLSTM // (4 * BRANCH))),
                  pl.BlockSpec((t_, LANES), lambda b, c: (b * nc + c, Z_MIF // LANES)),
                  pl.BlockSpec((1, LANES), lambda b, c: (0, 0)),
                  pl.BlockSpec((1, BRANCH), lambda b, c: (0, 0)),
                  state_spec((h_, HEAD_DIM, HEAD_DIM)),
                  state_spec((h_, 1, HEAD_DIM)),
                  state_spec((h_, 1, LANES))],
        out_specs=[pl.BlockSpec((t_, BRANCH), lambda b, c: (b * nc + c, 0)),
                   state_spec((h_, HEAD_DIM, HEAD_DIM)),
                   state_spec((h_, 1, HEAD_DIM)),
                   state_spec((h_, 1, LANES))],
        out_shape=[jax.ShapeDtypeStruct((b_ * l_, BRANCH), F32),
                   jax.ShapeDtypeStruct((b_, h_, HEAD_DIM, HEAD_DIM), F32),
                   jax.ShapeDtypeStruct((b_, h_, 1, HEAD_DIM), F32),
                   jax.ShapeDtypeStruct((b_, h_, 1, LANES), F32)],
        scratch_shapes=[pltpu.VMEM((h_, HEAD_DIM, HEAD_DIM), F32),
                        pltpu.VMEM((h_, 1, HEAD_DIM), F32),
                        pltpu.VMEM((h_, 1, LANES), F32)],
        compiler_params=_cparams("parallel", "arbitrary"),
        name="mlstm",
    )(z, z, bias_if, g, c0, n0, m0)
    return act, c1, n1.reshape(b_, h_, HEAD_DIM), m1[:, :, 0, 0]


def _conv3(p, prev2, prev1, w_ref):
    row = lax.broadcasted_iota(jnp.int32, p.shape, 0)
    p1 = jnp.where(row == 0, prev1, pltpu.roll(p, 1, axis=0))
    p2 = jnp.where(row == 0, prev2, jnp.where(row == 1, prev1, pltpu.roll(p, 2, axis=0)))
    return w_ref[0:1, :] * p2 + w_ref[1:2, :] * p1 + w_ref[2:3, :] * p


def _sconv_kernel(sb_ref, sc_ref, sh_ref, w_ref, buf_ref, out_ref, st_ref, carry, *, tm):
    @pl.when(pl.program_id(1) == 0)
    def _():
        carry[...] = buf_ref[0]

    p = sc_ref[...] * sh_ref[...]
    u = _conv3(p, carry[0:1, :], carry[1:2, :], w_ref)
    out_ref[...] = sb_ref[...] * u
    new = sc_ref[tm - 2:tm, :] * sh_ref[tm - 2:tm, :]
    carry[...] = new
    st_ref[0] = new


def _sconv(z, w, buf, grp):
    b_, l_ = grp["B"], grp["L"]
    tm = _pick_tile(l_, 512)
    nt = l_ // tm
    cb = Z_SCONV // BRANCH
    zspec = lambda j: pl.BlockSpec((tm, BRANCH), lambda b, i: (b * nt + i, cb + j))
    return pl.pallas_call(
        functools.partial(_sconv_kernel, tm=tm),
        grid=(b_, nt),
        in_specs=[zspec(0), zspec(1), zspec(2),
                  pl.BlockSpec((CONV_WIDTH, BRANCH), lambda b, i: (0, 0)),
                  pl.BlockSpec((1, CONV_WIDTH - 1, BRANCH), lambda b, i: (b, 0, 0))],
        out_specs=[pl.BlockSpec((tm, BRANCH), lambda b, i: (b * nt + i, 0)),
                   pl.BlockSpec((1, CONV_WIDTH - 1, BRANCH), lambda b, i: (b, 0, 0))],
        out_shape=[jax.ShapeDtypeStruct((b_ * l_, BRANCH), F32),
                   jax.ShapeDtypeStruct((b_, CONV_WIDTH - 1, BRANCH), F32)],
        scratch_shapes=[pltpu.VMEM((CONV_WIDTH - 1, BRANCH), F32)],
        compiler_params=_cparams("parallel", "arbitrary"),
        name="sconv",
    )(z, z, z, w, buf)


def _qk_prep_kernel(q_ref, k_ref, qg_ref, kg_ref, cos_ref, sa_ref, sb_ref, qo_ref, ko_ref):
    bd = _head_blockdiag(BRANCH)
    cos, sa, sb = cos_ref[...], sa_ref[...], sb_ref[...]
    half = ROPE_DIMS // 2

    def prep(x, g):
        y = x * lax.rsqrt(_head_sum(x * x, bd) * (1.0 / HEAD_DIM) + NORM_EPS) * g
        return y * cos + pltpu.roll(y, BRANCH - half, axis=1) * sa + pltpu.roll(y, half, axis=1) * sb

    qo_ref[...] = prep(q_ref[...], qg_ref[...])
    ko_ref[...] = prep(k_ref[...], kg_ref[...])


def _rope_tables(pos):
    half = ROPE_DIMS // 2
    inv = jnp.exp(-math.log(ROPE_THETA) * jnp.arange(0, ROPE_DIMS, 2, dtype=F32) / ROPE_DIMS)
    ang = pos.astype(F32)[:, None] * inv[None, :]
    cos, sin = jnp.cos(ang), jnp.sin(ang)
    n = pos.shape[0]
    rest = HEAD_DIM - ROPE_DIMS
    c_h = jnp.concatenate([cos, cos, jnp.ones((n, rest), F32)], axis=1)
    sa_h = jnp.concatenate([-sin, jnp.zeros((n, half + rest), F32)], axis=1)
    sb_h = jnp.concatenate([jnp.zeros((n, half), F32), sin, jnp.zeros((n, rest), F32)], axis=1)
    tile = lambda t: jnp.tile(t, (1, N_HEADS))
    return tile(c_h), tile(sa_h), tile(sb_h)


def _qk_prep(z, qg, kg, tables, grp):
    n = grp["B"] * grp["L"]
    ltab = tables[0].shape[0]
    tm = _pick_tile(ltab, 512)
    npos = ltab // tm
    cb = Z_MOBA // BRANCH
    tspec = pl.BlockSpec((tm, BRANCH), lambda i: (i % npos, 0))
    gspec = pl.BlockSpec((1, BRANCH), lambda i: (0, 0))
    ospec = pl.BlockSpec((tm, BRANCH), lambda i: (i, 0))
    return pl.pallas_call(
        _qk_prep_kernel,
        grid=(n // tm,),
        in_specs=[pl.BlockSpec((tm, BRANCH), lambda i: (i, cb)),
                  pl.BlockSpec((tm, BRANCH), lambda i: (i, cb + 1)),
                  gspec, gspec, tspec, tspec, tspec],
        out_specs=[ospec, ospec],
        out_shape=[jax.ShapeDtypeStruct((n, BRANCH), F32)] * 2,
        compiler_params=_cparams("parallel"),
        name="qk_prep",
    )(z, z, qg, kg, *tables)


def _moba_prompt_kernel(q_ref, k_ref, v_ref, o_ref, kmean, *, nb):
    qi = pl.program_id(1)
    tq = MB_BLOCK

    @pl.when(qi == 0)
    def _():
        kmean[...] = jnp.zeros_like(kmean)
        for n in range(nb):
            kmean[n:n + 1, :] = jnp.mean(k_ref[n * MB_BLOCK:(n + 1) * MB_BLOCK, :], axis=0, keepdims=True)

    lane = lax.broadcasted_iota(jnp.int32, (tq, LANES), 1)
    row = lax.broadcasted_iota(jnp.int32, (tq, tq), 0)
    col = lax.broadcasted_iota(jnp.int32, (tq, tq), 1)
    scale = HEAD_DIM ** -0.5
    own0 = pl.multiple_of(qi * MB_BLOCK, MB_BLOCK)
    for h in range(N_HEADS):
        hs = slice(h * HEAD_DIM, (h + 1) * HEAD_DIM)
        qh = q_ref[:, hs]
        qb = qh.astype(BF16)
        s = lax.dot_general(qh, kmean[:, hs], (((1,), (1,)), ((), ())),
                            precision=HIGHEST, preferred_element_type=F32)
        valid = lane < qi
        s = jnp.where(valid, s, -jnp.inf)
        rank = jnp.zeros((tq, LANES), jnp.int32)
        for m in range(nb):
            sm = s[:, m:m + 1]
            rank += ((sm > s) | ((sm == s) & (m < lane))).astype(jnp.int32)
        sel = (valid & (rank < MB_TOPK)).astype(F32)
        kb = k_ref[pl.ds(own0, MB_BLOCK), hs].astype(BF16)
        vb = v_ref[pl.ds(own0, MB_BLOCK), hs].astype(BF16)
        sc = jnp.where(col <= row, _dot_nt(qb, kb) * scale, -jnp.inf)
        m_i = jnp.max(sc, axis=1, keepdims=True)
        p = jnp.exp(sc - m_i)
        l_i = jnp.sum(p, axis=1, keepdims=True)
        acc = jnp.dot(p.astype(BF16), vb, preferred_element_type=F32)

        def body(n, carry):
            m_i, l_i, acc = carry
            k0 = pl.multiple_of(n * MB_BLOCK, MB_BLOCK)
            kb = k_ref[pl.ds(k0, MB_BLOCK), hs].astype(BF16)
            vb = v_ref[pl.ds(k0, MB_BLOCK), hs].astype(BF16)
            seln = jnp.sum(jnp.where(lane == n, sel, 0.0), axis=1, keepdims=True) > 0.0
            sc = jnp.where(seln, _dot_nt(qb, kb) * scale, -jnp.inf)
            m_new = jnp.maximum(m_i, jnp.max(sc, axis=1, keepdims=True))
            alpha = jnp.exp(m_i - m_new)
            p = jnp.exp(sc - m_new)
            l_new = alpha * l_i + jnp.sum(p, axis=1, keepdims=True)
            acc_new = alpha * acc + jnp.dot(p.astype(BF16), vb, preferred_element_type=F32)
            return m_new, l_new, acc_new

        m_i, l_i, acc = lax.fori_loop(0, qi, body, (m_i, l_i, acc))
        o_ref[:, hs] = acc / l_i


def _moba_prompt(q, k, z, grp):
    b_, l_ = grp["B"], grp["L"]
    assert l_ % MB_BLOCK == 0 and l_ // MB_BLOCK <= LANES
    nb = l_ // MB_BLOCK
    return pl.pallas_call(
        functools.partial(_moba_prompt_kernel, nb=nb),
        grid=(b_, nb),
        in_specs=[pl.BlockSpec((MB_BLOCK, BRANCH), lambda b, i: (b * nb + i, 0)),
                  pl.BlockSpec((l_, BRANCH), lambda b, i: (b, 0)),
                  pl.BlockSpec((l_, BRANCH), lambda b, i: (b, Z_MOBA // BRANCH + 2))],
        out_specs=pl.BlockSpec((MB_BLOCK, BRANCH), lambda b, i: (b * nb + i, 0)),
        out_shape=jax.ShapeDtypeStruct((b_ * l_, BRANCH), F32),
        scratch_shapes=[pltpu.VMEM((LANES, BRANCH), F32)],
        compiler_params=_cparams("parallel", "arbitrary"),
        name="moba_prompt",
    )(q, k, z)


def _moba_sample_kernel(pt_ref, q_ref, kn_ref, vn_ref, *refs, nblk, dec, gsz):
    ppb = MB_BLOCK // PAGE_SIZE
    k_refs, v_refs = refs[:ppb * gsz], refs[ppb * gsz:2 * ppb * gsz]
    o_ref, kmean_s, m_s, l_s, o_s = refs[2 * ppb * gsz:]
    step = pl.program_id(1)
    nq = N_HEADS * dec
    scale = HEAD_DIM ** -0.5
    lane = lax.broadcasted_iota(jnp.int32, (nq, LANES), 1)
    lane_c = lax.broadcasted_iota(jnp.int32, (BRANCH, LANES), 1)

    @pl.when(step == 0)
    def _():
        kmean_s[...] = jnp.zeros_like(kmean_s)
        m_s[...] = jnp.full_like(m_s, -jnp.inf)
        l_s[...] = jnp.zeros_like(l_s)

    row_head = jnp.concatenate([jnp.full((dec, BRANCH), h, jnp.int32) for h in range(N_HEADS)], axis=0)
    lane_head = lax.broadcasted_iota(jnp.int32, (nq, BRANCH), 1) // HEAD_DIM
    qbd = jnp.where(row_head == lane_head, jnp.concatenate([q_ref[...]] * N_HEADS, axis=0), 0.0)
    qbd_b = qbd.astype(BF16)
    for g in range(gsz):
        blk = step * gsz + g
        kt = jnp.concatenate([k_refs[ppb * g + j][...] for j in range(ppb)], axis=1)
        vt = jnp.concatenate([v_refs[ppb * g + j][...] for j in range(ppb)], axis=1)
        s = jnp.dot(qbd_b, kt.astype(BF16), preferred_element_type=F32) * scale
        mx = jnp.max(s, axis=1, keepdims=True)
        p = jnp.exp(s - mx)
        o_s[blk] = _dot_nt(p.astype(BF16), vt.astype(BF16))
        kmean_s[...] = jnp.where(lane_c == blk, jnp.mean(kt, axis=1, keepdims=True), kmean_s[...])
        m_s[...] = jnp.where(lane == blk, mx, m_s[...])
        l_s[...] = jnp.where(lane == blk, jnp.sum(p, axis=1, keepdims=True), l_s[...])

    @pl.when(step == pl.num_programs(1) - 1)
    def _():
        zpad = jnp.zeros((LANES - dec, BRANCH), F32)
        knew = jnp.concatenate([kn_ref[...], zpad], axis=0)
        vnew = jnp.concatenate([vn_ref[...], zpad], axis=0)
        rowq = jnp.concatenate([lax.broadcasted_iota(jnp.int32, (dec, LANES), 0)] * N_HEADS, axis=0)
        lane_f = lane.astype(F32)
        s = jnp.dot(qbd, kmean_s[...], precision=HIGHEST, preferred_element_type=F32)
        s = jnp.where(lane < nblk, s, -jnp.inf)
        sel = jnp.zeros((nq, LANES), jnp.bool_)
        for _ in range(MB_TOPK):
            mxv = jnp.max(s, axis=1, keepdims=True)
            idx = jnp.min(jnp.where(s == mxv, lane_f, 2.0 * LANES), axis=1, keepdims=True)
            pick = lane_f == idx
            sel = sel | pick
            s = jnp.where(pick, -jnp.inf, s)
        sel = sel & (lane < nblk)
        s_own = jnp.where(lane <= rowq, _dot_nt(qbd_b, knew.astype(BF16)) * scale, -jnp.inf)
        m_own = jnp.max(s_own, axis=1, keepdims=True)
        p_own = jnp.exp(s_own - m_own)
        l_own = jnp.sum(p_own, axis=1, keepdims=True)
        o_own = jnp.dot(p_own.astype(BF16), vnew.astype(BF16), preferred_element_type=F32)
        mb = m_s[...]
        mtot = jnp.maximum(jnp.max(jnp.where(sel, mb, -jnp.inf), axis=1, keepdims=True), m_own)
        wgt = jnp.where(sel, jnp.exp(mb - mtot), 0.0)
        a_own = jnp.exp(m_own - mtot)
        ltot = jnp.sum(wgt * l_s[...], axis=1, keepdims=True) + l_own * a_own
        acc = o_own * a_own
        for j in range(nblk):
            acc = acc + wgt[:, j:j + 1] * o_s[j]
        res = acc / ltot
        for h in range(N_HEADS):
            hs = slice(h * HEAD_DIM, (h + 1) * HEAD_DIM)
            o_ref[:, hs] = res[h * dec:(h + 1) * dec, hs]


def _moba_sample(q, k, v, pool_k, pool_v, page_table, layer, grp):
    b_, dec = grp["B"], grp["L"]
    n_pages = page_table.shape[1]
    ppb = MB_BLOCK // PAGE_SIZE
    assert n_pages % ppb == 0
    nblk = n_pages // ppb
    assert MB_TOPK <= nblk <= LANES and dec <= LANES and dec % 8 == 0
    gsz = max(g for g in range(1, 9) if nblk % g == 0)
    as_pages = lambda t: t.transpose(0, 1, 3, 4, 2).reshape(t.shape[0], t.shape[1], BRANCH, PAGE_SIZE)
    pool_k, pool_v = as_pages(pool_k), as_pages(pool_v)
    pt = page_table.reshape(-1)

    def page(j):
        return pl.BlockSpec((None, None, BRANCH, PAGE_SIZE),
                            lambda b, s, pt: (layer, pt[b * n_pages + s * gsz * ppb + j], 0, 0))

    pages = [page(j) for j in range(gsz * ppb)]
    new = pl.BlockSpec((dec, BRANCH), lambda b, s, pt: (b, 0))
    nq = N_HEADS * dec
    return pl.pallas_call(
        functools.partial(_moba_sample_kernel, nblk=nblk, dec=dec, gsz=gsz),
        grid_spec=pltpu.PrefetchScalarGridSpec(
            num_scalar_prefetch=1,
            grid=(b_, nblk // gsz),
            in_specs=[new, new, new] + pages + pages,
            out_specs=new,
            scratch_shapes=[pltpu.VMEM((BRANCH, LANES), F32),
                            pltpu.VMEM((nq, LANES), F32),
                            pltpu.VMEM((nq, LANES), F32),
                            pltpu.VMEM((nblk, nq, BRANCH), F32)]),
        out_shape=jax.ShapeDtypeStruct((b_ * dec, BRANCH), F32),
        compiler_params=_cparams("parallel", "arbitrary"),
        name="moba_sample",
    )(pt, q, k, v, *([pool_k] * (gsz * ppb)), *([pool_v] * (gsz * ppb)))


def _rwkv_prep_kernel(z_ref, prev_ref, mu_ref, wlr_ref, w0_ref, a0_ref, kks_ref, kas_ref, rk_ref,
                      r_o, w_o, k_o, v_o, kk_o, kka_o, bonus_o, g_o, carry, *, tm):
    @pl.when(pl.program_id(1) == 0)
    def _():
        carry[...] = prev_ref[0]

    z = z_ref[...]
    row = lax.broadcasted_iota(jnp.int32, z.shape, 0)
    zs = jnp.where(row == 0, carry[...], pltpu.roll(z, 1, axis=0))
    carry[...] = z_ref[tm - 1:tm, :]
    zz = z + mu_ref[...] * (zs - z)
    r = zz[:, 0:BRANCH]
    k = zz[:, BRANCH:2 * BRANCH]
    v = zz[:, 2 * BRANCH:3 * BRANCH]
    lr = zz[:, 3 * BRANCH:3 * BRANCH + RW_LORA]
    lane = lax.broadcasted_iota(jnp.int32, lr.shape, 1)
    lr_in = jnp.where(lane < 32, jnp.tanh(lr), jnp.where(lane < 64, lr, _sigmoid(lr)))
    lo = jnp.dot(lr_in.astype(BF16), wlr_ref[...], preferred_element_type=F32)
    log_decay = -math.exp(-0.5) * _sigmoid(w0_ref[...] + lo[:, 0:BRANCH])
    a = _sigmoid(a0_ref[...] + lo[:, BRANCH:2 * BRANCH])
    bd = _head_blockdiag(BRANCH)
    kk = k * kks_ref[...]
    kk = kk * lax.rsqrt(jnp.maximum(_head_sum(kk * kk, bd), 1e-24))
    k2 = k * (1.0 + (a - 1.0) * kas_ref[...])
    r_o[...] = r
    w_o[...] = log_decay
    k_o[...] = k2
    v_o[...] = v
    kk_o[...] = kk
    kka_o[...] = kk * a
    bonus_o[...] = _head_sum(r * k2 * rk_ref[...], bd) * v
    g_o[...] = lo[:, 2 * BRANCH:3 * BRANCH]


def _rwkv_prep(z, prev, p, grp):
    b_, l_ = grp["B"], grp["L"]
    tm = _pick_tile(l_, 256)
    nt = l_ // tm
    zw = Z_WIDTH - Z_RWKV
    vec = pl.BlockSpec((1, BRANCH), lambda b, i: (0, 0))
    ospec = pl.BlockSpec((tm, BRANCH), lambda b, i: (b * nt + i, 0))
    return pl.pallas_call(
        functools.partial(_rwkv_prep_kernel, tm=tm),
        grid=(b_, nt),
        in_specs=[pl.BlockSpec((tm, zw), lambda b, i: (b * nt + i, Z_RWKV // zw)),
                  pl.BlockSpec((1, 1, zw), lambda b, i: (b, 0, 0)),
                  pl.BlockSpec((1, zw), lambda b, i: (0, 0)),
                  pl.BlockSpec((RW_LORA, 3 * BRANCH), lambda b, i: (0, 0)),
                  vec, vec, vec, vec, vec],
        out_specs=[ospec] * 8,
        out_shape=[jax.ShapeDtypeStruct((b_ * l_, BRANCH), F32)] * 8,
        scratch_shapes=[pltpu.VMEM((1, zw), F32)],
        compiler_params=_cparams("parallel", "arbitrary"),
        name="rwkv_prep",
    )(z, prev, p["mu"], p["wlr"], p["w0"], p["a0"], p["kk"], p["ka"], p["rk"])


def _rwkv_chunk_kernel(r_ref, lw_ref, k_ref, v_ref, kk_ref, kka_ref, h0_ref, y_ref, h_out, h_sc, *, chunk):
    t_ = chunk
    ci = pl.program_id(1)

    @pl.when(ci == 0)
    def _():
        h_sc[...] = h0_ref[0]

    row = lax.broadcasted_iota(jnp.int32, (t_, t_), 0)
    col = lax.broadcasted_iota(jnp.int32, (t_, t_), 1)
    lower = row >= col
    strict = row > col
    eye_t = (row == col).astype(F32)
    eye_d = (lax.broadcasted_iota(jnp.int32, (HEAD_DIM, HEAD_DIM), 0)
             == lax.broadcasted_iota(jnp.int32, (HEAD_DIM, HEAD_DIM), 1))
    row1 = lax.broadcasted_iota(jnp.int32, (t_, 1), 0)

    lw = lw_ref[...]
    cum = jnp.dot(lower.astype(F32), lw, precision=HIGHEST, preferred_element_type=F32)
    p_in = jnp.exp(cum)
    p_inv = jnp.exp(-cum)
    p_end = jnp.sum(jnp.where(row1 == t_ - 1, p_in, 0.0), axis=0, keepdims=True)
    kkm = kk_ref[...] * jnp.exp(cum - lw)
    rp = r_ref[...] * p_in
    kh = k_ref[...] * p_inv
    ah = kka_ref[...] * p_inv
    khe = kh * p_end
    ahe = ah * p_end
    v_all = v_ref[...]
    for h in range(N_HEADS):
        hs = slice(h * HEAD_DIM, (h + 1) * HEAD_DIM)
        lhs = jnp.concatenate([kkm[:, hs], rp[:, hs]], axis=0).astype(BF16)
        vb = v_all[:, hs].astype(BF16)
        gk = _dot_nt(lhs, kh[:, hs].astype(BF16))
        ga = _dot_nt(lhs, ah[:, hs].astype(BF16))
        a_k = jnp.where(strict, gk[:t_], 0.0)
        b_k = jnp.where(lower, gk[t_:], 0.0)
        nmat = jnp.where(strict, -ga[:t_], 0.0)
        b_a = jnp.where(lower, ga[t_:], 0.0)
        m = eye_t + nmat
        pw = nmat
        span = 2
        while span < t_:
            pw = jnp.dot(pw, pw, precision=HIGHEST, preferred_element_type=F32)
            m = m + jnp.dot(m, pw, precision=HIGHEST, preferred_element_type=F32)
            span *= 2
        h0 = h_sc[h]
        x0 = jnp.dot(lhs, h0.astype(BF16), preferred_element_type=F32)
        kv = jnp.dot(jnp.concatenate([a_k, b_k], axis=0).astype(BF16), vb, preferred_element_type=F32)
        u = jnp.dot(m.astype(BF16), (x0[:t_] + kv[:t_]).astype(BF16), preferred_element_type=F32)
        ub = u.astype(BF16)
        y_ref[:, hs] = x0[t_:] + kv[t_:] - jnp.dot(b_a.astype(BF16), ub, preferred_element_type=F32)
        pe_col = jnp.sum(jnp.where(eye_d, p_end[:, hs], 0.0), axis=1, keepdims=True)
        h_sc[h] = pe_col * h0 + _dot_tn(khe[:, hs].astype(BF16), vb) - _dot_tn(ahe[:, hs].astype(BF16), ub)

    @pl.when(ci == pl.num_programs(1) - 1)
    def _():
        h_out[0] = h_sc[...]


def _rwkv_scan(r, lw, k, v, kk, kka, s0, grp):
    b_, l_ = grp["B"], grp["L"]
    chunk = _pick_tile(l_, RWKV_CHUNK)
    nc = l_ // chunk
    seq = pl.BlockSpec((chunk, BRANCH), lambda b, c: (b * nc + c, 0))
    st = pl.BlockSpec((1, N_HEADS, HEAD_DIM, HEAD_DIM), lambda b, c: (b, 0, 0, 0))
    y, h1 = pl.pallas_call(
        functools.partial(_rwkv_chunk_kernel, chunk=chunk),
        grid=(b_, nc),
        in_specs=[seq] * 6 + [st],
        out_specs=[seq, st],
        out_shape=[jax.ShapeDtypeStruct((b_ * l_, BRANCH), F32),
                   jax.ShapeDtypeStruct((b_, N_HEADS, HEAD_DIM, HEAD_DIM), F32)],
        scratch_shapes=[pltpu.VMEM((N_HEADS, HEAD_DIM, HEAD_DIM), F32)],
        compiler_params=_cparams("parallel", "arbitrary"),
        name="rwkv_scan",
    )(r, lw, k, v, kk, kka, jnp.swapaxes(s0, -1, -2))
    return y, jnp.swapaxes(h1, -1, -2)


def _merge_kernel(ml_ref, sc_ref, mb_ref, y_ref, bonus_ref, g_ref, gn_ref, gates_ref, x_ref, gt_ref,
                  mp_ref, sp_ref, ap_ref, rp_ref, wo_ref, o_ref):
    bd = _head_blockdiag(BRANCH)
    y = y_ref[...]
    yc = y - _head_sum(y, bd) * (1.0 / HEAD_DIM)
    yn = yc * lax.rsqrt(_head_sum(yc * yc, bd) * (1.0 / HEAD_DIM) + GN_EPS)
    rw = (yn * gn_ref[...] + bonus_ref[...]) * g_ref[...]
    acts = (ml_ref[...], sc_ref[...], mb_ref[...], rw)
    projs = (mp_ref, sp_ref, ap_ref, rp_ref)
    merged = None
    for j in range(4):
        br = jnp.dot(acts[j].astype(BF16), projs[j][...], preferred_element_type=F32)
        term = _sigmoid(gates_ref[:, j * D_MODEL:(j + 1) * D_MODEL]) * br
        merged = term if merged is None else merged + term
    o_ref[...] = x_ref[...] + gt_ref[...] * jnp.dot(merged.astype(BF16), wo_ref[...], preferred_element_type=F32)


def _merge(ml, sc, mb, y, bonus, g, gn, z, x, gt, projs, w_out, grp):
    n, d = x.shape
    tm = min(256, grp["tm"])
    tpg = grp["tpg"] * (grp["tm"] // tm)
    r = gt.shape[1]
    if r > 1:
        assert r == grp["tm"] and tpg == grp["tm"] // tm
        gt = gt.reshape(r // tm, tm, d)
        gt_spec = pl.BlockSpec((None, tm, d), lambda i: (i, 0, 0))
    else:
        gt_spec = pl.BlockSpec((None, 1, d), lambda i: (i // tpg, 0, 0))
    act = pl.BlockSpec((tm, BRANCH), lambda i: (i, 0))
    proj = pl.BlockSpec((BRANCH, d), lambda i: (0, 0))
    return pl.pallas_call(
        _merge_kernel,
        grid=(n // tm,),
        in_specs=[act] * 6 + [pl.BlockSpec((1, BRANCH), lambda i: (0, 0)),
                              pl.BlockSpec((tm, 4 * d), lambda i: (i, 0)),
                              pl.BlockSpec((tm, d), lambda i: (i, 0)),
                              gt_spec, proj, proj, proj, proj,
                              pl.BlockSpec((d, d), lambda i: (0, 0))],
        out_specs=pl.BlockSpec((tm, d), lambda i: (i, 0)),
        out_shape=jax.ShapeDtypeStruct((n, d), F32),
        compiler_params=_cparams("parallel"),
        name="merge",
    )(ml, sc, mb, y, bonus, g, gn, z, x, gt, *projs, w_out)


def _ffn_act_kernel(ua_ref, ul_ref, w_ref, buf_ref, o_ref, carry, *, tm):
    @pl.when(pl.program_id(1) == 0)
    def _():
        carry[...] = buf_ref[0]

    ua = ua_ref[...]
    u = _conv3(ua, carry[0:1, :], carry[1:2, :], w_ref)
    carry[...] = ua_ref[tm - 2:tm, :]
    o_ref[...] = (u * _sigmoid(u) * ul_ref[...]).astype(BF16)


def _ffn_act(up, w, buf, grp):
    b_, l_ = grp["B"], grp["L"]
    tm = _pick_tile(l_, 256)
    nt = l_ // tm
    half = lambda j: pl.BlockSpec((tm, D_FF), lambda b, i: (b * nt + i, j))
    return pl.pallas_call(
        functools.partial(_ffn_act_kernel, tm=tm),
        grid=(b_, nt),
        in_specs=[half(0), half(1),
                  pl.BlockSpec((CONV_WIDTH, D_FF), lambda b, i: (0, 0)),
                  pl.BlockSpec((1, CONV_WIDTH - 1, D_FF), lambda b, i: (b, 0, 0))],
        out_specs=pl.BlockSpec((tm, D_FF), lambda b, i: (b * nt + i, 0)),
        out_shape=jax.ShapeDtypeStruct((b_ * l_, D_FF), BF16),
        scratch_shapes=[pltpu.VMEM((CONV_WIDTH - 1, D_FF), F32)],
        compiler_params=_cparams("parallel", "arbitrary"),
        name="ffn_act",
    )(up, up, w, buf)


def _mm_res_kernel(a_ref, w_ref, x_ref, gt_ref, o_ref):
    o_ref[...] = x_ref[...] + gt_ref[...] * jnp.dot(a_ref[...], w_ref[...], preferred_element_type=F32)


def _mm_res(a, w, x, gt, grp):
    n, d = x.shape
    kdim = a.shape[1]
    tm, tpg, r = min(512, grp["tm"]), grp["tpg"] * (grp["tm"] // min(512, grp["tm"])), gt.shape[1]
    if r > 1:
        gt = gt.reshape(r // tm, tm, d)
        gt_spec = pl.BlockSpec((None, tm, d), lambda i: (i, 0, 0))
    else:
        gt_spec = pl.BlockSpec((None, 1, d), lambda i: (i // tpg, 0, 0))
    return pl.pallas_call(
        _mm_res_kernel,
        grid=(n // tm,),
        in_specs=[pl.BlockSpec((tm, kdim), lambda i: (i, 0)),
                  pl.BlockSpec((kdim, d), lambda i: (0, 0)),
                  pl.BlockSpec((tm, d), lambda i: (i, 0)),
                  gt_spec],
        out_specs=pl.BlockSpec((tm, d), lambda i: (i, 0)),
        out_shape=jax.ShapeDtypeStruct((n, d), F32),
        compiler_params=_cparams("parallel"),
        name="mm_res",
    )(a, w, x, gt)


def _prep_layer(l, P):
    d = D_MODEL
    w_in = P["w_in"][l]
    o_mif = 4 * BRANCH
    o_sb = o_mif + 2 * N_HEADS
    o_aq = o_sb + 3 * BRANCH
    o_rw = o_aq + 3 * BRANCH
    o_g = o_rw + RW_COLS
    w_in_p = jnp.concatenate([
        w_in[:, o_g:o_g + 4 * d], w_in[:, 0:o_mif], w_in[:, o_sb:o_aq], w_in[:, o_aq:o_rw],
        w_in[:, o_mif:o_sb], jnp.zeros((d, Z_RWKV - Z_MIF - 2 * N_HEADS), F32),
        w_in[:, o_rw:o_g], jnp.zeros((d, Z_WIDTH - Z_RWKV - RW_COLS), F32)], axis=1).astype(BF16)
    bias_if = jnp.concatenate([P["m_bi"][l], P["m_bf"][l], jnp.zeros((LANES - 2 * N_HEADS,), F32)])[None, :]
    zw = Z_WIDTH - Z_RWKV
    wlr = jnp.zeros((RW_LORA, 3 * BRANCH), F32)
    wlr = wlr.at[0:32, 0:BRANCH].set(P["r_wB"][l])
    wlr = wlr.at[32:64, BRANCH:2 * BRANCH].set(P["r_aB"][l])
    wlr = wlr.at[64:128, 2 * BRANCH:].set(P["r_gB"][l])
    row = lambda t: t[None, :]
    rw = dict(mu=jnp.pad(P["r_mu"][l], (0, zw - RW_COLS))[None, :], wlr=wlr.astype(BF16),
              w0=row(P["r_w0"][l]), a0=row(P["r_a0"][l]), kk=row(P["r_kk"][l]), ka=row(P["r_ka"][l]),
              rk=row(P["r_rk"][l]))
    bf = lambda name: P[name][l].astype(BF16)
    return dict(
        w_ada=bf("w_ada"), b_ada=row(P["b_ada"][l]), w_in=w_in_p, bias_if=bias_if,
        norm_mix_g=row(P["norm_mix_g"][l]), norm_ffn_g=row(P["norm_ffn_g"][l]),
        m_norm_g=row(P["m_norm_g"][l]), s_conv=P["s_conv"][l],
        qg=row(jnp.tile(P["a_qnorm"][l], N_HEADS)), kg=row(jnp.tile(P["a_knorm"][l], N_HEADS)),
        rw=rw, r_norm_g=row(P["r_norm_g"][l]),
        projs=(bf("m_proj"), bf("s_proj"), bf("a_proj"), bf("r_proj")), w_out=bf("w_out"),
        f_up=bf("f_up"), f_conv=P["f_conv"][l], f_down=bf("f_down"))


def _layer(x, mod, W, state, tables, attend, grp):
    mc, mn, mm, sbuf, rs, rshift, fbuf = state
    b_, l_ = grp["B"], grp["L"]
    sh_m, sc_m, gt_m, sh_f, sc_f, gt_f = mod
    z = _norm_mod_matmul(x, W["norm_mix_g"], sc_m, sh_m, W["w_in"], grp, 1024)
    ml, mc, mn, mm = _mlstm(z, W["bias_if"], W["m_norm_g"], mc, mn, mm, grp)
    sc, sbuf = _sconv(z, W["s_conv"], sbuf, grp)
    q, k = _qk_prep(z, W["qg"], W["kg"], tables, grp)
    v = z[:, Z_MOBA + 2 * BRANCH:Z_MOBA + 3 * BRANCH]
    mb = attend(q, k, z, v)
    zw = Z_WIDTH - Z_RWKV
    prev = jnp.pad(rshift, ((0, 0), (0, zw - RW_COLS)))[:, None, :]
    r, w, k2, vv, kk, kka, bonus, g = _rwkv_prep(z, prev, W["rw"], grp)
    y, rs_t = _rwkv_scan(r, w, k2, vv, kk, kka, rs, grp)
    rshift = z.reshape(b_, l_, Z_WIDTH)[:, -1, Z_RWKV:Z_RWKV + RW_COLS]
    x = _merge(ml, sc, mb, y, bonus, g, W["r_norm_g"], z, x, gt_m, W["projs"], W["w_out"], grp)
    up = _norm_mod_matmul(x, W["norm_ffn_g"], sc_f, sh_f, W["f_up"], grp, 1408)
    act = _ffn_act(up, W["f_conv"], fbuf, grp)
    fbuf = up.reshape(b_, l_, 2 * D_FF)[:, l_ - (CONV_WIDTH - 1):, :D_FF]
    x = _mm_res(act, W["f_down"], x, gt_f, grp)
    return x, (k, v), (mc, mn, mm, sbuf, rs_t, rshift, fbuf)


def kernel(x_prompt, x_sample, c_prompt, c_sample, cache_k, cache_v, page_table, state_mlstm_c, state_mlstm_n, state_mlstm_m, state_conv, state_rwkv, state_rwkv_shift, state_ffn_conv, norm_mix_g, norm_ffn_g, w_ada, b_ada, w_in, m_bi, m_bf, m_norm_g, m_proj, s_conv, s_proj, a_qnorm, a_knorm, a_proj, r_mu, r_w0, r_wB, r_a0, r_aB, r_gB, r_kk, r_ka, r_rk, r_norm_g, r_proj, w_out, f_up, f_conv, f_down):
    P = dict(norm_mix_g=norm_mix_g, norm_ffn_g=norm_ffn_g, w_ada=w_ada, b_ada=b_ada, w_in=w_in,
             m_bi=m_bi, m_bf=m_bf, m_norm_g=m_norm_g, m_proj=m_proj, s_conv=s_conv, s_proj=s_proj,
             a_qnorm=a_qnorm, a_knorm=a_knorm, a_proj=a_proj, r_mu=r_mu, r_w0=r_w0, r_wB=r_wB,
             r_a0=r_a0, r_aB=r_aB, r_gB=r_gB, r_kk=r_kk, r_ka=r_ka, r_rk=r_rk, r_norm_g=r_norm_g,
             r_proj=r_proj, w_out=w_out, f_up=f_up, f_conv=f_conv, f_down=f_down)
    depth = w_in.shape[0]
    bp, lp, d = x_prompt.shape
    bs, ls, _ = x_sample.shape
    n_s = bs * ls
    past = page_table.shape[1] * PAGE_SIZE
    tm_p = _pick_tile(lp, 1024)
    grp_p = dict(B=bp, L=lp, tm=tm_p, tpg=lp // tm_p)
    grp_s = dict(B=bs, L=ls, tm=n_s, tpg=1)
    assert n_s % 8 == 0 and n_s <= 1024

    zeros = lambda *s: jnp.zeros(s, F32)
    st_p = (zeros(bp, N_HEADS, HEAD_DIM, HEAD_DIM), zeros(bp, N_HEADS, HEAD_DIM), zeros(bp, N_HEADS),
            zeros(bp, CONV_WIDTH - 1, BRANCH), zeros(bp, N_HEADS, HEAD_DIM, HEAD_DIM), zeros(bp, RW_COLS),
            zeros(bp, CONV_WIDTH - 1, D_FF))
    tab_p = _rope_tables(jnp.arange(lp, dtype=jnp.int32))
    tab_s = tuple(jnp.tile(t, (bs, 1)) for t in _rope_tables(past + jnp.arange(ls, dtype=jnp.int32)))

    c_all = jnp.concatenate([c_prompt, c_sample], axis=0)
    hp = x_prompt.reshape(bp * lp, d)
    hs = x_sample.reshape(n_s, d)
    kv_p, kv_s, sts_p, sts_s = [], [], [], []
    for l in range(depth):
        W = _prep_layer(l, P)
        mod = _ada(c_all, W["w_ada"], W["b_ada"])
        mods = [mod[:, j * d:(j + 1) * d] for j in range(6)]
        mod_p = [m[:bp][:, None, :] for m in mods]
        mod_s = [jnp.repeat(m[bp:], ls, axis=0)[None] for m in mods]

        attend_p = lambda q, k, z, v: _moba_prompt(q, k, z, grp_p)
        hp, kv, st = _layer(hp, mod_p, W, st_p, tab_p, attend_p, grp_p)
        kv_p.append(kv)
        sts_p.append(st)

        prev = (state_mlstm_c[l], state_mlstm_n[l], state_mlstm_m[l], state_conv[l], state_rwkv[l],
                state_rwkv_shift[l], state_ffn_conv[l])
        attend_s = lambda q, k, z, v: _moba_sample(q, k, v, cache_k, cache_v, page_table, l, grp_s)
        hs, kv, st = _layer(hs, mod_s, W, prev, tab_s, attend_s, grp_s)
        kv_s.append(kv)
        sts_s.append(st)

    heads = lambda t, b_, l_: t.reshape(b_, l_, N_HEADS, HEAD_DIM)
    k_prompt = jnp.stack([heads(k, bp, lp) for k, _ in kv_p])
    v_prompt = jnp.stack([heads(v, bp, lp) for _, v in kv_p])
    k_sample = jnp.stack([heads(k, bs, ls) for k, _ in kv_s])
    v_sample = jnp.stack([heads(v, bs, ls) for _, v in kv_s])
    stack = lambda sts: [jnp.stack(t) for t in zip(*sts)]
    return (hp.reshape(bp, lp, d), hs.reshape(bs, ls, d), k_prompt, v_prompt, k_sample, v_sample,
            *stack(sts_p), *stack(sts_s))
```

```python
import functools
import math

import jax
import jax.numpy as jnp
from jax import lax
from jax.experimental import pallas as pl
from jax.experimental.pallas import tpu as pltpu

F32 = jnp.float32
BF16 = jnp.bfloat16
HIGHEST = lax.Precision.HIGHEST

D_MODEL = 1024
HEAD_DIM = 64
N_HEADS = 4
BRANCH = N_HEADS * HEAD_DIM
CONV_WIDTH = 3
MB_BLOCK = 256
MB_TOPK = 3
PAGE_SIZE = 128
ROPE_DIMS = HEAD_DIM // 4
ROPE_THETA = 500000.0
RW_LORA = 128
RW_COLS = 3 * BRANCH + RW_LORA
D_FF = 11 * D_MODEL // 4
MLSTM_CHUNK = 64
RWKV_CHUNK = 64
SCAN_BATCH = 4
NORM_EPS = 1e-6
GN_EPS = 64e-5
LANES = 128

Z_GATES = 0
Z_MLSTM = 4096
Z_SCONV = 5120
Z_MOBA = 5888
Z_MIF = 6656
Z_RWKV = 7168
Z_WIDTH = 8192

VMEM_LIMIT = 48 * 1024 * 1024


def _cparams(*sem):
    return pltpu.CompilerParams(dimension_semantics=sem, vmem_limit_bytes=VMEM_LIMIT)


def _sigmoid(x):
    return 1.0 / (1.0 + jnp.exp(-x))


def _head_blockdiag(n):
    r = lax.broadcasted_iota(jnp.int32, (n, n), 0) // HEAD_DIM
    c = lax.broadcasted_iota(jnp.int32, (n, n), 1) // HEAD_DIM
    return (r == c).astype(F32)


def _head_sum(x, bd):
    return jnp.dot(x, bd, precision=HIGHEST, preferred_element_type=F32)


def _dot_nt(a, b):
    return lax.dot_general(a, b, (((1,), (1,)), ((), ())), preferred_element_type=F32)


def _dot_tn(a, b):
    return lax.dot_general(a, b, (((0,), (0,)), ((), ())), preferred_element_type=F32)


def _pick_tile(n, cap):
    t = cap
    while n % t:
        t //= 2
    return t


def _ada_kernel(c_ref, w_ref, b_ref, o_ref):
    c = c_ref[...]
    a = (c * _sigmoid(c)).astype(BF16)
    o_ref[...] = jnp.dot(a, w_ref[...], preferred_element_type=F32) + b_ref[...]


def _ada(c_all, w, b):
    m, d = c_all.shape
    n = w.shape[1]
    tn = 1536
    return pl.pallas_call(
        _ada_kernel,
        grid=(n // tn,),
        in_specs=[pl.BlockSpec((m, d), lambda j: (0, 0)),
                  pl.BlockSpec((d, tn), lambda j: (0, j)),
                  pl.BlockSpec((1, tn), lambda j: (0, j))],
        out_specs=pl.BlockSpec((m, tn), lambda j: (0, j)),
        out_shape=jax.ShapeDtypeStruct((m, n), F32),
        compiler_params=_cparams("parallel"),
        name="ada",
    )(c_all, w, b)


def _nmm_kernel(x_ref, g_ref, sc_ref, sh_ref, w_ref, o_ref, h_sc):
    @pl.when(pl.program_id(1) == 0)
    def _():
        x = x_ref[...]
        y = x * lax.rsqrt(jnp.mean(x * x, axis=-1, keepdims=True) + NORM_EPS) * g_ref[...]
        h_sc[...] = (y * (1.0 + sc_ref[...]) + sh_ref[...]).astype(BF16)

    o_ref[...] = jnp.dot(h_sc[...], w_ref[...], preferred_element_type=F32)


def _norm_mod_matmul(x, g, sc, sh, w, grp, tn):
    n, d = x.shape
    nout = w.shape[1]
    tm, tpg, r = grp["tm"], grp["tpg"], sc.shape[1]
    return pl.pallas_call(
        _nmm_kernel,
        grid=(n // tm, nout // tn),
        in_specs=[pl.BlockSpec((tm, d), lambda i, j: (i, 0)),
                  pl.BlockSpec((1, d), lambda i, j: (0, 0)),
                  pl.BlockSpec((None, r, d), lambda i, j: (i // tpg, 0, 0)),
                  pl.BlockSpec((None, r, d), lambda i, j: (i // tpg, 0, 0)),
                  pl.BlockSpec((d, tn), lambda i, j: (0, j))],
        out_specs=pl.BlockSpec((tm, tn), lambda i, j: (i, j)),
        out_shape=jax.ShapeDtypeStruct((n, nout), F32),
        scratch_shapes=[pltpu.VMEM((tm, d), BF16)],
        compiler_params=_cparams("parallel", "arbitrary"),
        name="norm_mod_matmul",
    )(x, g, sc, sh, w)


def _log_sigmoid(x):
    return jnp.minimum(x, 0.0) - jnp.log(1.0 + jnp.exp(-jnp.abs(x)))


def _mlstm_kernel(zq_ref, zif_ref, bias_ref, g_ref, c0_ref, n0_ref, m0_ref,
                  act_ref, c_out, n_out, m_out, c_sc, n_sc, m_sc, *, chunk, bb):
    t_ = chunk
    ci = pl.program_id(1)

    @pl.when(ci == 0)
    def _():
        c_sc[...] = c0_ref[...]
        n_sc[...] = n0_ref[...]
        m_sc[...] = m0_ref[...]

    row = lax.broadcasted_iota(jnp.int32, (t_, t_), 0)
    col = lax.broadcasted_iota(jnp.int32, (t_, t_), 1)
    causal = col <= row
    eye = row == col
    row1 = lax.broadcasted_iota(jnp.int32, (t_, 1), 0)
    for b, h in [(b, h) for b in range(bb) for h in range(N_HEADS)]:
        zq_b, act_b = zq_ref.at[b], act_ref.at[b]
        c_sc_b, n_sc_b, m_sc_b = c_sc.at[b], n_sc.at[b], m_sc.at[b]
        gates = zif_ref[b] + bias_ref[...]
        logf = _log_sigmoid(gates)
        lo = h * HEAD_DIM
        q = zq_b[:, lo:lo + HEAD_DIM]
        k = zq_b[:, BRANCH + lo:BRANCH + lo + HEAD_DIM] * (HEAD_DIM ** -0.5)
        v = zq_b[:, 2 * BRANCH + lo:2 * BRANCH + lo + HEAD_DIM]
        o = zq_b[:, 3 * BRANCH + lo:3 * BRANCH + lo + HEAD_DIM]
        li_col = gates[:, h:h + 1]
        lf_col = logf[:, N_HEADS + h:N_HEADS + h + 1]
        li_row = jnp.sum(jnp.where(eye, li_col, 0.0), axis=0, keepdims=True)
        lf_row = jnp.sum(jnp.where(eye, lf_col, 0.0), axis=0, keepdims=True)
        b_col = jnp.sum(jnp.where(causal, lf_row, 0.0), axis=1, keepdims=True)
        b_row = jnp.sum(jnp.where(row <= col, lf_col, 0.0), axis=0, keepdims=True)
        m_prev = m_sc_b[h][:, 0:1]
        d = jnp.where(causal, b_col - b_row + li_row, -jnp.inf)
        inter = b_col + m_prev
        mt = jnp.maximum(jnp.max(d, axis=1, keepdims=True), inter)
        qb, kb, vb = q.astype(BF16), k.astype(BF16), v.astype(BF16)
        w = jnp.exp(d - mt) * _dot_nt(qb, kb)
        a_int = jnp.exp(inter - mt)
        c = c_sc_b[h]
        n = n_sc_b[h]
        num = jnp.dot(w.astype(BF16), vb, preferred_element_type=F32) \
            + a_int * jnp.dot(qb, c.astype(BF16), preferred_element_type=F32)
        den = jnp.sum(w, axis=1, keepdims=True) + a_int * jnp.sum(q * n, axis=1, keepdims=True)
        hh = num / jnp.maximum(jnp.abs(den), jnp.exp(-mt))
        hn = hh * lax.rsqrt(jnp.mean(hh * hh, axis=-1, keepdims=True) + NORM_EPS)
        act_b[:, lo:lo + HEAD_DIM] = hn * g_ref[:, lo:lo + HEAD_DIM] * _sigmoid(o)
        b_end = jnp.sum(jnp.where(row1 == t_ - 1, b_col, 0.0), axis=0, keepdims=True)
        g_col = b_end - b_col + li_col
        m_new = jnp.maximum(b_end + m_prev, jnp.max(g_col, axis=0, keepdims=True))
        wk = jnp.exp(g_col - m_new) * k
        decay = jnp.exp(b_end + m_prev - m_new)
        c_sc_b[h] = decay * c + _dot_tn(wk.astype(BF16), vb)
        n_sc_b[h] = decay * n + jnp.sum(wk, axis=0, keepdims=True)
        m_sc_b[h] = jnp.broadcast_to(m_new, (1, LANES))

    @pl.when(ci == pl.num_programs(1) - 1)
    def _():
        c_out[...] = c_sc[...]
        n_out[...] = n_sc[...]
        m_out[...] = m_sc[...]


def _mlstm(z, bias_if, g, c0, n0, m0, grp):
    b_, l_ = grp["B"], grp["L"]
    t_ = min(MLSTM_CHUNK, l_)
    nc = l_ // t_
    h_ = N_HEADS
    bb = _pick_tile(b_, SCAN_BATCH)
    n0 = n0.reshape(b_, h_, 1, HEAD_DIM)
    m0 = jnp.broadcast_to(m0.reshape(b_, h_, 1, 1), (b_, h_, 1, LANES))
    z3 = z.reshape(b_, l_, Z_WIDTH)
    state_spec = lambda shp: pl.BlockSpec((bb,) + shp, lambda b, c: (b, 0, 0, 0))
    act, c1, n1, m1 = pl.pallas_call(
        functools.partial(_mlstm_kernel, chunk=t_, bb=bb),
        grid=(b_ // bb, nc),
        in_specs=[pl.BlockSpec((bb, t_, 4 * BRANCH), lambda b, c: (b, c, Z_MLSTM // (4 * BRANCH))),
                  pl.BlockSpec((bb, t_, LANES), lambda b, c: (b, c, Z_MIF // LANES)),
                  pl.BlockSpec((1, LANES), lambda b, c: (0, 0)),
                  pl.BlockSpec((1, BRANCH), lambda b, c: (0, 0)),
                  state_spec((h_, HEAD_DIM, HEAD_DIM)),
                  state_spec((h_, 1, HEAD_DIM)),
                  state_spec((h_, 1, LANES))],
        out_specs=[pl.BlockSpec((bb, t_, BRANCH), lambda b, c: (b, c, 0)),
                   state_spec((h_, HEAD_DIM, HEAD_DIM)),
                   state_spec((h_, 1, HEAD_DIM)),
                   state_spec((h_, 1, LANES))],
        out_shape=[jax.ShapeDtypeStruct((b_, l_, BRANCH), F32),
                   jax.ShapeDtypeStruct((b_, h_, HEAD_DIM, HEAD_DIM), F32),
                   jax.ShapeDtypeStruct((b_, h_, 1, HEAD_DIM), F32),
                   jax.ShapeDtypeStruct((b_, h_, 1, LANES), F32)],
        scratch_shapes=[pltpu.VMEM((bb, h_, HEAD_DIM, HEAD_DIM), F32),
                        pltpu.VMEM((bb, h_, 1, HEAD_DIM), F32),
                        pltpu.VMEM((bb, h_, 1, LANES), F32)],
        compiler_params=_cparams("parallel", "arbitrary"),
        name="mlstm",
    )(z3, z3, bias_if, g, c0, n0, m0)
    return act.reshape(b_ * l_, BRANCH), c1, n1.reshape(b_, h_, HEAD_DIM), m1[:, :, 0, 0]


def _conv3(p, prev2, prev1, w_ref):
    row = lax.broadcasted_iota(jnp.int32, p.shape, 0)
    p1 = jnp.where(row == 0, prev1, pltpu.roll(p, 1, axis=0))
    p2 = jnp.where(row == 0, prev2, jnp.where(row == 1, prev1, pltpu.roll(p, 2, axis=0)))
    return w_ref[0:1, :] * p2 + w_ref[1:2, :] * p1 + w_ref[2:3, :] * p


def _sconv_kernel(sb_ref, sc_ref, sh_ref, w_ref, buf_ref, out_ref, st_ref, carry, *, tm):
    @pl.when(pl.program_id(1) == 0)
    def _():
        carry[...] = buf_ref[0]

    p = sc_ref[...] * sh_ref[...]
    u = _conv3(p, carry[0:1, :], carry[1:2, :], w_ref)
    out_ref[...] = sb_ref[...] * u
    new = sc_ref[tm - 2:tm, :] * sh_ref[tm - 2:tm, :]
    carry[...] = new
    st_ref[0] = new


def _sconv(z, w, buf, grp):
    b_, l_ = grp["B"], grp["L"]
    tm = _pick_tile(l_, 512)
    nt = l_ // tm
    cb = Z_SCONV // BRANCH
    zspec = lambda j: pl.BlockSpec((tm, BRANCH), lambda b, i: (b * nt + i, cb + j))
    return pl.pallas_call(
        functools.partial(_sconv_kernel, tm=tm),
        grid=(b_, nt),
        in_specs=[zspec(0), zspec(1), zspec(2),
                  pl.BlockSpec((CONV_WIDTH, BRANCH), lambda b, i: (0, 0)),
                  pl.BlockSpec((1, CONV_WIDTH - 1, BRANCH), lambda b, i: (b, 0, 0))],
        out_specs=[pl.BlockSpec((tm, BRANCH), lambda b, i: (b * nt + i, 0)),
                   pl.BlockSpec((1, CONV_WIDTH - 1, BRANCH), lambda b, i: (b, 0, 0))],
        out_shape=[jax.ShapeDtypeStruct((b_ * l_, BRANCH), F32),
                   jax.ShapeDtypeStruct((b_, CONV_WIDTH - 1, BRANCH), F32)],
        scratch_shapes=[pltpu.VMEM((CONV_WIDTH - 1, BRANCH), F32)],
        compiler_params=_cparams("parallel", "arbitrary"),
        name="sconv",
    )(z, z, z, w, buf)


def _qk_prep_kernel(q_ref, k_ref, qg_ref, kg_ref, cos_ref, sa_ref, sb_ref, qo_ref, ko_ref):
    bd = _head_blockdiag(BRANCH)
    cos, sa, sb = cos_ref[...], sa_ref[...], sb_ref[...]
    half = ROPE_DIMS // 2

    def prep(x, g):
        y = x * lax.rsqrt(_head_sum(x * x, bd) * (1.0 / HEAD_DIM) + NORM_EPS) * g
        return y * cos + pltpu.roll(y, BRANCH - half, axis=1) * sa + pltpu.roll(y, half, axis=1) * sb

    qo_ref[...] = prep(q_ref[...], qg_ref[...])
    ko_ref[...] = prep(k_ref[...], kg_ref[...])


def _rope_tables(pos):
    half = ROPE_DIMS // 2
    inv = jnp.exp(-math.log(ROPE_THETA) * jnp.arange(0, ROPE_DIMS, 2, dtype=F32) / ROPE_DIMS)
    ang = pos.astype(F32)[:, None] * inv[None, :]
    cos, sin = jnp.cos(ang), jnp.sin(ang)
    n = pos.shape[0]
    rest = HEAD_DIM - ROPE_DIMS
    c_h = jnp.concatenate([cos, cos, jnp.ones((n, rest), F32)], axis=1)
    sa_h = jnp.concatenate([-sin, jnp.zeros((n, half + rest), F32)], axis=1)
    sb_h = jnp.concatenate([jnp.zeros((n, half), F32), sin, jnp.zeros((n, rest), F32)], axis=1)
    tile = lambda t: jnp.tile(t, (1, N_HEADS))
    return tile(c_h), tile(sa_h), tile(sb_h)


def _qk_prep(z, qg, kg, tables, grp):
    n = grp["B"] * grp["L"]
    ltab = tables[0].shape[0]
    tm = _pick_tile(ltab, 512)
    npos = ltab // tm
    cb = Z_MOBA // BRANCH
    tspec = pl.BlockSpec((tm, BRANCH), lambda i: (i % npos, 0))
    gspec = pl.BlockSpec((1, BRANCH), lambda i: (0, 0))
    ospec = pl.BlockSpec((tm, BRANCH), lambda i: (i, 0))
    return pl.pallas_call(
        _qk_prep_kernel,
        grid=(n // tm,),
        in_specs=[pl.BlockSpec((tm, BRANCH), lambda i: (i, cb)),
                  pl.BlockSpec((tm, BRANCH), lambda i: (i, cb + 1)),
                  gspec, gspec, tspec, tspec, tspec],
        out_specs=[ospec, ospec],
        out_shape=[jax.ShapeDtypeStruct((n, BRANCH), F32)] * 2,
        compiler_params=_cparams("parallel"),
        name="qk_prep",
    )(z, z, qg, kg, *tables)


def _moba_prompt_kernel(q_ref, k_ref, v_ref, o_ref, kmean, *, nb):
    qi = pl.program_id(1)
    tq = MB_BLOCK

    @pl.when(qi == 0)
    def _():
        kmean[...] = jnp.zeros_like(kmean)
        for n in range(nb):
            kmean[n:n + 1, :] = jnp.mean(k_ref[n * MB_BLOCK:(n + 1) * MB_BLOCK, :], axis=0, keepdims=True)

    lane = lax.broadcasted_iota(jnp.int32, (tq, LANES), 1)
    row = lax.broadcasted_iota(jnp.int32, (tq, tq), 0)
    col = lax.broadcasted_iota(jnp.int32, (tq, tq), 1)
    scale = HEAD_DIM ** -0.5
    own0 = pl.multiple_of(qi * MB_BLOCK, MB_BLOCK)
    for h in range(N_HEADS):
        hs = slice(h * HEAD_DIM, (h + 1) * HEAD_DIM)
        qh = q_ref[:, hs]
        qb = qh.astype(BF16)
        s = lax.dot_general(qh, kmean[:, hs], (((1,), (1,)), ((), ())),
                            precision=HIGHEST, preferred_element_type=F32)
        valid = lane < qi
        s = jnp.where(valid, s, -jnp.inf)
        rank = jnp.zeros((tq, LANES), jnp.int32)
        for m in range(nb):
            sm = s[:, m:m + 1]
            rank += ((sm > s) | ((sm == s) & (m < lane))).astype(jnp.int32)
        sel = (valid & (rank < MB_TOPK)).astype(F32)
        kb = k_ref[pl.ds(own0, MB_BLOCK), hs].astype(BF16)
        vb = v_ref[pl.ds(own0, MB_BLOCK), hs].astype(BF16)
        sc = jnp.where(col <= row, _dot_nt(qb, kb) * scale, -jnp.inf)
        m_i = jnp.max(sc, axis=1, keepdims=True)
        p = jnp.exp(sc - m_i)
        l_i = jnp.sum(p, axis=1, keepdims=True)
        acc = jnp.dot(p.astype(BF16), vb, preferred_element_type=F32)

        def body(n, carry):
            m_i, l_i, acc = carry
            k0 = pl.multiple_of(n * MB_BLOCK, MB_BLOCK)
            kb = k_ref[pl.ds(k0, MB_BLOCK), hs].astype(BF16)
            vb = v_ref[pl.ds(k0, MB_BLOCK), hs].astype(BF16)
            seln = jnp.sum(jnp.where(lane == n, sel, 0.0), axis=1, keepdims=True) > 0.0
            sc = jnp.where(seln, _dot_nt(qb, kb) * scale, -jnp.inf)
            m_new = jnp.maximum(m_i, jnp.max(sc, axis=1, keepdims=True))
            alpha = jnp.exp(m_i - m_new)
            p = jnp.exp(sc - m_new)
            l_new = alpha * l_i + jnp.sum(p, axis=1, keepdims=True)
            acc_new = alpha * acc + jnp.dot(p.astype(BF16), vb, preferred_element_type=F32)
            return m_new, l_new, acc_new

        m_i, l_i, acc = lax.fori_loop(0, qi, body, (m_i, l_i, acc))
        o_ref[:, hs] = acc / l_i


def _moba_prompt(q, k, z, grp):
    b_, l_ = grp["B"], grp["L"]
    assert l_ % MB_BLOCK == 0 and l_ // MB_BLOCK <= LANES
    nb = l_ // MB_BLOCK
    return pl.pallas_call(
        functools.partial(_moba_prompt_kernel, nb=nb),
        grid=(b_, nb),
        in_specs=[pl.BlockSpec((MB_BLOCK, BRANCH), lambda b, i: (b * nb + i, 0)),
                  pl.BlockSpec((l_, BRANCH), lambda b, i: (b, 0)),
                  pl.BlockSpec((l_, BRANCH), lambda b, i: (b, Z_MOBA // BRANCH + 2))],
        out_specs=pl.BlockSpec((MB_BLOCK, BRANCH), lambda b, i: (b * nb + i, 0)),
        out_shape=jax.ShapeDtypeStruct((b_ * l_, BRANCH), F32),
        scratch_shapes=[pltpu.VMEM((LANES, BRANCH), F32)],
        compiler_params=_cparams("parallel", "arbitrary"),
        name="moba_prompt",
    )(q, k, z)


def _moba_sample_kernel(pt_ref, q_ref, kn_ref, vn_ref, *refs, nblk, dec, gsz):
    ppb = MB_BLOCK // PAGE_SIZE
    k_refs, v_refs = refs[:ppb * gsz], refs[ppb * gsz:2 * ppb * gsz]
    o_ref, kmean_s, m_s, l_s, o_s = refs[2 * ppb * gsz:]
    step = pl.program_id(1)
    nq = N_HEADS * dec
    scale = HEAD_DIM ** -0.5
    lane = lax.broadcasted_iota(jnp.int32, (nq, LANES), 1)
    lane_c = lax.broadcasted_iota(jnp.int32, (BRANCH, LANES), 1)

    @pl.when(step == 0)
    def _():
        kmean_s[...] = jnp.zeros_like(kmean_s)
        m_s[...] = jnp.full_like(m_s, -jnp.inf)
        l_s[...] = jnp.zeros_like(l_s)

    row_head = jnp.concatenate([jnp.full((dec, BRANCH), h, jnp.int32) for h in range(N_HEADS)], axis=0)
    lane_head = lax.broadcasted_iota(jnp.int32, (nq, BRANCH), 1) // HEAD_DIM
    qbd = jnp.where(row_head == lane_head, jnp.concatenate([q_ref[...]] * N_HEADS, axis=0), 0.0)
    qbd_b = qbd.astype(BF16)
    kmean, m_all, l_all = kmean_s[...], m_s[...], l_s[...]
    for g in range(gsz):
        blk = step * gsz + g
        kt = jnp.concatenate([k_refs[ppb * g + j][...] for j in range(ppb)], axis=1)
        vt = jnp.concatenate([v_refs[ppb * g + j][...] for j in range(ppb)], axis=1)
        s = jnp.dot(qbd_b, kt.astype(BF16), preferred_element_type=F32) * scale
        mx = jnp.max(s, axis=1, keepdims=True)
        p = jnp.exp(s - mx)
        o_s[blk] = _dot_nt(p.astype(BF16), vt.astype(BF16))
        kmean = jnp.where(lane_c == blk, jnp.mean(kt, axis=1, keepdims=True), kmean)
        m_all = jnp.where(lane == blk, mx, m_all)
        l_all = jnp.where(lane == blk, jnp.sum(p, axis=1, keepdims=True), l_all)
    kmean_s[...] = kmean
    m_s[...] = m_all
    l_s[...] = l_all

    @pl.when(step == pl.num_programs(1) - 1)
    def _():
        zpad = jnp.zeros((LANES - dec, BRANCH), F32)
        knew = jnp.concatenate([kn_ref[...], zpad], axis=0)
        vnew = jnp.concatenate([vn_ref[...], zpad], axis=0)
        rowq = jnp.concatenate([lax.broadcasted_iota(jnp.int32, (dec, LANES), 0)] * N_HEADS, axis=0)
        lane_f = lane.astype(F32)
        s = jnp.dot(qbd, kmean_s[...], precision=HIGHEST, preferred_element_type=F32)
        s = jnp.where(lane < nblk, s, -jnp.inf)
        sel = jnp.zeros((nq, LANES), jnp.bool_)
        for _ in range(MB_TOPK):
            mxv = jnp.max(s, axis=1, keepdims=True)
            idx = jnp.min(jnp.where(s == mxv, lane_f, 2.0 * LANES), axis=1, keepdims=True)
            pick = lane_f == idx
            sel = sel | pick
            s = jnp.where(pick, -jnp.inf, s)
        sel = sel & (lane < nblk)
        s_own = jnp.where(lane <= rowq, _dot_nt(qbd_b, knew.astype(BF16)) * scale, -jnp.inf)
        m_own = jnp.max(s_own, axis=1, keepdims=True)
        p_own = jnp.exp(s_own - m_own)
        l_own = jnp.sum(p_own, axis=1, keepdims=True)
        o_own = jnp.dot(p_own.astype(BF16), vnew.astype(BF16), preferred_element_type=F32)
        mb = m_s[...]
        mtot = jnp.maximum(jnp.max(jnp.where(sel, mb, -jnp.inf), axis=1, keepdims=True), m_own)
        wgt = jnp.where(sel, jnp.exp(mb - mtot), 0.0)
        a_own = jnp.exp(m_own - mtot)
        ltot = jnp.sum(wgt * l_s[...], axis=1, keepdims=True) + l_own * a_own
        acc = o_own * a_own
        for j in range(nblk):
            acc = acc + wgt[:, j:j + 1] * o_s[j]
        res = acc / ltot
        for h in range(N_HEADS):
            hs = slice(h * HEAD_DIM, (h + 1) * HEAD_DIM)
            o_ref[:, hs] = res[h * dec:(h + 1) * dec, hs]


def _moba_sample(q, k, v, pool_k, pool_v, page_table, layer, grp):
    b_, dec = grp["B"], grp["L"]
    n_pages = page_table.shape[1]
    ppb = MB_BLOCK // PAGE_SIZE
    assert n_pages % ppb == 0
    nblk = n_pages // ppb
    assert MB_TOPK <= nblk <= LANES and dec <= LANES and dec % 8 == 0
    gsz = max(g for g in range(1, 9) if nblk % g == 0)
    as_pages = lambda t: t.transpose(0, 1, 3, 4, 2).reshape(t.shape[0], t.shape[1], BRANCH, PAGE_SIZE)
    pool_k, pool_v = as_pages(pool_k), as_pages(pool_v)
    pt = page_table.reshape(-1)

    def page(j):
        return pl.BlockSpec((None, None, BRANCH, PAGE_SIZE),
                            lambda b, s, pt: (layer, pt[b * n_pages + s * gsz * ppb + j], 0, 0))

    pages = [page(j) for j in range(gsz * ppb)]
    new = pl.BlockSpec((dec, BRANCH), lambda b, s, pt: (b, 0))
    nq = N_HEADS * dec
    return pl.pallas_call(
        functools.partial(_moba_sample_kernel, nblk=nblk, dec=dec, gsz=gsz),
        grid_spec=pltpu.PrefetchScalarGridSpec(
            num_scalar_prefetch=1,
            grid=(b_, nblk // gsz),
            in_specs=[new, new, new] + pages + pages,
            out_specs=new,
            scratch_shapes=[pltpu.VMEM((BRANCH, LANES), F32),
                            pltpu.VMEM((nq, LANES), F32),
                            pltpu.VMEM((nq, LANES), F32),
                            pltpu.VMEM((nblk, nq, BRANCH), F32)]),
        out_shape=jax.ShapeDtypeStruct((b_ * dec, BRANCH), F32),
        compiler_params=_cparams("parallel", "arbitrary"),
        name="moba_sample",
    )(pt, q, k, v, *([pool_k] * (gsz * ppb)), *([pool_v] * (gsz * ppb)))


def _rwkv_prep_kernel(z_ref, prev_ref, mu_ref, wlr_ref, w0_ref, a0_ref, kks_ref, kas_ref, rk_ref,
                      r_o, w_o, k_o, v_o, kk_o, kka_o, bonus_o, g_o, carry, *, tm):
    @pl.when(pl.program_id(1) == 0)
    def _():
        carry[...] = prev_ref[0]

    z = z_ref[...]
    row = lax.broadcasted_iota(jnp.int32, z.shape, 0)
    zs = jnp.where(row == 0, carry[...], pltpu.roll(z, 1, axis=0))
    carry[...] = z_ref[tm - 1:tm, :]
    zz = z + mu_ref[...] * (zs - z)
    r = zz[:, 0:BRANCH]
    k = zz[:, BRANCH:2 * BRANCH]
    v = zz[:, 2 * BRANCH:3 * BRANCH]
    lr = zz[:, 3 * BRANCH:3 * BRANCH + RW_LORA]
    lane = lax.broadcasted_iota(jnp.int32, lr.shape, 1)
    lr_in = jnp.where(lane < 32, jnp.tanh(lr), jnp.where(lane < 64, lr, _sigmoid(lr)))
    lo = jnp.dot(lr_in.astype(BF16), wlr_ref[...], preferred_element_type=F32)
    log_decay = -math.exp(-0.5) * _sigmoid(w0_ref[...] + lo[:, 0:BRANCH])
    a = _sigmoid(a0_ref[...] + lo[:, BRANCH:2 * BRANCH])
    bd = _head_blockdiag(BRANCH)
    kk = k * kks_ref[...]
    kk = kk * lax.rsqrt(jnp.maximum(_head_sum(kk * kk, bd), 1e-24))
    k2 = k * (1.0 + (a - 1.0) * kas_ref[...])
    r_o[...] = r
    w_o[...] = log_decay
    k_o[...] = k2
    v_o[...] = v
    kk_o[...] = kk
    kka_o[...] = kk * a
    bonus_o[...] = _head_sum(r * k2 * rk_ref[...], bd) * v
    g_o[...] = lo[:, 2 * BRANCH:3 * BRANCH]


def _rwkv_prep(z, prev, p, grp):
    b_, l_ = grp["B"], grp["L"]
    tm = _pick_tile(l_, 256)
    nt = l_ // tm
    zw = Z_WIDTH - Z_RWKV
    vec = pl.BlockSpec((1, BRANCH), lambda b, i: (0, 0))
    ospec = pl.BlockSpec((tm, BRANCH), lambda b, i: (b * nt + i, 0))
    return pl.pallas_call(
        functools.partial(_rwkv_prep_kernel, tm=tm),
        grid=(b_, nt),
        in_specs=[pl.BlockSpec((tm, zw), lambda b, i: (b * nt + i, Z_RWKV // zw)),
                  pl.BlockSpec((1, 1, zw), lambda b, i: (b, 0, 0)),
                  pl.BlockSpec((1, zw), lambda b, i: (0, 0)),
                  pl.BlockSpec((RW_LORA, 3 * BRANCH), lambda b, i: (0, 0)),
                  vec, vec, vec, vec, vec],
        out_specs=[ospec] * 8,
        out_shape=[jax.ShapeDtypeStruct((b_ * l_, BRANCH), F32)] * 8,
        scratch_shapes=[pltpu.VMEM((1, zw), F32)],
        compiler_params=_cparams("parallel", "arbitrary"),
        name="rwkv_prep",
    )(z, prev, p["mu"], p["wlr"], p["w0"], p["a0"], p["kk"], p["ka"], p["rk"])


def _rwkv_chunk_kernel(r_ref, lw_ref, k_ref, v_ref, kk_ref, kka_ref, h0_ref, y_ref, h_out, h_sc, *, chunk, bb):
    t_ = chunk
    ci = pl.program_id(1)

    @pl.when(ci == 0)
    def _():
        h_sc[...] = h0_ref[...]

    row = lax.broadcasted_iota(jnp.int32, (t_, t_), 0)
    col = lax.broadcasted_iota(jnp.int32, (t_, t_), 1)
    lower = row >= col
    strict = row > col
    eye_t = (row == col).astype(F32)
    eye_d = (lax.broadcasted_iota(jnp.int32, (HEAD_DIM, HEAD_DIM), 0)
             == lax.broadcasted_iota(jnp.int32, (HEAD_DIM, HEAD_DIM), 1))
    row1 = lax.broadcasted_iota(jnp.int32, (t_, 1), 0)

    for b in range(bb):
        _rwkv_chunk_row(r_ref[b], lw_ref[b], k_ref[b], v_ref[b], kk_ref[b], kka_ref[b], y_ref.at[b], h_sc.at[b],
                        t_, lower, strict, eye_t, eye_d, row1)

    @pl.when(ci == pl.num_programs(1) - 1)
    def _():
        h_out[...] = h_sc[...]


def _rwkv_chunk_row(r, lw, k, v_all, kk, kka, y_ref, h_sc, t_, lower, strict, eye_t, eye_d, row1):
    cum = jnp.dot(lower.astype(F32), lw, precision=HIGHEST, preferred_element_type=F32)
    p_in = jnp.exp(cum)
    p_inv = jnp.exp(-cum)
    p_end = jnp.sum(jnp.where(row1 == t_ - 1, p_in, 0.0), axis=0, keepdims=True)
    kkm = kk * jnp.exp(cum - lw)
    rp = r * p_in
    kh = k * p_inv
    ah = kka * p_inv
    khe = kh * p_end
    ahe = ah * p_end
    for h in range(N_HEADS):
        hs = slice(h * HEAD_DIM, (h + 1) * HEAD_DIM)
        lhs = jnp.concatenate([kkm[:, hs], rp[:, hs]], axis=0).astype(BF16)
        vb = v_all[:, hs].astype(BF16)
        gk = _dot_nt(lhs, kh[:, hs].astype(BF16))
        ga = _dot_nt(lhs, ah[:, hs].astype(BF16))
        a_k = jnp.where(strict, gk[:t_], 0.0)
        b_k = jnp.where(lower, gk[t_:], 0.0)
        nmat = jnp.where(strict, -ga[:t_], 0.0)
        b_a = jnp.where(lower, ga[t_:], 0.0)
        m = eye_t + nmat
        pw = nmat
        span = 2
        while span < t_:
            pw = jnp.dot(pw, pw, precision=HIGHEST, preferred_element_type=F32)
            m = m + jnp.dot(m, pw, precision=HIGHEST, preferred_element_type=F32)
            span *= 2
        h0 = h_sc[h]
        x0 = jnp.dot(lhs, h0.astype(BF16), preferred_element_type=F32)
        kv = jnp.dot(jnp.concatenate([a_k, b_k], axis=0).astype(BF16), vb, preferred_element_type=F32)
        u = jnp.dot(m.astype(BF16), (x0[:t_] + kv[:t_]).astype(BF16), preferred_element_type=F32)
        ub = u.astype(BF16)
        y_ref[:, hs] = x0[t_:] + kv[t_:] - jnp.dot(b_a.astype(BF16), ub, preferred_element_type=F32)
        pe_col = jnp.sum(jnp.where(eye_d, p_end[:, hs], 0.0), axis=1, keepdims=True)
        h_sc[h] = pe_col * h0 + _dot_tn(khe[:, hs].astype(BF16), vb) - _dot_tn(ahe[:, hs].astype(BF16), ub)


def _rwkv_scan(r, lw, k, v, kk, kka, s0, grp):
    b_, l_ = grp["B"], grp["L"]
    chunk = _pick_tile(l_, RWKV_CHUNK)
    nc = l_ // chunk
    bb = _pick_tile(b_, SCAN_BATCH)
    seq = pl.BlockSpec((bb, chunk, BRANCH), lambda b, c: (b, c, 0))
    st = pl.BlockSpec((bb, N_HEADS, HEAD_DIM, HEAD_DIM), lambda b, c: (b, 0, 0, 0))
    as3 = lambda t: t.reshape(b_, l_, BRANCH)
    y, h1 = pl.pallas_call(
        functools.partial(_rwkv_chunk_kernel, chunk=chunk, bb=bb),
        grid=(b_ // bb, nc),
        in_specs=[seq] * 6 + [st],
        out_specs=[seq, st],
        out_shape=[jax.ShapeDtypeStruct((b_, l_, BRANCH), F32),
                   jax.ShapeDtypeStruct((b_, N_HEADS, HEAD_DIM, HEAD_DIM), F32)],
        scratch_shapes=[pltpu.VMEM((bb, N_HEADS, HEAD_DIM, HEAD_DIM), F32)],
        compiler_params=_cparams("parallel", "arbitrary"),
        name="rwkv_scan",
    )(as3(r), as3(lw), as3(k), as3(v), as3(kk), as3(kka), jnp.swapaxes(s0, -1, -2))
    return y.reshape(b_ * l_, BRANCH), jnp.swapaxes(h1, -1, -2)


def _merge_kernel(ml_ref, sc_ref, mb_ref, y_ref, bonus_ref, g_ref, gn_ref, gates_ref, x_ref, gt_ref,
                  mp_ref, sp_ref, ap_ref, rp_ref, wo_ref, o_ref):
    bd = _head_blockdiag(BRANCH)
    y = y_ref[...]
    yc = y - _head_sum(y, bd) * (1.0 / HEAD_DIM)
    yn = yc * lax.rsqrt(_head_sum(yc * yc, bd) * (1.0 / HEAD_DIM) + GN_EPS)
    rw = (yn * gn_ref[...] + bonus_ref[...]) * g_ref[...]
    acts = (ml_ref[...], sc_ref[...], mb_ref[...], rw)
    projs = (mp_ref, sp_ref, ap_ref, rp_ref)
    merged = None
    for j in range(4):
        br = jnp.dot(acts[j].astype(BF16), projs[j][...], preferred_element_type=F32)
        term = _sigmoid(gates_ref[:, j * D_MODEL:(j + 1) * D_MODEL]) * br
        merged = term if merged is None else merged + term
    o_ref[...] = x_ref[...] + gt_ref[...] * jnp.dot(merged.astype(BF16), wo_ref[...], preferred_element_type=F32)


def _merge(ml, sc, mb, y, bonus, g, gn, z, x, gt, projs, w_out, grp):
    n, d = x.shape
    tm = min(256, grp["tm"])
    tpg = grp["tpg"] * (grp["tm"] // tm)
    r = gt.shape[1]
    if r > 1:
        assert r == grp["tm"] and tpg == grp["tm"] // tm
        gt = gt.reshape(r // tm, tm, d)
        gt_spec = pl.BlockSpec((None, tm, d), lambda i: (i, 0, 0))
    else:
        gt_spec = pl.BlockSpec((None, 1, d), lambda i: (i // tpg, 0, 0))
    act = pl.BlockSpec((tm, BRANCH), lambda i: (i, 0))
    proj = pl.BlockSpec((BRANCH, d), lambda i: (0, 0))
    return pl.pallas_call(
        _merge_kernel,
        grid=(n // tm,),
        in_specs=[act] * 6 + [pl.BlockSpec((1, BRANCH), lambda i: (0, 0)),
                              pl.BlockSpec((tm, 4 * d), lambda i: (i, 0)),
                              pl.BlockSpec((tm, d), lambda i: (i, 0)),
                              gt_spec, proj, proj, proj, proj,
                              pl.BlockSpec((d, d), lambda i: (0, 0))],
        out_specs=pl.BlockSpec((tm, d), lambda i: (i, 0)),
        out_shape=jax.ShapeDtypeStruct((n, d), F32),
        compiler_params=_cparams("parallel"),
        name="merge",
    )(ml, sc, mb, y, bonus, g, gn, z, x, gt, *projs, w_out)


def _ffn_act_kernel(ua_ref, ul_ref, w_ref, buf_ref, o_ref, carry, *, tm):
    @pl.when(pl.program_id(1) == 0)
    def _():
        carry[...] = buf_ref[0]

    ua = ua_ref[...]
    u = _conv3(ua, carry[0:1, :], carry[1:2, :], w_ref)
    carry[...] = ua_ref[tm - 2:tm, :]
    o_ref[...] = (u * _sigmoid(u) * ul_ref[...]).astype(BF16)


def _ffn_act(up, w, buf, grp):
    b_, l_ = grp["B"], grp["L"]
    tm = _pick_tile(l_, 256)
    nt = l_ // tm
    half = lambda j: pl.BlockSpec((tm, D_FF), lambda b, i: (b * nt + i, j))
    return pl.pallas_call(
        functools.partial(_ffn_act_kernel, tm=tm),
        grid=(b_, nt),
        in_specs=[half(0), half(1),
                  pl.BlockSpec((CONV_WIDTH, D_FF), lambda b, i: (0, 0)),
                  pl.BlockSpec((1, CONV_WIDTH - 1, D_FF), lambda b, i: (b, 0, 0))],
        out_specs=pl.BlockSpec((tm, D_FF), lambda b, i: (b * nt + i, 0)),
        out_shape=jax.ShapeDtypeStruct((b_ * l_, D_FF), BF16),
        scratch_shapes=[pltpu.VMEM((CONV_WIDTH - 1, D_FF), F32)],
        compiler_params=_cparams("parallel", "arbitrary"),
        name="ffn_act",
    )(up, up, w, buf)


def _mm_res_kernel(a_ref, w_ref, x_ref, gt_ref, o_ref):
    o_ref[...] = x_ref[...] + gt_ref[...] * jnp.dot(a_ref[...], w_ref[...], preferred_element_type=F32)


def _mm_res(a, w, x, gt, grp):
    n, d = x.shape
    kdim = a.shape[1]
    tm, tpg, r = min(512, grp["tm"]), grp["tpg"] * (grp["tm"] // min(512, grp["tm"])), gt.shape[1]
    if r > 1:
        gt = gt.reshape(r // tm, tm, d)
        gt_spec = pl.BlockSpec((None, tm, d), lambda i: (i, 0, 0))
    else:
        gt_spec = pl.BlockSpec((None, 1, d), lambda i: (i // tpg, 0, 0))
    return pl.pallas_call(
        _mm_res_kernel,
        grid=(n // tm,),
        in_specs=[pl.BlockSpec((tm, kdim), lambda i: (i, 0)),
                  pl.BlockSpec((kdim, d), lambda i: (0, 0)),
                  pl.BlockSpec((tm, d), lambda i: (i, 0)),
                  gt_spec],
        out_specs=pl.BlockSpec((tm, d), lambda i: (i, 0)),
        out_shape=jax.ShapeDtypeStruct((n, d), F32),
        compiler_params=_cparams("parallel"),
        name="mm_res",
    )(a, w, x, gt)


def _prep_layer(l, P):
    d = D_MODEL
    w_in = P["w_in"][l]
    o_mif = 4 * BRANCH
    o_sb = o_mif + 2 * N_HEADS
    o_aq = o_sb + 3 * BRANCH
    o_rw = o_aq + 3 * BRANCH
    o_g = o_rw + RW_COLS
    w_in_p = jnp.concatenate([
        w_in[:, o_g:o_g + 4 * d], w_in[:, 0:o_mif], w_in[:, o_sb:o_aq], w_in[:, o_aq:o_rw],
        w_in[:, o_mif:o_sb], jnp.zeros((d, Z_RWKV - Z_MIF - 2 * N_HEADS), F32),
        w_in[:, o_rw:o_g], jnp.zeros((d, Z_WIDTH - Z_RWKV - RW_COLS), F32)], axis=1).astype(BF16)
    bias_if = jnp.concatenate([P["m_bi"][l], P["m_bf"][l], jnp.zeros((LANES - 2 * N_HEADS,), F32)])[None, :]
    zw = Z_WIDTH - Z_RWKV
    wlr = jnp.zeros((RW_LORA, 3 * BRANCH), F32)
    wlr = wlr.at[0:32, 0:BRANCH].set(P["r_wB"][l])
    wlr = wlr.at[32:64, BRANCH:2 * BRANCH].set(P["r_aB"][l])
    wlr = wlr.at[64:128, 2 * BRANCH:].set(P["r_gB"][l])
    row = lambda t: t[None, :]
    rw = dict(mu=jnp.pad(P["r_mu"][l], (0, zw - RW_COLS))[None, :], wlr=wlr.astype(BF16),
              w0=row(P["r_w0"][l]), a0=row(P["r_a0"][l]), kk=row(P["r_kk"][l]), ka=row(P["r_ka"][l]),
              rk=row(P["r_rk"][l]))
    bf = lambda name: P[name][l].astype(BF16)
    return dict(
        w_ada=bf("w_ada"), b_ada=row(P["b_ada"][l]), w_in=w_in_p, bias_if=bias_if,
        norm_mix_g=row(P["norm_mix_g"][l]), norm_ffn_g=row(P["norm_ffn_g"][l]),
        m_norm_g=row(P["m_norm_g"][l]), s_conv=P["s_conv"][l],
        qg=row(jnp.tile(P["a_qnorm"][l], N_HEADS)), kg=row(jnp.tile(P["a_knorm"][l], N_HEADS)),
        rw=rw, r_norm_g=row(P["r_norm_g"][l]),
        projs=(bf("m_proj"), bf("s_proj"), bf("a_proj"), bf("r_proj")), w_out=bf("w_out"),
        f_up=bf("f_up"), f_conv=P["f_conv"][l], f_down=bf("f_down"))


def _layer(x, mod, W, state, tables, attend, grp):
    mc, mn, mm, sbuf, rs, rshift, fbuf = state
    b_, l_ = grp["B"], grp["L"]
    sh_m, sc_m, gt_m, sh_f, sc_f, gt_f = mod
    z = _norm_mod_matmul(x, W["norm_mix_g"], sc_m, sh_m, W["w_in"], grp, 1024)
    ml, mc, mn, mm = _mlstm(z, W["bias_if"], W["m_norm_g"], mc, mn, mm, grp)
    sc, sbuf = _sconv(z, W["s_conv"], sbuf, grp)
    q, k = _qk_prep(z, W["qg"], W["kg"], tables, grp)
    v = z[:, Z_MOBA + 2 * BRANCH:Z_MOBA + 3 * BRANCH]
    mb = attend(q, k, z, v)
    zw = Z_WIDTH - Z_RWKV
    prev = jnp.pad(rshift, ((0, 0), (0, zw - RW_COLS)))[:, None, :]
    r, w, k2, vv, kk, kka, bonus, g = _rwkv_prep(z, prev, W["rw"], grp)
    y, rs_t = _rwkv_scan(r, w, k2, vv, kk, kka, rs, grp)
    rshift = z.reshape(b_, l_, Z_WIDTH)[:, -1, Z_RWKV:Z_RWKV + RW_COLS]
    x = _merge(ml, sc, mb, y, bonus, g, W["r_norm_g"], z, x, gt_m, W["projs"], W["w_out"], grp)
    up = _norm_mod_matmul(x, W["norm_ffn_g"], sc_f, sh_f, W["f_up"], grp, 1408)
    act = _ffn_act(up, W["f_conv"], fbuf, grp)
    fbuf = up.reshape(b_, l_, 2 * D_FF)[:, l_ - (CONV_WIDTH - 1):, :D_FF]
    x = _mm_res(act, W["f_down"], x, gt_f, grp)
    return x, (k, v), (mc, mn, mm, sbuf, rs_t, rshift, fbuf)


def kernel(x_prompt, x_sample, c_prompt, c_sample, cache_k, cache_v, page_table, state_mlstm_c, state_mlstm_n, state_mlstm_m, state_conv, state_rwkv, state_rwkv_shift, state_ffn_conv, norm_mix_g, norm_ffn_g, w_ada, b_ada, w_in, m_bi, m_bf, m_norm_g, m_proj, s_conv, s_proj, a_qnorm, a_knorm, a_proj, r_mu, r_w0, r_wB, r_a0, r_aB, r_gB, r_kk, r_ka, r_rk, r_norm_g, r_proj, w_out, f_up, f_conv, f_down):
    P = dict(norm_mix_g=norm_mix_g, norm_ffn_g=norm_ffn_g, w_ada=w_ada, b_ada=b_ada, w_in=w_in,
             m_bi=m_bi, m_bf=m_bf, m_norm_g=m_norm_g, m_proj=m_proj, s_conv=s_conv, s_proj=s_proj,
             a_qnorm=a_qnorm, a_knorm=a_knorm, a_proj=a_proj, r_mu=r_mu, r_w0=r_w0, r_wB=r_wB,
             r_a0=r_a0, r_aB=r_aB, r_gB=r_gB, r_kk=r_kk, r_ka=r_ka, r_rk=r_rk, r_norm_g=r_norm_g,
             r_proj=r_proj, w_out=w_out, f_up=f_up, f_conv=f_conv, f_down=f_down)
    depth = w_in.shape[0]
    bp, lp, d = x_prompt.shape
    bs, ls, _ = x_sample.shape
    n_s = bs * ls
    past = page_table.shape[1] * PAGE_SIZE
    tm_p = _pick_tile(lp, 1024)
    grp_p = dict(B=bp, L=lp, tm=tm_p, tpg=lp // tm_p)
    grp_s = dict(B=bs, L=ls, tm=n_s, tpg=1)
    assert n_s % 8 == 0 and n_s <= 1024

    zeros = lambda *s: jnp.zeros(s, F32)
    st_p = (zeros(bp, N_HEADS, HEAD_DIM, HEAD_DIM), zeros(bp, N_HEADS, HEAD_DIM), zeros(bp, N_HEADS),
            zeros(bp, CONV_WIDTH - 1, BRANCH), zeros(bp, N_HEADS, HEAD_DIM, HEAD_DIM), zeros(bp, RW_COLS),
            zeros(bp, CONV_WIDTH - 1, D_FF))
    tab_p = _rope_tables(jnp.arange(lp, dtype=jnp.int32))
    tab_s = tuple(jnp.tile(t, (bs, 1)) for t in _rope_tables(past + jnp.arange(ls, dtype=jnp.int32)))

    c_all = jnp.concatenate([c_prompt, c_sample], axis=0)
    hp = x_prompt.reshape(bp * lp, d)
    hs = x_sample.reshape(n_s, d)
    kv_p, kv_s, sts_p, sts_s = [], [], [], []
    for l in range(depth):
        W = _prep_layer(l, P)
        mod = _ada(c_all, W["w_ada"], W["b_ada"])
        mods = [mod[:, j * d:(j + 1) * d] for j in range(6)]
        mod_p = [m[:bp][:, None, :] for m in mods]
        mod_s = [jnp.repeat(m[bp:], ls, axis=0)[None] for m in mods]

        attend_p = lambda q, k, z, v: _moba_prompt(q, k, z, grp_p)
        hp, kv, st = _layer(hp, mod_p, W, st_p, tab_p, attend_p, grp_p)
        kv_p.append(kv)
        sts_p.append(st)

        prev = (state_mlstm_c[l], state_mlstm_n[l], state_mlstm_m[l], state_conv[l], state_rwkv[l],
                state_rwkv_shift[l], state_ffn_conv[l])
        attend_s = lambda q, k, z, v: _moba_sample(q, k, v, cache_k, cache_v, page_table, l, grp_s)
        hs, kv, st = _layer(hs, mod_s, W, prev, tab_s, attend_s, grp_s)
        kv_s.append(kv)
        sts_s.append(st)

    heads = lambda t, b_, l_: t.reshape(b_, l_, N_HEADS, HEAD_DIM)
    k_prompt = jnp.stack([heads(k, bp, lp) for k, _ in kv_p])
    v_prompt = jnp.stack([heads(v, bp, lp) for _, v in kv_p])
    k_sample = jnp.stack([heads(k, bs, ls) for k, _ in kv_s])
    v_sample = jnp.stack([heads(v, bs, ls) for _, v in kv_s])
    stack = lambda sts: [jnp.stack(t) for t in zip(*sts)]
    return (hp.reshape(bp, lp, d), hs.reshape(bs, ls, d), k_prompt, v_prompt, k_sample, v_sample,
            *stack(sts_p), *stack(sts_s))
```

```python
import functools
import math

import jax
import jax.numpy as jnp
from jax import lax
from jax.experimental import pallas as pl
from jax.experimental.pallas import tpu as pltpu

F32 = jnp.float32
BF16 = jnp.bfloat16
HIGHEST = lax.Precision.HIGHEST

D_MODEL = 1024
HEAD_DIM = 64
N_HEADS = 4
BRANCH = N_HEADS * HEAD_DIM
CONV_WIDTH = 3
MB_BLOCK = 256
MB_TOPK = 3
PAGE_SIZE = 128
ROPE_DIMS = HEAD_DIM // 4
ROPE_THETA = 500000.0
RW_LORA = 128
RW_COLS = 3 * BRANCH + RW_LORA
D_FF = 11 * D_MODEL // 4
MLSTM_CHUNK = 64
RWKV_CHUNK = 64
SCAN_BATCH = 1
RWKV_BATCH = 2
NORM_EPS = 1e-6
GN_EPS = 64e-5
LANES = 128

Z_GATES = 0
Z_MLSTM = 4096
Z_SCONV = 5120
Z_MOBA = 5888
Z_MIF = 6656
Z_RWKV = 7168
Z_WIDTH = 8192

VMEM_LIMIT = 48 * 1024 * 1024


def _cparams(*sem):
    return pltpu.CompilerParams(dimension_semantics=sem, vmem_limit_bytes=VMEM_LIMIT)


def _sigmoid(x):
    return 1.0 / (1.0 + jnp.exp(-x))


def _head_blockdiag(n):
    r = lax.broadcasted_iota(jnp.int32, (n, n), 0) // HEAD_DIM
    c = lax.broadcasted_iota(jnp.int32, (n, n), 1) // HEAD_DIM
    return (r == c).astype(F32)


def _head_sum(x, bd):
    return jnp.dot(x, bd, precision=HIGHEST, preferred_element_type=F32)


def _dot_nt(a, b):
    return lax.dot_general(a, b, (((1,), (1,)), ((), ())), preferred_element_type=F32)


def _dot_tn(a, b):
    return lax.dot_general(a, b, (((0,), (0,)), ((), ())), preferred_element_type=F32)


def _pick_tile(n, cap):
    t = cap
    while n % t:
        t //= 2
    return t


def _ada_kernel(c_ref, w_ref, b_ref, o_ref):
    c = c_ref[...]
    a = (c * _sigmoid(c)).astype(BF16)
    o_ref[...] = jnp.dot(a, w_ref[...], preferred_element_type=F32) + b_ref[...]


def _ada(c_all, w, b):
    m, d = c_all.shape
    n = w.shape[1]
    tn = 1536
    return pl.pallas_call(
        _ada_kernel,
        grid=(n // tn,),
        in_specs=[pl.BlockSpec((m, d), lambda j: (0, 0)),
                  pl.BlockSpec((d, tn), lambda j: (0, j)),
                  pl.BlockSpec((1, tn), lambda j: (0, j))],
        out_specs=pl.BlockSpec((m, tn), lambda j: (0, j)),
        out_shape=jax.ShapeDtypeStruct((m, n), F32),
        compiler_params=_cparams("parallel"),
        name="ada",
    )(c_all, w, b)


def _nmm_kernel(x_ref, g_ref, sc_ref, sh_ref, w_ref, o_ref, h_sc):
    @pl.when(pl.program_id(1) == 0)
    def _():
        x = x_ref[...]
        y = x * lax.rsqrt(jnp.mean(x * x, axis=-1, keepdims=True) + NORM_EPS) * g_ref[...]
        h_sc[...] = (y * (1.0 + sc_ref[...]) + sh_ref[...]).astype(BF16)

    o_ref[...] = jnp.dot(h_sc[...], w_ref[...], preferred_element_type=F32)


def _norm_mod_matmul(x, g, sc, sh, w, grp, tn):
    n, d = x.shape
    nout = w.shape[1]
    tm, tpg, r = grp["tm"], grp["tpg"], sc.shape[1]
    return pl.pallas_call(
        _nmm_kernel,
        grid=(n // tm, nout // tn),
        in_specs=[pl.BlockSpec((tm, d), lambda i, j: (i, 0)),
                  pl.BlockSpec((1, d), lambda i, j: (0, 0)),
                  pl.BlockSpec((None, r, d), lambda i, j: (i // tpg, 0, 0)),
                  pl.BlockSpec((None, r, d), lambda i, j: (i // tpg, 0, 0)),
                  pl.BlockSpec((d, tn), lambda i, j: (0, j))],
        out_specs=pl.BlockSpec((tm, tn), lambda i, j: (i, j)),
        out_shape=jax.ShapeDtypeStruct((n, nout), F32),
        scratch_shapes=[pltpu.VMEM((tm, d), BF16)],
        compiler_params=_cparams("parallel", "arbitrary"),
        name="norm_mod_matmul",
    )(x, g, sc, sh, w)


def _log_sigmoid(x):
    return jnp.minimum(x, 0.0) - jnp.log(1.0 + jnp.exp(-jnp.abs(x)))


def _mlstm_kernel(zq_ref, zif_ref, bias_ref, g_ref, c0_ref, n0_ref, m0_ref,
                  act_ref, c_out, n_out, m_out, c_sc, n_sc, m_sc, *, chunk, bb):
    t_ = chunk
    ci = pl.program_id(1)

    @pl.when(ci == 0)
    def _():
        c_sc[...] = c0_ref[...]
        n_sc[...] = n0_ref[...]
        m_sc[...] = m0_ref[...]

    row = lax.broadcasted_iota(jnp.int32, (t_, t_), 0)
    col = lax.broadcasted_iota(jnp.int32, (t_, t_), 1)
    causal = col <= row
    eye = row == col
    row1 = lax.broadcasted_iota(jnp.int32, (t_, 1), 0)
    for b, h in [(b, h) for b in range(bb) for h in range(N_HEADS)]:
        zq_b, act_b = zq_ref.at[b], act_ref.at[b]
        c_sc_b, n_sc_b, m_sc_b = c_sc.at[b], n_sc.at[b], m_sc.at[b]
        gates = zif_ref[b] + bias_ref[...]
        logf = _log_sigmoid(gates)
        lo = h * HEAD_DIM
        q = zq_b[:, lo:lo + HEAD_DIM]
        k = zq_b[:, BRANCH + lo:BRANCH + lo + HEAD_DIM] * (HEAD_DIM ** -0.5)
        v = zq_b[:, 2 * BRANCH + lo:2 * BRANCH + lo + HEAD_DIM]
        o = zq_b[:, 3 * BRANCH + lo:3 * BRANCH + lo + HEAD_DIM]
        li_col = gates[:, h:h + 1]
        lf_col = logf[:, N_HEADS + h:N_HEADS + h + 1]
        li_row = jnp.sum(jnp.where(eye, li_col, 0.0), axis=0, keepdims=True)
        lf_row = jnp.sum(jnp.where(eye, lf_col, 0.0), axis=0, keepdims=True)
        b_col = jnp.sum(jnp.where(causal, lf_row, 0.0), axis=1, keepdims=True)
        b_row = jnp.sum(jnp.where(row <= col, lf_col, 0.0), axis=0, keepdims=True)
        m_prev = m_sc_b[h][:, 0:1]
        d = jnp.where(causal, b_col - b_row + li_row, -jnp.inf)
        inter = b_col + m_prev
        mt = jnp.maximum(jnp.max(d, axis=1, keepdims=True), inter)
        qb, kb, vb = q.astype(BF16), k.astype(BF16), v.astype(BF16)
        w = jnp.exp(d - mt) * _dot_nt(qb, kb)
        a_int = jnp.exp(inter - mt)
        c = c_sc_b[h]
        n = n_sc_b[h]
        num = jnp.dot(w.astype(BF16), vb, preferred_element_type=F32) \
            + a_int * jnp.dot(qb, c.astype(BF16), preferred_element_type=F32)
        den = jnp.sum(w, axis=1, keepdims=True) + a_int * jnp.sum(q * n, axis=1, keepdims=True)
        hh = num / jnp.maximum(jnp.abs(den), jnp.exp(-mt))
        hn = hh * lax.rsqrt(jnp.mean(hh * hh, axis=-1, keepdims=True) + NORM_EPS)
        act_b[:, lo:lo + HEAD_DIM] = hn * g_ref[:, lo:lo + HEAD_DIM] * _sigmoid(o)
        b_end = jnp.sum(jnp.where(row1 == t_ - 1, b_col, 0.0), axis=0, keepdims=True)
        g_col = b_end - b_col + li_col
        m_new = jnp.maximum(b_end + m_prev, jnp.max(g_col, axis=0, keepdims=True))
        wk = jnp.exp(g_col - m_new) * k
        decay = jnp.exp(b_end + m_prev - m_new)
        c_sc_b[h] = decay * c + _dot_tn(wk.astype(BF16), vb)
        n_sc_b[h] = decay * n + jnp.sum(wk, axis=0, keepdims=True)
        m_sc_b[h] = jnp.broadcast_to(m_new, (1, LANES))

    @pl.when(ci == pl.num_programs(1) - 1)
    def _():
        c_out[...] = c_sc[...]
        n_out[...] = n_sc[...]
        m_out[...] = m_sc[...]


def _mlstm(z, bias_if, g, c0, n0, m0, grp):
    b_, l_ = grp["B"], grp["L"]
    t_ = min(MLSTM_CHUNK, l_)
    nc = l_ // t_
    h_ = N_HEADS
    bb = _pick_tile(b_, SCAN_BATCH)
    n0 = n0.reshape(b_, h_, 1, HEAD_DIM)
    m0 = jnp.broadcast_to(m0.reshape(b_, h_, 1, 1), (b_, h_, 1, LANES))
    z3 = z.reshape(b_, l_, Z_WIDTH)
    state_spec = lambda shp: pl.BlockSpec((bb,) + shp, lambda b, c: (b, 0, 0, 0))
    act, c1, n1, m1 = pl.pallas_call(
        functools.partial(_mlstm_kernel, chunk=t_, bb=bb),
        grid=(b_ // bb, nc),
        in_specs=[pl.BlockSpec((bb, t_, 4 * BRANCH), lambda b, c: (b, c, Z_MLSTM // (4 * BRANCH))),
                  pl.BlockSpec((bb, t_, LANES), lambda b, c: (b, c, Z_MIF // LANES)),
                  pl.BlockSpec((1, LANES), lambda b, c: (0, 0)),
                  pl.BlockSpec((1, BRANCH), lambda b, c: (0, 0)),
                  state_spec((h_, HEAD_DIM, HEAD_DIM)),
                  state_spec((h_, 1, HEAD_DIM)),
                  state_spec((h_, 1, LANES))],
        out_specs=[pl.BlockSpec((bb, t_, BRANCH), lambda b, c: (b, c, 0)),
                   state_spec((h_, HEAD_DIM, HEAD_DIM)),
                   state_spec((h_, 1, HEAD_DIM)),
                   state_spec((h_, 1, LANES))],
        out_shape=[jax.ShapeDtypeStruct((b_, l_, BRANCH), F32),
                   jax.ShapeDtypeStruct((b_, h_, HEAD_DIM, HEAD_DIM), F32),
                   jax.ShapeDtypeStruct((b_, h_, 1, HEAD_DIM), F32),
                   jax.ShapeDtypeStruct((b_, h_, 1, LANES), F32)],
        scratch_shapes=[pltpu.VMEM((bb, h_, HEAD_DIM, HEAD_DIM), F32),
                        pltpu.VMEM((bb, h_, 1, HEAD_DIM), F32),
                        pltpu.VMEM((bb, h_, 1, LANES), F32)],
        compiler_params=_cparams("parallel", "arbitrary"),
        name="mlstm",
    )(z3, z3, bias_if, g, c0, n0, m0)
    return act.reshape(b_ * l_, BRANCH), c1, n1.reshape(b_, h_, HEAD_DIM), m1[:, :, 0, 0]


def _conv3(p, prev2, prev1, w_ref):
    row = lax.broadcasted_iota(jnp.int32, p.shape, 0)
    p1 = jnp.where(row == 0, prev1, pltpu.roll(p, 1, axis=0))
    p2 = jnp.where(row == 0, prev2, jnp.where(row == 1, prev1, pltpu.roll(p, 2, axis=0)))
    return w_ref[0:1, :] * p2 + w_ref[1:2, :] * p1 + w_ref[2:3, :] * p


def _sconv_kernel(sb_ref, sc_ref, sh_ref, w_ref, buf_ref, out_ref, st_ref, carry, *, tm):
    @pl.when(pl.program_id(1) == 0)
    def _():
        carry[...] = buf_ref[0]

    p = sc_ref[...] * sh_ref[...]
    u = _conv3(p, carry[0:1, :], carry[1:2, :], w_ref)
    out_ref[...] = sb_ref[...] * u
    new = sc_ref[tm - 2:tm, :] * sh_ref[tm - 2:tm, :]
    carry[...] = new
    st_ref[0] = new


def _sconv(z, w, buf, grp):
    b_, l_ = grp["B"], grp["L"]
    tm = _pick_tile(l_, 512)
    nt = l_ // tm
    cb = Z_SCONV // BRANCH
    zspec = lambda j: pl.BlockSpec((tm, BRANCH), lambda b, i: (b * nt + i, cb + j))
    return pl.pallas_call(
        functools.partial(_sconv_kernel, tm=tm),
        grid=(b_, nt),
        in_specs=[zspec(0), zspec(1), zspec(2),
                  pl.BlockSpec((CONV_WIDTH, BRANCH), lambda b, i: (0, 0)),
                  pl.BlockSpec((1, CONV_WIDTH - 1, BRANCH), lambda b, i: (b, 0, 0))],
        out_specs=[pl.BlockSpec((tm, BRANCH), lambda b, i: (b * nt + i, 0)),
                   pl.BlockSpec((1, CONV_WIDTH - 1, BRANCH), lambda b, i: (b, 0, 0))],
        out_shape=[jax.ShapeDtypeStruct((b_ * l_, BRANCH), F32),
                   jax.ShapeDtypeStruct((b_, CONV_WIDTH - 1, BRANCH), F32)],
        scratch_shapes=[pltpu.VMEM((CONV_WIDTH - 1, BRANCH), F32)],
        compiler_params=_cparams("parallel", "arbitrary"),
        name="sconv",
    )(z, z, z, w, buf)


def _qk_prep_kernel(q_ref, k_ref, qg_ref, kg_ref, cos_ref, sa_ref, sb_ref, qo_ref, ko_ref):
    bd = _head_blockdiag(BRANCH)
    cos, sa, sb = cos_ref[...], sa_ref[...], sb_ref[...]
    half = ROPE_DIMS // 2

    def prep(x, g):
        y = x * lax.rsqrt(_head_sum(x * x, bd) * (1.0 / HEAD_DIM) + NORM_EPS) * g
        return y * cos + pltpu.roll(y, BRANCH - half, axis=1) * sa + pltpu.roll(y, half, axis=1) * sb

    qo_ref[...] = prep(q_ref[...], qg_ref[...])
    ko_ref[...] = prep(k_ref[...], kg_ref[...])


def _rope_tables(pos):
    half = ROPE_DIMS // 2
    inv = jnp.exp(-math.log(ROPE_THETA) * jnp.arange(0, ROPE_DIMS, 2, dtype=F32) / ROPE_DIMS)
    ang = pos.astype(F32)[:, None] * inv[None, :]
    cos, sin = jnp.cos(ang), jnp.sin(ang)
    n = pos.shape[0]
    rest = HEAD_DIM - ROPE_DIMS
    c_h = jnp.concatenate([cos, cos, jnp.ones((n, rest), F32)], axis=1)
    sa_h = jnp.concatenate([-sin, jnp.zeros((n, half + rest), F32)], axis=1)
    sb_h = jnp.concatenate([jnp.zeros((n, half), F32), sin, jnp.zeros((n, rest), F32)], axis=1)
    tile = lambda t: jnp.tile(t, (1, N_HEADS))
    return tile(c_h), tile(sa_h), tile(sb_h)


def _qk_prep(z, qg, kg, tables, grp):
    n = grp["B"] * grp["L"]
    ltab = tables[0].shape[0]
    tm = _pick_tile(ltab, 512)
    npos = ltab // tm
    cb = Z_MOBA // BRANCH
    tspec = pl.BlockSpec((tm, BRANCH), lambda i: (i % npos, 0))
    gspec = pl.BlockSpec((1, BRANCH), lambda i: (0, 0))
    ospec = pl.BlockSpec((tm, BRANCH), lambda i: (i, 0))
    return pl.pallas_call(
        _qk_prep_kernel,
        grid=(n // tm,),
        in_specs=[pl.BlockSpec((tm, BRANCH), lambda i: (i, cb)),
                  pl.BlockSpec((tm, BRANCH), lambda i: (i, cb + 1)),
                  gspec, gspec, tspec, tspec, tspec],
        out_specs=[ospec, ospec],
        out_shape=[jax.ShapeDtypeStruct((n, BRANCH), F32)] * 2,
        compiler_params=_cparams("parallel"),
        name="qk_prep",
    )(z, z, qg, kg, *tables)


def _moba_prompt_kernel(q_ref, k_ref, v_ref, o_ref, kmean, *, nb):
    qi = pl.program_id(1)
    tq = MB_BLOCK

    @pl.when(qi == 0)
    def _():
        kmean[...] = jnp.zeros_like(kmean)
        for n in range(nb):
            kmean[n:n + 1, :] = jnp.mean(k_ref[n * MB_BLOCK:(n + 1) * MB_BLOCK, :], axis=0, keepdims=True)

    lane = lax.broadcasted_iota(jnp.int32, (tq, LANES), 1)
    row = lax.broadcasted_iota(jnp.int32, (tq, tq), 0)
    col = lax.broadcasted_iota(jnp.int32, (tq, tq), 1)
    scale = HEAD_DIM ** -0.5
    own0 = pl.multiple_of(qi * MB_BLOCK, MB_BLOCK)
    for h in range(N_HEADS):
        hs = slice(h * HEAD_DIM, (h + 1) * HEAD_DIM)
        qh = q_ref[:, hs]
        qb = qh.astype(BF16)
        s = lax.dot_general(qh, kmean[:, hs], (((1,), (1,)), ((), ())),
                            precision=HIGHEST, preferred_element_type=F32)
        valid = lane < qi
        s = jnp.where(valid, s, -jnp.inf)
        rank = jnp.zeros((tq, LANES), jnp.int32)
        for m in range(nb):
            sm = s[:, m:m + 1]
            rank += ((sm > s) | ((sm == s) & (m < lane))).astype(jnp.int32)
        sel = (valid & (rank < MB_TOPK)).astype(F32)
        kb = k_ref[pl.ds(own0, MB_BLOCK), hs].astype(BF16)
        vb = v_ref[pl.ds(own0, MB_BLOCK), hs].astype(BF16)
        sc = jnp.where(col <= row, _dot_nt(qb, kb) * scale, -jnp.inf)
        m_i = jnp.max(sc, axis=1, keepdims=True)
        p = jnp.exp(sc - m_i)
        l_i = jnp.sum(p, axis=1, keepdims=True)
        acc = jnp.dot(p.astype(BF16), vb, preferred_element_type=F32)

        def body(n, carry):
            m_i, l_i, acc = carry
            k0 = pl.multiple_of(n * MB_BLOCK, MB_BLOCK)
            kb = k_ref[pl.ds(k0, MB_BLOCK), hs].astype(BF16)
            vb = v_ref[pl.ds(k0, MB_BLOCK), hs].astype(BF16)
            seln = jnp.sum(jnp.where(lane == n, sel, 0.0), axis=1, keepdims=True) > 0.0
            sc = jnp.where(seln, _dot_nt(qb, kb) * scale, -jnp.inf)
            m_new = jnp.maximum(m_i, jnp.max(sc, axis=1, keepdims=True))
            alpha = jnp.exp(m_i - m_new)
            p = jnp.exp(sc - m_new)
            l_new = alpha * l_i + jnp.sum(p, axis=1, keepdims=True)
            acc_new = alpha * acc + jnp.dot(p.astype(BF16), vb, preferred_element_type=F32)
            return m_new, l_new, acc_new

        m_i, l_i, acc = lax.fori_loop(0, qi, body, (m_i, l_i, acc))
        o_ref[:, hs] = acc / l_i


def _moba_prompt(q, k, z, grp):
    b_, l_ = grp["B"], grp["L"]
    assert l_ % MB_BLOCK == 0 and l_ // MB_BLOCK <= LANES
    nb = l_ // MB_BLOCK
    return pl.pallas_call(
        functools.partial(_moba_prompt_kernel, nb=nb),
        grid=(b_, nb),
        in_specs=[pl.BlockSpec((MB_BLOCK, BRANCH), lambda b, i: (b * nb + i, 0)),
                  pl.BlockSpec((l_, BRANCH), lambda b, i: (b, 0)),
                  pl.BlockSpec((l_, BRANCH), lambda b, i: (b, Z_MOBA // BRANCH + 2))],
        out_specs=pl.BlockSpec((MB_BLOCK, BRANCH), lambda b, i: (b * nb + i, 0)),
        out_shape=jax.ShapeDtypeStruct((b_ * l_, BRANCH), F32),
        scratch_shapes=[pltpu.VMEM((LANES, BRANCH), F32)],
        compiler_params=_cparams("parallel", "arbitrary"),
        name="moba_prompt",
    )(q, k, z)


def _moba_sample_kernel(pt_ref, q_ref, kn_ref, vn_ref, *refs, nblk, dec, gsz):
    ppb = MB_BLOCK // PAGE_SIZE
    k_refs, v_refs = refs[:ppb * gsz], refs[ppb * gsz:2 * ppb * gsz]
    o_ref, kmean_s, m_s, l_s, o_s = refs[2 * ppb * gsz:]
    step = pl.program_id(1)
    nq = N_HEADS * dec
    scale = HEAD_DIM ** -0.5
    lane = lax.broadcasted_iota(jnp.int32, (nq, LANES), 1)
    lane_c = lax.broadcasted_iota(jnp.int32, (BRANCH, LANES), 1)

    @pl.when(step == 0)
    def _():
        kmean_s[...] = jnp.zeros_like(kmean_s)
        m_s[...] = jnp.full_like(m_s, -jnp.inf)
        l_s[...] = jnp.zeros_like(l_s)

    row_head = jnp.concatenate([jnp.full((dec, BRANCH), h, jnp.int32) for h in range(N_HEADS)], axis=0)
    lane_head = lax.broadcasted_iota(jnp.int32, (nq, BRANCH), 1) // HEAD_DIM
    qbd = jnp.where(row_head == lane_head, jnp.concatenate([q_ref[...]] * N_HEADS, axis=0), 0.0)
    qbd_b = qbd.astype(BF16)
    kmean, m_all, l_all = kmean_s[...], m_s[...], l_s[...]
    for g in range(gsz):
        blk = step * gsz + g
        kt = jnp.concatenate([k_refs[ppb * g + j][...] for j in range(ppb)], axis=1)
        vt = jnp.concatenate([v_refs[ppb * g + j][...] for j in range(ppb)], axis=1)
        s = jnp.dot(qbd_b, kt.astype(BF16), preferred_element_type=F32) * scale
        mx = jnp.max(s, axis=1, keepdims=True)
        p = jnp.exp(s - mx)
        o_s[blk] = _dot_nt(p.astype(BF16), vt.astype(BF16))
        kmean = jnp.where(lane_c == blk, jnp.mean(kt, axis=1, keepdims=True), kmean)
        m_all = jnp.where(lane == blk, mx, m_all)
        l_all = jnp.where(lane == blk, jnp.sum(p, axis=1, keepdims=True), l_all)
    kmean_s[...] = kmean
    m_s[...] = m_all
    l_s[...] = l_all

    @pl.when(step == pl.num_programs(1) - 1)
    def _():
        zpad = jnp.zeros((LANES - dec, BRANCH), F32)
        knew = jnp.concatenate([kn_ref[...], zpad], axis=0)
        vnew = jnp.concatenate([vn_ref[...], zpad], axis=0)
        rowq = jnp.concatenate([lax.broadcasted_iota(jnp.int32, (dec, LANES), 0)] * N_HEADS, axis=0)
        lane_f = lane.astype(F32)
        s = jnp.dot(qbd, kmean_s[...], precision=HIGHEST, preferred_element_type=F32)
        s = jnp.where(lane < nblk, s, -jnp.inf)
        sel = jnp.zeros((nq, LANES), jnp.bool_)
        for _ in range(MB_TOPK):
            mxv = jnp.max(s, axis=1, keepdims=True)
            idx = jnp.min(jnp.where(s == mxv, lane_f, 2.0 * LANES), axis=1, keepdims=True)
            pick = lane_f == idx
            sel = sel | pick
            s = jnp.where(pick, -jnp.inf, s)
        sel = sel & (lane < nblk)
        s_own = jnp.where(lane <= rowq, _dot_nt(qbd_b, knew.astype(BF16)) * scale, -jnp.inf)
        m_own = jnp.max(s_own, axis=1, keepdims=True)
        p_own = jnp.exp(s_own - m_own)
        l_own = jnp.sum(p_own, axis=1, keepdims=True)
        o_own = jnp.dot(p_own.astype(BF16), vnew.astype(BF16), preferred_element_type=F32)
        mb = m_s[...]
        mtot = jnp.maximum(jnp.max(jnp.where(sel, mb, -jnp.inf), axis=1, keepdims=True), m_own)
        wgt = jnp.where(sel, jnp.exp(mb - mtot), 0.0)
        a_own = jnp.exp(m_own - mtot)
        ltot = jnp.sum(wgt * l_s[...], axis=1, keepdims=True) + l_own * a_own
        acc = o_own * a_own
        for j in range(nblk):
            acc = acc + wgt[:, j:j + 1] * o_s[j]
        res = acc / ltot
        for h in range(N_HEADS):
            hs = slice(h * HEAD_DIM, (h + 1) * HEAD_DIM)
            o_ref[:, hs] = res[h * dec:(h + 1) * dec, hs]


def _moba_sample(q, k, v, pool_k, pool_v, page_table, layer, grp):
    b_, dec = grp["B"], grp["L"]
    n_pages = page_table.shape[1]
    ppb = MB_BLOCK // PAGE_SIZE
    assert n_pages % ppb == 0
    nblk = n_pages // ppb
    assert MB_TOPK <= nblk <= LANES and dec <= LANES and dec % 8 == 0
    gsz = max(g for g in range(1, 9) if nblk % g == 0)
    as_pages = lambda t: t.transpose(0, 1, 3, 4, 2).reshape(t.shape[0], t.shape[1], BRANCH, PAGE_SIZE)
    pool_k, pool_v = as_pages(pool_k), as_pages(pool_v)
    pt = page_table.reshape(-1)

    def page(j):
        return pl.BlockSpec((None, None, BRANCH, PAGE_SIZE),
                            lambda b, s, pt: (layer, pt[b * n_pages + s * gsz * ppb + j], 0, 0))

    pages = [page(j) for j in range(gsz * ppb)]
    new = pl.BlockSpec((dec, BRANCH), lambda b, s, pt: (b, 0))
    nq = N_HEADS * dec
    return pl.pallas_call(
        functools.partial(_moba_sample_kernel, nblk=nblk, dec=dec, gsz=gsz),
        grid_spec=pltpu.PrefetchScalarGridSpec(
            num_scalar_prefetch=1,
            grid=(b_, nblk // gsz),
            in_specs=[new, new, new] + pages + pages,
            out_specs=new,
            scratch_shapes=[pltpu.VMEM((BRANCH, LANES), F32),
                            pltpu.VMEM((nq, LANES), F32),
                            pltpu.VMEM((nq, LANES), F32),
                            pltpu.VMEM((nblk, nq, BRANCH), F32)]),
        out_shape=jax.ShapeDtypeStruct((b_ * dec, BRANCH), F32),
        compiler_params=_cparams("parallel", "arbitrary"),
        name="moba_sample",
    )(pt, q, k, v, *([pool_k] * (gsz * ppb)), *([pool_v] * (gsz * ppb)))


def _rwkv_prep_kernel(z_ref, prev_ref, mu_ref, wlr_ref, w0_ref, a0_ref, kks_ref, kas_ref, rk_ref,
                      r_o, w_o, k_o, v_o, kk_o, kka_o, bonus_o, g_o, carry, *, tm):
    @pl.when(pl.program_id(1) == 0)
    def _():
        carry[...] = prev_ref[0]

    z = z_ref[...]
    row = lax.broadcasted_iota(jnp.int32, z.shape, 0)
    zs = jnp.where(row == 0, carry[...], pltpu.roll(z, 1, axis=0))
    carry[...] = z_ref[tm - 1:tm, :]
    zz = z + mu_ref[...] * (zs - z)
    r = zz[:, 0:BRANCH]
    k = zz[:, BRANCH:2 * BRANCH]
    v = zz[:, 2 * BRANCH:3 * BRANCH]
    lr = zz[:, 3 * BRANCH:3 * BRANCH + RW_LORA]
    lane = lax.broadcasted_iota(jnp.int32, lr.shape, 1)
    lr_in = jnp.where(lane < 32, jnp.tanh(lr), jnp.where(lane < 64, lr, _sigmoid(lr)))
    lo = jnp.dot(lr_in.astype(BF16), wlr_ref[...], preferred_element_type=F32)
    log_decay = -math.exp(-0.5) * _sigmoid(w0_ref[...] + lo[:, 0:BRANCH])
    a = _sigmoid(a0_ref[...] + lo[:, BRANCH:2 * BRANCH])
    bd = _head_blockdiag(BRANCH)
    kk = k * kks_ref[...]
    kk = kk * lax.rsqrt(jnp.maximum(_head_sum(kk * kk, bd), 1e-24))
    k2 = k * (1.0 + (a - 1.0) * kas_ref[...])
    r_o[...] = r
    w_o[...] = log_decay
    k_o[...] = k2
    v_o[...] = v
    kk_o[...] = kk
    kka_o[...] = kk * a
    bonus_o[...] = _head_sum(r * k2 * rk_ref[...], bd) * v
    g_o[...] = lo[:, 2 * BRANCH:3 * BRANCH]


def _rwkv_prep(z, prev, p, grp):
    b_, l_ = grp["B"], grp["L"]
    tm = _pick_tile(l_, 256)
    nt = l_ // tm
    zw = Z_WIDTH - Z_RWKV
    vec = pl.BlockSpec((1, BRANCH), lambda b, i: (0, 0))
    ospec = pl.BlockSpec((tm, BRANCH), lambda b, i: (b * nt + i, 0))
    return pl.pallas_call(
        functools.partial(_rwkv_prep_kernel, tm=tm),
        grid=(b_, nt),
        in_specs=[pl.BlockSpec((tm, zw), lambda b, i: (b * nt + i, Z_RWKV // zw)),
                  pl.BlockSpec((1, 1, zw), lambda b, i: (b, 0, 0)),
                  pl.BlockSpec((1, zw), lambda b, i: (0, 0)),
                  pl.BlockSpec((RW_LORA, 3 * BRANCH), lambda b, i: (0, 0)),
                  vec, vec, vec, vec, vec],
        out_specs=[ospec] * 8,
        out_shape=[jax.ShapeDtypeStruct((b_ * l_, BRANCH), F32)] * 8,
        scratch_shapes=[pltpu.VMEM((1, zw), F32)],
        compiler_params=_cparams("parallel", "arbitrary"),
        name="rwkv_prep",
    )(z, prev, p["mu"], p["wlr"], p["w0"], p["a0"], p["kk"], p["ka"], p["rk"])


def _rwkv_chunk_kernel(r_ref, lw_ref, k_ref, v_ref, kk_ref, kka_ref, h0_ref, y_ref, h_out, h_sc, *, chunk, bb):
    t_ = chunk
    ci = pl.program_id(1)

    @pl.when(ci == 0)
    def _():
        h_sc[...] = h0_ref[...]

    t4 = N_HEADS * t_
    rows = list(range(bb))
    each = lambda f, *xs: [f(*a) for a in zip(*xs)]
    tri = (lax.broadcasted_iota(jnp.int32, (t_, t_), 0) >= lax.broadcasted_iota(jnp.int32, (t_, t_), 1)).astype(F32)
    row1 = lax.broadcasted_iota(jnp.int32, (t_, 1), 0)
    head_mask = (jnp.concatenate([jnp.full((t_, BRANCH), h, jnp.int32) for h in range(N_HEADS)], axis=0)
                 == lax.broadcasted_iota(jnp.int32, (t4, BRANCH), 1) // HEAD_DIM)
    t_row = jnp.concatenate([lax.broadcasted_iota(jnp.int32, (t_, t4), 0)] * N_HEADS, axis=0)
    t_col = lax.broadcasted_iota(jnp.int32, (t4, t4), 1) & (t_ - 1)
    strict = t_row > t_col
    lower = t_row >= t_col
    stack = lambda x: jnp.where(head_mask, jnp.concatenate([x] * N_HEADS, axis=0), 0.0).astype(BF16)
    mm = lambda a, b: jnp.dot(a, b, preferred_element_type=F32)

    lw = [lw_ref[b] for b in rows]
    cum = each(lambda x: jnp.dot(tri, x, precision=HIGHEST, preferred_element_type=F32), lw)
    p_in = each(jnp.exp, cum)
    p_inv = each(lambda c: jnp.exp(-c), cum)
    p_end = each(lambda p: jnp.sum(jnp.where(row1 == t_ - 1, p, 0.0), axis=0, keepdims=True), p_in)
    kkm = [stack(kk_ref[b] * jnp.exp(cum[b] - lw[b])) for b in rows]
    rp = [stack(r_ref[b] * p_in[b]) for b in rows]
    kh = [k_ref[b] * p_inv[b] for b in rows]
    ah = [kka_ref[b] * p_inv[b] for b in rows]
    khe = [stack(kh[b] * p_end[b]) for b in rows]
    ahe = [stack(ah[b] * p_end[b]) for b in rows]
    khm, ahm = each(stack, kh), each(stack, ah)
    vm = [stack(v_ref[b]) for b in rows]
    lr = each(lambda a, b: jnp.concatenate([a, b], axis=0), kkm, rp)
    gk = each(_dot_nt, lr, khm)
    ga = each(_dot_nt, lr, ahm)
    ab_k = each(lambda g: jnp.concatenate([jnp.where(strict, g[:t4], 0.0), jnp.where(lower, g[t4:], 0.0)],
                                          axis=0).astype(BF16), gk)
    b_a = each(lambda g: jnp.where(lower, g[t4:], 0.0).astype(BF16), ga)
    e = each(lambda g: jnp.where(strict, -g[:t4], 0.0), ga)
    pw = e
    span = 2
    while span < t_:
        pw = each(lambda p: mm(p.astype(BF16), p.astype(BF16)), pw)
        e = each(lambda x, p: x + p + mm(x.astype(BF16), p.astype(BF16)), e, pw)
        span *= 2
    ht = [h_sc[b] for b in rows]
    x0 = each(lambda a, h: _dot_nt(a, h.astype(BF16)), lr, ht)
    kv = each(mm, ab_k, vm)
    x1 = each(lambda a, b: a[:t4] + b[:t4], x0, kv)
    u = each(lambda x, m: x + mm(m.astype(BF16), x.astype(BF16)), x1, e)
    ub = each(lambda x: x.astype(BF16), u)
    ybd = each(lambda a, b, m, x: a[t4:] + b[t4:] - mm(m, x), x0, kv, b_a, ub)
    for b in rows:
        y = ybd[b][0:t_]
        for h in range(1, N_HEADS):
            y = y + ybd[b][h * t_:(h + 1) * t_]
        y_ref[b] = y
        h_sc[b] = ht[b] * p_end[b] + _dot_tn(vm[b], khe[b]) - _dot_tn(ub[b], ahe[b])

    @pl.when(ci == pl.num_programs(1) - 1)
    def _():
        h_out[...] = h_sc[...]


def _rwkv_scan(r, lw, k, v, kk, kka, s0, grp):
    b_, l_ = grp["B"], grp["L"]
    chunk = _pick_tile(l_, RWKV_CHUNK)
    nc = l_ // chunk
    bb = _pick_tile(b_, RWKV_BATCH)
    seq = pl.BlockSpec((bb, chunk, BRANCH), lambda b, c: (b, c, 0))
    st = pl.BlockSpec((bb, BRANCH, BRANCH), lambda b, c: (b, 0, 0))
    as3 = lambda t: t.reshape(b_, l_, BRANCH)
    blocks = [slice(h * HEAD_DIM, (h + 1) * HEAD_DIM) for h in range(N_HEADS)]
    s0_bd = jnp.zeros((b_, BRANCH, BRANCH), F32)
    for h, hs in enumerate(blocks):
        s0_bd = s0_bd.at[:, hs, hs].set(s0[:, h])
    y, s1 = pl.pallas_call(
        functools.partial(_rwkv_chunk_kernel, chunk=chunk, bb=bb),
        grid=(b_ // bb, nc),
        in_specs=[seq] * 6 + [st],
        out_specs=[seq, st],
        out_shape=[jax.ShapeDtypeStruct((b_, l_, BRANCH), F32),
                   jax.ShapeDtypeStruct((b_, BRANCH, BRANCH), F32)],
        scratch_shapes=[pltpu.VMEM((bb, BRANCH, BRANCH), F32)],
        compiler_params=_cparams("parallel", "arbitrary"),
        name="rwkv_scan",
    )(as3(r), as3(lw), as3(k), as3(v), as3(kk), as3(kka), s0_bd)
    return y.reshape(b_ * l_, BRANCH), jnp.stack([s1[:, hs, hs] for hs in blocks], axis=1)


def _merge_kernel(ml_ref, sc_ref, mb_ref, y_ref, bonus_ref, g_ref, gn_ref, gates_ref, x_ref, gt_ref,
                  mp_ref, sp_ref, ap_ref, rp_ref, wo_ref, o_ref):
    bd = _head_blockdiag(BRANCH)
    y = y_ref[...]
    yc = y - _head_sum(y, bd) * (1.0 / HEAD_DIM)
    yn = yc * lax.rsqrt(_head_sum(yc * yc, bd) * (1.0 / HEAD_DIM) + GN_EPS)
    rw = (yn * gn_ref[...] + bonus_ref[...]) * g_ref[...]
    acts = (ml_ref[...], sc_ref[...], mb_ref[...], rw)
    projs = (mp_ref, sp_ref, ap_ref, rp_ref)
    merged = None
    for j in range(4):
        br = jnp.dot(acts[j].astype(BF16), projs[j][...], preferred_element_type=F32)
        term = _sigmoid(gates_ref[:, j * D_MODEL:(j + 1) * D_MODEL]) * br
        merged = term if merged is None else merged + term
    o_ref[...] = x_ref[...] + gt_ref[...] * jnp.dot(merged.astype(BF16), wo_ref[...], preferred_element_type=F32)


def _merge(ml, sc, mb, y, bonus, g, gn, z, x, gt, projs, w_out, grp):
    n, d = x.shape
    tm = min(256, grp["tm"])
    tpg = grp["tpg"] * (grp["tm"] // tm)
    r = gt.shape[1]
    if r > 1:
        assert r == grp["tm"] and tpg == grp["tm"] // tm
        gt = gt.reshape(r // tm, tm, d)
        gt_spec = pl.BlockSpec((None, tm, d), lambda i: (i, 0, 0))
    else:
        gt_spec = pl.BlockSpec((None, 1, d), lambda i: (i // tpg, 0, 0))
    act = pl.BlockSpec((tm, BRANCH), lambda i: (i, 0))
    proj = pl.BlockSpec((BRANCH, d), lambda i: (0, 0))
    return pl.pallas_call(
        _merge_kernel,
        grid=(n // tm,),
        in_specs=[act] * 6 + [pl.BlockSpec((1, BRANCH), lambda i: (0, 0)),
                              pl.BlockSpec((tm, 4 * d), lambda i: (i, 0)),
                              pl.BlockSpec((tm, d), lambda i: (i, 0)),
                              gt_spec, proj, proj, proj, proj,
                              pl.BlockSpec((d, d), lambda i: (0, 0))],
        out_specs=pl.BlockSpec((tm, d), lambda i: (i, 0)),
        out_shape=jax.ShapeDtypeStruct((n, d), F32),
        compiler_params=_cparams("parallel"),
        name="merge",
    )(ml, sc, mb, y, bonus, g, gn, z, x, gt, *projs, w_out)


def _ffn_act_kernel(ua_ref, ul_ref, w_ref, buf_ref, o_ref, carry, *, tm):
    @pl.when(pl.program_id(1) == 0)
    def _():
        carry[...] = buf_ref[0]

    ua = ua_ref[...]
    u = _conv3(ua, carry[0:1, :], carry[1:2, :], w_ref)
    carry[...] = ua_ref[tm - 2:tm, :]
    o_ref[...] = (u * _sigmoid(u) * ul_ref[...]).astype(BF16)


def _ffn_act(up, w, buf, grp):
    b_, l_ = grp["B"], grp["L"]
    tm = _pick_tile(l_, 256)
    nt = l_ // tm
    half = lambda j: pl.BlockSpec((tm, D_FF), lambda b, i: (b * nt + i, j))
    return pl.pallas_call(
        functools.partial(_ffn_act_kernel, tm=tm),
        grid=(b_, nt),
        in_specs=[half(0), half(1),
                  pl.BlockSpec((CONV_WIDTH, D_FF), lambda b, i: (0, 0)),
                  pl.BlockSpec((1, CONV_WIDTH - 1, D_FF), lambda b, i: (b, 0, 0))],
        out_specs=pl.BlockSpec((tm, D_FF), lambda b, i: (b * nt + i, 0)),
        out_shape=jax.ShapeDtypeStruct((b_ * l_, D_FF), BF16),
        scratch_shapes=[pltpu.VMEM((CONV_WIDTH - 1, D_FF), F32)],
        compiler_params=_cparams("parallel", "arbitrary"),
        name="ffn_act",
    )(up, up, w, buf)


def _mm_res_kernel(a_ref, w_ref, x_ref, gt_ref, o_ref):
    o_ref[...] = x_ref[...] + gt_ref[...] * jnp.dot(a_ref[...], w_ref[...], preferred_element_type=F32)


def _mm_res(a, w, x, gt, grp):
    n, d = x.shape
    kdim = a.shape[1]
    tm, tpg, r = min(512, grp["tm"]), grp["tpg"] * (grp["tm"] // min(512, grp["tm"])), gt.shape[1]
    if r > 1:
        gt = gt.reshape(r // tm, tm, d)
        gt_spec = pl.BlockSpec((None, tm, d), lambda i: (i, 0, 0))
    else:
        gt_spec = pl.BlockSpec((None, 1, d), lambda i: (i // tpg, 0, 0))
    return pl.pallas_call(
        _mm_res_kernel,
        grid=(n // tm,),
        in_specs=[pl.BlockSpec((tm, kdim), lambda i: (i, 0)),
                  pl.BlockSpec((kdim, d), lambda i: (0, 0)),
                  pl.BlockSpec((tm, d), lambda i: (i, 0)),
                  gt_spec],
        out_specs=pl.BlockSpec((tm, d), lambda i: (i, 0)),
        out_shape=jax.ShapeDtypeStruct((n, d), F32),
        compiler_params=_cparams("parallel"),
        name="mm_res",
    )(a, w, x, gt)


def _prep_layer(l, P):
    d = D_MODEL
    w_in = P["w_in"][l]
    o_mif = 4 * BRANCH
    o_sb = o_mif + 2 * N_HEADS
    o_aq = o_sb + 3 * BRANCH
    o_rw = o_aq + 3 * BRANCH
    o_g = o_rw + RW_COLS
    w_in_p = jnp.concatenate([
        w_in[:, o_g:o_g + 4 * d], w_in[:, 0:o_mif], w_in[:, o_sb:o_aq], w_in[:, o_aq:o_rw],
        w_in[:, o_mif:o_sb], jnp.zeros((d, Z_RWKV - Z_MIF - 2 * N_HEADS), F32),
        w_in[:, o_rw:o_g], jnp.zeros((d, Z_WIDTH - Z_RWKV - RW_COLS), F32)], axis=1).astype(BF16)
    bias_if = jnp.concatenate([P["m_bi"][l], P["m_bf"][l], jnp.zeros((LANES - 2 * N_HEADS,), F32)])[None, :]
    zw = Z_WIDTH - Z_RWKV
    wlr = jnp.zeros((RW_LORA, 3 * BRANCH), F32)
    wlr = wlr.at[0:32, 0:BRANCH].set(P["r_wB"][l])
    wlr = wlr.at[32:64, BRANCH:2 * BRANCH].set(P["r_aB"][l])
    wlr = wlr.at[64:128, 2 * BRANCH:].set(P["r_gB"][l])
    row = lambda t: t[None, :]
    rw = dict(mu=jnp.pad(P["r_mu"][l], (0, zw - RW_COLS))[None, :], wlr=wlr.astype(BF16),
              w0=row(P["r_w0"][l]), a0=row(P["r_a0"][l]), kk=row(P["r_kk"][l]), ka=row(P["r_ka"][l]),
              rk=row(P["r_rk"][l]))
    bf = lambda name: P[name][l].astype(BF16)
    return dict(
        w_ada=bf("w_ada"), b_ada=row(P["b_ada"][l]), w_in=w_in_p, bias_if=bias_if,
        norm_mix_g=row(P["norm_mix_g"][l]), norm_ffn_g=row(P["norm_ffn_g"][l]),
        m_norm_g=row(P["m_norm_g"][l]), s_conv=P["s_conv"][l],
        qg=row(jnp.tile(P["a_qnorm"][l], N_HEADS)), kg=row(jnp.tile(P["a_knorm"][l], N_HEADS)),
        rw=rw, r_norm_g=row(P["r_norm_g"][l]),
        projs=(bf("m_proj"), bf("s_proj"), bf("a_proj"), bf("r_proj")), w_out=bf("w_out"),
        f_up=bf("f_up"), f_conv=P["f_conv"][l], f_down=bf("f_down"))


def _layer(x, mod, W, state, tables, attend, grp):
    mc, mn, mm, sbuf, rs, rshift, fbuf = state
    b_, l_ = grp["B"], grp["L"]
    sh_m, sc_m, gt_m, sh_f, sc_f, gt_f = mod
    z = _norm_mod_matmul(x, W["norm_mix_g"], sc_m, sh_m, W["w_in"], grp, 1024)
    ml, mc, mn, mm = _mlstm(z, W["bias_if"], W["m_norm_g"], mc, mn, mm, grp)
    sc, sbuf = _sconv(z, W["s_conv"], sbuf, grp)
    q, k = _qk_prep(z, W["qg"], W["kg"], tables, grp)
    v = z[:, Z_MOBA + 2 * BRANCH:Z_MOBA + 3 * BRANCH]
    mb = attend(q, k, z, v)
    zw = Z_WIDTH - Z_RWKV
    prev = jnp.pad(rshift, ((0, 0), (0, zw - RW_COLS)))[:, None, :]
    r, w, k2, vv, kk, kka, bonus, g = _rwkv_prep(z, prev, W["rw"], grp)
    y, rs_t = _rwkv_scan(r, w, k2, vv, kk, kka, rs, grp)
    rshift = z.reshape(b_, l_, Z_WIDTH)[:, -1, Z_RWKV:Z_RWKV + RW_COLS]
    x = _merge(ml, sc, mb, y, bonus, g, W["r_norm_g"], z, x, gt_m, W["projs"], W["w_out"], grp)
    up = _norm_mod_matmul(x, W["norm_ffn_g"], sc_f, sh_f, W["f_up"], grp, 1408)
    act = _ffn_act(up, W["f_conv"], fbuf, grp)
    fbuf = up.reshape(b_, l_, 2 * D_FF)[:, l_ - (CONV_WIDTH - 1):, :D_FF]
    x = _mm_res(act, W["f_down"], x, gt_f, grp)
    return x, (k, v), (mc, mn, mm, sbuf, rs_t, rshift, fbuf)


def kernel(x_prompt, x_sample, c_prompt, c_sample, cache_k, cache_v, page_table, state_mlstm_c, state_mlstm_n, state_mlstm_m, state_conv, state_rwkv, state_rwkv_shift, state_ffn_conv, norm_mix_g, norm_ffn_g, w_ada, b_ada, w_in, m_bi, m_bf, m_norm_g, m_proj, s_conv, s_proj, a_qnorm, a_knorm, a_proj, r_mu, r_w0, r_wB, r_a0, r_aB, r_gB, r_kk, r_ka, r_rk, r_norm_g, r_proj, w_out, f_up, f_conv, f_down):
    P = dict(norm_mix_g=norm_mix_g, norm_ffn_g=norm_ffn_g, w_ada=w_ada, b_ada=b_ada, w_in=w_in,
             m_bi=m_bi, m_bf=m_bf, m_norm_g=m_norm_g, m_proj=m_proj, s_conv=s_conv, s_proj=s_proj,
             a_qnorm=a_qnorm, a_knorm=a_knorm, a_proj=a_proj, r_mu=r_mu, r_w0=r_w0, r_wB=r_wB,
             r_a0=r_a0, r_aB=r_aB, r_gB=r_gB, r_kk=r_kk, r_ka=r_ka, r_rk=r_rk, r_norm_g=r_norm_g,
             r_proj=r_proj, w_out=w_out, f_up=f_up, f_conv=f_conv, f_down=f_down)
    depth = w_in.shape[0]
    bp, lp, d = x_prompt.shape
    bs, ls, _ = x_sample.shape
    n_s = bs * ls
    past = page_table.shape[1] * PAGE_SIZE
    tm_p = _pick_tile(lp, 1024)
    grp_p = dict(B=bp, L=lp, tm=tm_p, tpg=lp // tm_p)
    grp_s = dict(B=bs, L=ls, tm=n_s, tpg=1)
    assert n_s % 8 == 0 and n_s <= 1024

    zeros = lambda *s: jnp.zeros(s, F32)
    st_p = (zeros(bp, N_HEADS, HEAD_DIM, HEAD_DIM), zeros(bp, N_HEADS, HEAD_DIM), zeros(bp, N_HEADS),
            zeros(bp, CONV_WIDTH - 1, BRANCH), zeros(bp, N_HEADS, HEAD_DIM, HEAD_DIM), zeros(bp, RW_COLS),
            zeros(bp, CONV_WIDTH - 1, D_FF))
    tab_p = _rope_tables(jnp.arange(lp, dtype=jnp.int32))
    tab_s = tuple(jnp.tile(t, (bs, 1)) for t in _rope_tables(past + jnp.arange(ls, dtype=jnp.int32)))

    c_all = jnp.concatenate([c_prompt, c_sample], axis=0)
    hp = x_prompt.reshape(bp * lp, d)
    hs = x_sample.reshape(n_s, d)
    kv_p, kv_s, sts_p, sts_s = [], [], [], []
    for l in range(depth):
        W = _prep_layer(l, P)
        mod = _ada(c_all, W["w_ada"], W["b_ada"])
        mods = [mod[:, j * d:(j + 1) * d] for j in range(6)]
        mod_p = [m[:bp][:, None, :] for m in mods]
        mod_s = [jnp.repeat(m[bp:], ls, axis=0)[None] for m in mods]

        attend_p = lambda q, k, z, v: _moba_prompt(q, k, z, grp_p)
        hp, kv, st = _layer(hp, mod_p, W, st_p, tab_p, attend_p, grp_p)
        kv_p.append(kv)
        sts_p.append(st)

        prev = (state_mlstm_c[l], state_mlstm_n[l], state_mlstm_m[l], state_conv[l], state_rwkv[l],
                state_rwkv_shift[l], state_ffn_conv[l])
        attend_s = lambda q, k, z, v: _moba_sample(q, k, v, cache_k, cache_v, page_table, l, grp_s)
        hs, kv, st = _layer(hs, mod_s, W, prev, tab_s, attend_s, grp_s)
        kv_s.append(kv)
        sts_s.append(st)

    heads = lambda t, b_, l_: t.reshape(b_, l_, N_HEADS, HEAD_DIM)
    k_prompt = jnp.stack([heads(k, bp, lp) for k, _ in kv_p])
    v_prompt = jnp.stack([heads(v, bp, lp) for _, v in kv_p])
    k_sample = jnp.stack([heads(k, bs, ls) for k, _ in kv_s])
    v_sample = jnp.stack([heads(v, bs, ls) for _, v in kv_s])
    stack = lambda sts: [jnp.stack(t) for t in zip(*sts)]
    return (hp.reshape(bp, lp, d), hs.reshape(bs, ls, d), k_prompt, v_prompt, k_sample, v_sample,
            *stack(sts_p), *stack(sts_s))
```

```python
import functools
import math

import jax
import jax.numpy as jnp
from jax import lax
from jax.experimental import pallas as pl
from jax.experimental.pallas import tpu as pltpu

F32 = jnp.float32
BF16 = jnp.bfloat16
HIGHEST = lax.Precision.HIGHEST

D_MODEL = 1024
HEAD_DIM = 64
N_HEADS = 4
BRANCH = N_HEADS * HEAD_DIM
CONV_WIDTH = 3
MB_BLOCK = 256
MB_TOPK = 3
PAGE_SIZE = 128
ROPE_DIMS = HEAD_DIM // 4
ROPE_THETA = 500000.0
RW_LORA = 128
RW_COLS = 3 * BRANCH + RW_LORA
D_FF = 11 * D_MODEL // 4
MLSTM_CHUNK = 64
RWKV_CHUNK = 64
SCAN_BATCH = 1
RWKV_BATCH = 2
NORM_EPS = 1e-6
GN_EPS = 64e-5
LANES = 128

Z_GATES = 0
Z_MLSTM = 4096
Z_SCONV = 5120
Z_MOBA = 5888
Z_MIF = 6656
Z_RWKV = 7168
Z_WIDTH = 8192

VMEM_LIMIT = 48 * 1024 * 1024


def _cparams(*sem):
    return pltpu.CompilerParams(dimension_semantics=sem, vmem_limit_bytes=VMEM_LIMIT)


def _sigmoid(x):
    return 1.0 / (1.0 + jnp.exp(-x))


def _head_blockdiag(n):
    r = lax.broadcasted_iota(jnp.int32, (n, n), 0) // HEAD_DIM
    c = lax.broadcasted_iota(jnp.int32, (n, n), 1) // HEAD_DIM
    return (r == c).astype(F32)


def _head_sum(x, bd):
    return jnp.dot(x, bd, precision=HIGHEST, preferred_element_type=F32)


def _dot_nt(a, b):
    return lax.dot_general(a, b, (((1,), (1,)), ((), ())), preferred_element_type=F32)


def _dot_tn(a, b):
    return lax.dot_general(a, b, (((0,), (0,)), ((), ())), preferred_element_type=F32)


def _pick_tile(n, cap):
    t = cap
    while n % t:
        t //= 2
    return t


def _ada_kernel(c_ref, w_ref, b_ref, o_ref):
    c = c_ref[...]
    a = (c * _sigmoid(c)).astype(BF16)
    o_ref[...] = jnp.dot(a, w_ref[...], preferred_element_type=F32) + b_ref[...]


def _ada(c_all, w, b):
    m, d = c_all.shape
    n = w.shape[1]
    tn = 1536
    return pl.pallas_call(
        _ada_kernel,
        grid=(n // tn,),
        in_specs=[pl.BlockSpec((m, d), lambda j: (0, 0)),
                  pl.BlockSpec((d, tn), lambda j: (0, j)),
                  pl.BlockSpec((1, tn), lambda j: (0, j))],
        out_specs=pl.BlockSpec((m, tn), lambda j: (0, j)),
        out_shape=jax.ShapeDtypeStruct((m, n), F32),
        compiler_params=_cparams("parallel"),
        name="ada",
    )(c_all, w, b)


def _nmm_kernel(x_ref, g_ref, sc_ref, sh_ref, w_ref, o_ref, h_sc):
    @pl.when(pl.program_id(1) == 0)
    def _():
        x = x_ref[...]
        y = x * lax.rsqrt(jnp.mean(x * x, axis=-1, keepdims=True) + NORM_EPS) * g_ref[...]
        h_sc[...] = (y * (1.0 + sc_ref[...]) + sh_ref[...]).astype(BF16)

    o_ref[...] = jnp.dot(h_sc[...], w_ref[...], preferred_element_type=F32)


def _norm_mod_matmul(x, g, sc, sh, w, grp, tn):
    n, d = x.shape
    nout = w.shape[1]
    tm, tpg, r = grp["tm"], grp["tpg"], sc.shape[1]
    return pl.pallas_call(
        _nmm_kernel,
        grid=(n // tm, nout // tn),
        in_specs=[pl.BlockSpec((tm, d), lambda i, j: (i, 0)),
                  pl.BlockSpec((1, d), lambda i, j: (0, 0)),
                  pl.BlockSpec((None, r, d), lambda i, j: (i // tpg, 0, 0)),
                  pl.BlockSpec((None, r, d), lambda i, j: (i // tpg, 0, 0)),
                  pl.BlockSpec((d, tn), lambda i, j: (0, j))],
        out_specs=pl.BlockSpec((tm, tn), lambda i, j: (i, j)),
        out_shape=jax.ShapeDtypeStruct((n, nout), F32),
        scratch_shapes=[pltpu.VMEM((tm, d), BF16)],
        compiler_params=_cparams("parallel", "arbitrary"),
        name="norm_mod_matmul",
    )(x, g, sc, sh, w)


def _log_sigmoid(x):
    return jnp.minimum(x, 0.0) - jnp.log(1.0 + jnp.exp(-jnp.abs(x)))


def _mlstm_kernel(zq_ref, zif_ref, bias_ref, g_ref, c0_ref, n0_ref, m0_ref,
                  act_ref, c_out, n_out, m_out, c_sc, n_sc, m_sc, *, chunk, bb):
    t_ = chunk
    ci = pl.program_id(1)

    @pl.when(ci == 0)
    def _():
        c_sc[...] = c0_ref[...]
        n_sc[...] = n0_ref[...]
        m_sc[...] = m0_ref[...]

    row = lax.broadcasted_iota(jnp.int32, (t_, t_), 0)
    col = lax.broadcasted_iota(jnp.int32, (t_, t_), 1)
    causal = col <= row
    eye = row == col
    row1 = lax.broadcasted_iota(jnp.int32, (t_, 1), 0)
    for b, h in [(b, h) for b in range(bb) for h in range(N_HEADS)]:
        zq_b, act_b = zq_ref.at[b], act_ref.at[b]
        c_sc_b, n_sc_b, m_sc_b = c_sc.at[b], n_sc.at[b], m_sc.at[b]
        gates = zif_ref[b] + bias_ref[...]
        logf = _log_sigmoid(gates)
        lo = h * HEAD_DIM
        q = zq_b[:, lo:lo + HEAD_DIM]
        k = zq_b[:, BRANCH + lo:BRANCH + lo + HEAD_DIM] * (HEAD_DIM ** -0.5)
        v = zq_b[:, 2 * BRANCH + lo:2 * BRANCH + lo + HEAD_DIM]
        o = zq_b[:, 3 * BRANCH + lo:3 * BRANCH + lo + HEAD_DIM]
        li_col = gates[:, h:h + 1]
        lf_col = logf[:, N_HEADS + h:N_HEADS + h + 1]
        li_row = jnp.sum(jnp.where(eye, li_col, 0.0), axis=0, keepdims=True)
        lf_row = jnp.sum(jnp.where(eye, lf_col, 0.0), axis=0, keepdims=True)
        b_col = jnp.sum(jnp.where(causal, lf_row, 0.0), axis=1, keepdims=True)
        b_row = jnp.sum(jnp.where(row <= col, lf_col, 0.0), axis=0, keepdims=True)
        m_prev = m_sc_b[h][:, 0:1]
        d = jnp.where(causal, b_col - b_row + li_row, -jnp.inf)
        inter = b_col + m_prev
        mt = jnp.maximum(jnp.max(d, axis=1, keepdims=True), inter)
        qb, kb, vb = q.astype(BF16), k.astype(BF16), v.astype(BF16)
        w = jnp.exp(d - mt) * _dot_nt(qb, kb)
        a_int = jnp.exp(inter - mt)
        c = c_sc_b[h]
        n = n_sc_b[h]
        num = jnp.dot(w.astype(BF16), vb, preferred_element_type=F32) \
            + a_int * jnp.dot(qb, c.astype(BF16), preferred_element_type=F32)
        den = jnp.sum(w, axis=1, keepdims=True) + a_int * jnp.sum(q * n, axis=1, keepdims=True)
        hh = num / jnp.maximum(jnp.abs(den), jnp.exp(-mt))
        hn = hh * lax.rsqrt(jnp.mean(hh * hh, axis=-1, keepdims=True) + NORM_EPS)
        act_b[:, lo:lo + HEAD_DIM] = hn * g_ref[:, lo:lo + HEAD_DIM] * _sigmoid(o)
        b_end = jnp.sum(jnp.where(row1 == t_ - 1, b_col, 0.0), axis=0, keepdims=True)
        g_col = b_end - b_col + li_col
        m_new = jnp.maximum(b_end + m_prev, jnp.max(g_col, axis=0, keepdims=True))
        wk = jnp.exp(g_col - m_new) * k
        decay = jnp.exp(b_end + m_prev - m_new)
        c_sc_b[h] = decay * c + _dot_tn(wk.astype(BF16), vb)
        n_sc_b[h] = decay * n + jnp.sum(wk, axis=0, keepdims=True)
        m_sc_b[h] = jnp.broadcast_to(m_new, (1, LANES))

    @pl.when(ci == pl.num_programs(1) - 1)
    def _():
        c_out[...] = c_sc[...]
        n_out[...] = n_sc[...]
        m_out[...] = m_sc[...]


def _mlstm(z, bias_if, g, c0, n0, m0, grp):
    b_, l_ = grp["B"], grp["L"]
    t_ = min(MLSTM_CHUNK, l_)
    nc = l_ // t_
    h_ = N_HEADS
    bb = _pick_tile(b_, SCAN_BATCH)
    n0 = n0.reshape(b_, h_, 1, HEAD_DIM)
    m0 = jnp.broadcast_to(m0.reshape(b_, h_, 1, 1), (b_, h_, 1, LANES))
    z3 = z.reshape(b_, l_, Z_WIDTH)
    state_spec = lambda shp: pl.BlockSpec((bb,) + shp, lambda b, c: (b, 0, 0, 0))
    act, c1, n1, m1 = pl.pallas_call(
        functools.partial(_mlstm_kernel, chunk=t_, bb=bb),
        grid=(b_ // bb, nc),
        in_specs=[pl.BlockSpec((bb, t_, 4 * BRANCH), lambda b, c: (b, c, Z_MLSTM // (4 * BRANCH))),
                  pl.BlockSpec((bb, t_, LANES), lambda b, c: (b, c, Z_MIF // LANES)),
                  pl.BlockSpec((1, LANES), lambda b, c: (0, 0)),
                  pl.BlockSpec((1, BRANCH), lambda b, c: (0, 0)),
                  state_spec((h_, HEAD_DIM, HEAD_DIM)),
                  state_spec((h_, 1, HEAD_DIM)),
                  state_spec((h_, 1, LANES))],
        out_specs=[pl.BlockSpec((bb, t_, BRANCH), lambda b, c: (b, c, 0)),
                   state_spec((h_, HEAD_DIM, HEAD_DIM)),
                   state_spec((h_, 1, HEAD_DIM)),
                   state_spec((h_, 1, LANES))],
        out_shape=[jax.ShapeDtypeStruct((b_, l_, BRANCH), F32),
                   jax.ShapeDtypeStruct((b_, h_, HEAD_DIM, HEAD_DIM), F32),
                   jax.ShapeDtypeStruct((b_, h_, 1, HEAD_DIM), F32),
                   jax.ShapeDtypeStruct((b_, h_, 1, LANES), F32)],
        scratch_shapes=[pltpu.VMEM((bb, h_, HEAD_DIM, HEAD_DIM), F32),
                        pltpu.VMEM((bb, h_, 1, HEAD_DIM), F32),
                        pltpu.VMEM((bb, h_, 1, LANES), F32)],
        compiler_params=_cparams("parallel", "arbitrary"),
        name="mlstm",
    )(z3, z3, bias_if, g, c0, n0, m0)
    return act.reshape(b_ * l_, BRANCH), c1, n1.reshape(b_, h_, HEAD_DIM), m1[:, :, 0, 0]


def _conv3(p, prev2, prev1, w_ref):
    row = lax.broadcasted_iota(jnp.int32, p.shape, 0)
    p1 = jnp.where(row == 0, prev1, pltpu.roll(p, 1, axis=0))
    p2 = jnp.where(row == 0, prev2, jnp.where(row == 1, prev1, pltpu.roll(p, 2, axis=0)))
    return w_ref[0:1, :] * p2 + w_ref[1:2, :] * p1 + w_ref[2:3, :] * p


def _sconv_kernel(sb_ref, sc_ref, sh_ref, w_ref, buf_ref, out_ref, st_ref, carry, *, tm):
    @pl.when(pl.program_id(1) == 0)
    def _():
        carry[...] = buf_ref[0]

    p = sc_ref[...] * sh_ref[...]
    u = _conv3(p, carry[0:1, :], carry[1:2, :], w_ref)
    out_ref[...] = sb_ref[...] * u
    new = sc_ref[tm - 2:tm, :] * sh_ref[tm - 2:tm, :]
    carry[...] = new
    st_ref[0] = new


def _sconv(z, w, buf, grp):
    b_, l_ = grp["B"], grp["L"]
    tm = _pick_tile(l_, 512)
    nt = l_ // tm
    cb = Z_SCONV // BRANCH
    zspec = lambda j: pl.BlockSpec((tm, BRANCH), lambda b, i: (b * nt + i, cb + j))
    return pl.pallas_call(
        functools.partial(_sconv_kernel, tm=tm),
        grid=(b_, nt),
        in_specs=[zspec(0), zspec(1), zspec(2),
                  pl.BlockSpec((CONV_WIDTH, BRANCH), lambda b, i: (0, 0)),
                  pl.BlockSpec((1, CONV_WIDTH - 1, BRANCH), lambda b, i: (b, 0, 0))],
        out_specs=[pl.BlockSpec((tm, BRANCH), lambda b, i: (b * nt + i, 0)),
                   pl.BlockSpec((1, CONV_WIDTH - 1, BRANCH), lambda b, i: (b, 0, 0))],
        out_shape=[jax.ShapeDtypeStruct((b_ * l_, BRANCH), F32),
                   jax.ShapeDtypeStruct((b_, CONV_WIDTH - 1, BRANCH), F32)],
        scratch_shapes=[pltpu.VMEM((CONV_WIDTH - 1, BRANCH), F32)],
        compiler_params=_cparams("parallel", "arbitrary"),
        name="sconv",
    )(z, z, z, w, buf)


def _qk_prep_kernel(q_ref, k_ref, qg_ref, kg_ref, cos_ref, sa_ref, sb_ref, qo_ref, ko_ref):
    bd = _head_blockdiag(BRANCH)
    cos, sa, sb = cos_ref[...], sa_ref[...], sb_ref[...]
    half = ROPE_DIMS // 2

    def prep(x, g):
        y = x * lax.rsqrt(_head_sum(x * x, bd) * (1.0 / HEAD_DIM) + NORM_EPS) * g
        return y * cos + pltpu.roll(y, BRANCH - half, axis=1) * sa + pltpu.roll(y, half, axis=1) * sb

    qo_ref[...] = prep(q_ref[...], qg_ref[...])
    ko_ref[...] = prep(k_ref[...], kg_ref[...])


def _rope_tables(pos):
    half = ROPE_DIMS // 2
    inv = jnp.exp(-math.log(ROPE_THETA) * jnp.arange(0, ROPE_DIMS, 2, dtype=F32) / ROPE_DIMS)
    ang = pos.astype(F32)[:, None] * inv[None, :]
    cos, sin = jnp.cos(ang), jnp.sin(ang)
    n = pos.shape[0]
    rest = HEAD_DIM - ROPE_DIMS
    c_h = jnp.concatenate([cos, cos, jnp.ones((n, rest), F32)], axis=1)
    sa_h = jnp.concatenate([-sin, jnp.zeros((n, half + rest), F32)], axis=1)
    sb_h = jnp.concatenate([jnp.zeros((n, half), F32), sin, jnp.zeros((n, rest), F32)], axis=1)
    tile = lambda t: jnp.tile(t, (1, N_HEADS))
    return tile(c_h), tile(sa_h), tile(sb_h)


def _qk_prep(z, qg, kg, tables, grp):
    n = grp["B"] * grp["L"]
    ltab = tables[0].shape[0]
    tm = _pick_tile(ltab, 512)
    npos = ltab // tm
    cb = Z_MOBA // BRANCH
    tspec = pl.BlockSpec((tm, BRANCH), lambda i: (i % npos, 0))
    gspec = pl.BlockSpec((1, BRANCH), lambda i: (0, 0))
    ospec = pl.BlockSpec((tm, BRANCH), lambda i: (i, 0))
    return pl.pallas_call(
        _qk_prep_kernel,
        grid=(n // tm,),
        in_specs=[pl.BlockSpec((tm, BRANCH), lambda i: (i, cb)),
                  pl.BlockSpec((tm, BRANCH), lambda i: (i, cb + 1)),
                  gspec, gspec, tspec, tspec, tspec],
        out_specs=[ospec, ospec],
        out_shape=[jax.ShapeDtypeStruct((n, BRANCH), F32)] * 2,
        compiler_params=_cparams("parallel"),
        name="qk_prep",
    )(z, z, qg, kg, *tables)


def _moba_prompt_kernel(q_ref, k_ref, v_ref, o_ref, kmean, kx, vx, *, nb):
    qi = pl.program_id(1)
    tq = MB_BLOCK
    half = LANES // 2
    masked = float(jnp.finfo(BF16).min)
    lane = lax.broadcasted_iota(jnp.int32, (tq, LANES), 1)
    low = lane < half

    @pl.when(qi == 0)
    def _():
        kmean[...] = jnp.zeros_like(kmean)
        for n in range(nb):
            rows = slice(n * MB_BLOCK, (n + 1) * MB_BLOCK)
            kmean[n:n + 1, :] = jnp.mean(k_ref[rows, :], axis=0, keepdims=True)
            onehot = (lane == half + n).astype(F32)
            for h in range(N_HEADS):
                pair = slice((h // 2) * LANES, (h // 2 + 1) * LANES)
                k2, v2 = k_ref[rows, pair], v_ref[rows, pair]
                if h % 2:
                    k2, v2 = pltpu.roll(k2, half, axis=1), pltpu.roll(v2, half, axis=1)
                kx[h, rows, :] = jnp.where(low, k2, onehot).astype(BF16)
                vx[h, rows, :] = jnp.where(low, v2, 1.0).astype(BF16)

    row = lax.broadcasted_iota(jnp.int32, (tq, tq), 0)
    col = lax.broadcasted_iota(jnp.int32, (tq, tq), 1)
    scale = HEAD_DIM ** -0.5
    own0 = pl.multiple_of(qi * MB_BLOCK, MB_BLOCK)
    heads = range(N_HEADS)
    sub = 8
    assert nb <= sub
    blk = lax.broadcasted_iota(jnp.int32, (sub, tq), 0)
    place = (lax.broadcasted_iota(jnp.int32, (sub, LANES), 1)
             == lax.broadcasted_iota(jnp.int32, (sub, LANES), 0) + half).astype(BF16)
    qx = []
    for h in heads:
        hs = slice(h * HEAD_DIM, (h + 1) * HEAD_DIM)
        s = lax.dot_general(kmean[0:sub, hs], q_ref[:, hs], (((1,), (1,)), ((), ())),
                            precision=HIGHEST, preferred_element_type=F32)
        valid = blk < qi
        s = jnp.where(valid, s, -jnp.inf)
        rank = jnp.zeros((sub, tq), jnp.int32)
        for m in range(nb):
            sm = s[m:m + 1, :]
            rank += ((sm > s) | ((sm == s) & (m < blk))).astype(jnp.int32)
        keep = ((valid & (rank < MB_TOPK)) | (blk == qi)).astype(BF16)
        keep_q = _dot_tn(keep, place)
        q2 = q_ref[:, (h // 2) * LANES:(h // 2 + 1) * LANES] * scale
        if h % 2:
            q2 = pltpu.roll(q2, half, axis=1)
        qx.append(jnp.where(low, q2, jnp.where(keep_q > 0.5, 0.0, masked)).astype(BF16))
    sc = [jnp.where(col <= row, _dot_nt(qx[h], kx[h, pl.ds(own0, MB_BLOCK), :]), -jnp.inf) for h in heads]
    m_i = [jnp.max(sc[h], axis=1, keepdims=True) for h in heads]
    p = [jnp.exp(sc[h] - m_i[h]).astype(BF16) for h in heads]
    acc = [jnp.dot(p[h], vx[h, pl.ds(own0, MB_BLOCK), :], preferred_element_type=F32) for h in heads]

    def body(n, carry):
        m_i, acc = carry
        k0 = pl.multiple_of(n * MB_BLOCK, MB_BLOCK)
        sc = [_dot_nt(qx[h], kx[h, pl.ds(k0, MB_BLOCK), :]) for h in heads]
        m_new = [jnp.maximum(m_i[h], jnp.max(sc[h], axis=1, keepdims=True)) for h in heads]
        p = [jnp.exp(sc[h] - m_new[h]).astype(BF16) for h in heads]
        pv = [jnp.dot(p[h], vx[h, pl.ds(k0, MB_BLOCK), :], preferred_element_type=F32) for h in heads]
        return m_new, [jnp.exp(m_i[h] - m_new[h]) * acc[h] + pv[h] for h in heads]

    m_i, acc = lax.fori_loop(0, qi, body, (m_i, acc))
    for h in heads:
        o_ref[:, h * HEAD_DIM:(h + 1) * HEAD_DIM] = (acc[h] / pltpu.roll(acc[h], half, axis=1))[:, :half]


def _moba_prompt(q, k, z, grp):
    b_, l_ = grp["B"], grp["L"]
    assert l_ % MB_BLOCK == 0 and l_ // MB_BLOCK <= LANES // 2
    nb = l_ // MB_BLOCK
    return pl.pallas_call(
        functools.partial(_moba_prompt_kernel, nb=nb),
        grid=(b_, nb),
        in_specs=[pl.BlockSpec((MB_BLOCK, BRANCH), lambda b, i: (b * nb + i, 0)),
                  pl.BlockSpec((l_, BRANCH), lambda b, i: (b, 0)),
                  pl.BlockSpec((l_, BRANCH), lambda b, i: (b, Z_MOBA // BRANCH + 2))],
        out_specs=pl.BlockSpec((MB_BLOCK, BRANCH), lambda b, i: (b * nb + i, 0)),
        out_shape=jax.ShapeDtypeStruct((b_ * l_, BRANCH), F32),
        scratch_shapes=[pltpu.VMEM((LANES, BRANCH), F32),
                        pltpu.VMEM((N_HEADS, l_, LANES), BF16),
                        pltpu.VMEM((N_HEADS, l_, LANES), BF16)],
        compiler_params=_cparams("parallel", "arbitrary"),
        name="moba_prompt",
    )(q, k, z)


def _moba_sample_kernel(pt_ref, q_ref, kn_ref, vn_ref, *refs, nblk, dec, gsz):
    ppb = MB_BLOCK // PAGE_SIZE
    k_refs, v_refs = refs[:ppb * gsz], refs[ppb * gsz:2 * ppb * gsz]
    o_ref, kmean_s, m_s, l_s, o_s = refs[2 * ppb * gsz:]
    step = pl.program_id(1)
    nq = N_HEADS * dec
    scale = HEAD_DIM ** -0.5
    lane = lax.broadcasted_iota(jnp.int32, (nq, LANES), 1)
    lane_c = lax.broadcasted_iota(jnp.int32, (BRANCH, LANES), 1)

    @pl.when(step == 0)
    def _():
        kmean_s[...] = jnp.zeros_like(kmean_s)
        m_s[...] = jnp.full_like(m_s, -jnp.inf)
        l_s[...] = jnp.zeros_like(l_s)

    row_head = jnp.concatenate([jnp.full((dec, BRANCH), h, jnp.int32) for h in range(N_HEADS)], axis=0)
    lane_head = lax.broadcasted_iota(jnp.int32, (nq, BRANCH), 1) // HEAD_DIM
    qbd = jnp.where(row_head == lane_head, jnp.concatenate([q_ref[...]] * N_HEADS, axis=0), 0.0)
    qbd_b = qbd.astype(BF16)
    kmean, m_all, l_all = kmean_s[...], m_s[...], l_s[...]
    blocks = range(gsz)
    kt = [jnp.concatenate([k_refs[ppb * g + j][...] for j in range(ppb)], axis=1) for g in blocks]
    s = [jnp.dot(qbd_b, kt[g].astype(BF16), preferred_element_type=F32) * scale for g in blocks]
    mx = [jnp.max(s[g], axis=1, keepdims=True) for g in blocks]
    p = [jnp.exp(s[g] - mx[g]) for g in blocks]
    vt = [jnp.concatenate([v_refs[ppb * g + j][...] for j in range(ppb)], axis=1) for g in blocks]
    o = [_dot_nt(p[g].astype(BF16), vt[g].astype(BF16)) for g in blocks]
    for g in blocks:
        blk = step * gsz + g
        o_s[blk] = o[g]
        kmean = jnp.where(lane_c == blk, jnp.mean(kt[g], axis=1, keepdims=True), kmean)
        m_all = jnp.where(lane == blk, mx[g], m_all)
        l_all = jnp.where(lane == blk, jnp.sum(p[g], axis=1, keepdims=True), l_all)
    kmean_s[...] = kmean
    m_s[...] = m_all
    l_s[...] = l_all

    @pl.when(step == pl.num_programs(1) - 1)
    def _():
        zpad = jnp.zeros((LANES - dec, BRANCH), F32)
        knew = jnp.concatenate([kn_ref[...], zpad], axis=0)
        vnew = jnp.concatenate([vn_ref[...], zpad], axis=0)
        rowq = jnp.concatenate([lax.broadcasted_iota(jnp.int32, (dec, LANES), 0)] * N_HEADS, axis=0)
        lane_f = lane.astype(F32)
        s = jnp.dot(qbd, kmean_s[...], precision=HIGHEST, preferred_element_type=F32)
        s = jnp.where(lane < nblk, s, -jnp.inf)
        sel = jnp.zeros((nq, LANES), jnp.bool_)
        for _ in range(MB_TOPK):
            mxv = jnp.max(s, axis=1, keepdims=True)
            idx = jnp.min(jnp.where(s == mxv, lane_f, 2.0 * LANES), axis=1, keepdims=True)
            pick = lane_f == idx
            sel = sel | pick
            s = jnp.where(pick, -jnp.inf, s)
        sel = sel & (lane < nblk)
        s_own = jnp.where(lane <= rowq, _dot_nt(qbd_b, knew.astype(BF16)) * scale, -jnp.inf)
        m_own = jnp.max(s_own, axis=1, keepdims=True)
        p_own = jnp.exp(s_own - m_own)
        l_own = jnp.sum(p_own, axis=1, keepdims=True)
        o_own = jnp.dot(p_own.astype(BF16), vnew.astype(BF16), preferred_element_type=F32)
        mb = m_s[...]
        mtot = jnp.maximum(jnp.max(jnp.where(sel, mb, -jnp.inf), axis=1, keepdims=True), m_own)
        wgt = jnp.where(sel, jnp.exp(mb - mtot), 0.0)
        a_own = jnp.exp(m_own - mtot)
        ltot = jnp.sum(wgt * l_s[...], axis=1, keepdims=True) + l_own * a_own
        acc = o_own * a_own
        for j in range(nblk):
            acc = acc + wgt[:, j:j + 1] * o_s[j]
        res = acc / ltot
        for h in range(N_HEADS):
            hs = slice(h * HEAD_DIM, (h + 1) * HEAD_DIM)
            o_ref[:, hs] = res[h * dec:(h + 1) * dec, hs]


def _moba_sample(q, k, v, pool_k, pool_v, page_table, layer, grp):
    b_, dec = grp["B"], grp["L"]
    n_pages = page_table.shape[1]
    ppb = MB_BLOCK // PAGE_SIZE
    assert n_pages % ppb == 0
    nblk = n_pages // ppb
    assert MB_TOPK <= nblk <= LANES and dec <= LANES and dec % 8 == 0
    gsz = max(g for g in range(1, 9) if nblk % g == 0)
    as_pages = lambda t: t.transpose(0, 1, 3, 4, 2).reshape(t.shape[0], t.shape[1], BRANCH, PAGE_SIZE)
    pool_k, pool_v = as_pages(pool_k), as_pages(pool_v)
    pt = page_table.reshape(-1)

    def page(j):
        return pl.BlockSpec((None, None, BRANCH, PAGE_SIZE),
                            lambda b, s, pt: (layer, pt[b * n_pages + s * gsz * ppb + j], 0, 0))

    pages = [page(j) for j in range(gsz * ppb)]
    new = pl.BlockSpec((dec, BRANCH), lambda b, s, pt: (b, 0))
    nq = N_HEADS * dec
    return pl.pallas_call(
        functools.partial(_moba_sample_kernel, nblk=nblk, dec=dec, gsz=gsz),
        grid_spec=pltpu.PrefetchScalarGridSpec(
            num_scalar_prefetch=1,
            grid=(b_, nblk // gsz),
            in_specs=[new, new, new] + pages + pages,
            out_specs=new,
            scratch_shapes=[pltpu.VMEM((BRANCH, LANES), F32),
                            pltpu.VMEM((nq, LANES), F32),
                            pltpu.VMEM((nq, LANES), F32),
                            pltpu.VMEM((nblk, nq, BRANCH), F32)]),
        out_shape=jax.ShapeDtypeStruct((b_ * dec, BRANCH), F32),
        compiler_params=_cparams("parallel", "arbitrary"),
        name="moba_sample",
    )(pt, q, k, v, *([pool_k] * (gsz * ppb)), *([pool_v] * (gsz * ppb)))


def _rwkv_prep_kernel(z_ref, prev_ref, mu_ref, wlr_ref, w0_ref, a0_ref, kks_ref, kas_ref, rk_ref,
                      r_o, w_o, k_o, v_o, kk_o, kka_o, bonus_o, g_o, carry, *, tm):
    @pl.when(pl.program_id(1) == 0)
    def _():
        carry[...] = prev_ref[0]

    z = z_ref[...]
    row = lax.broadcasted_iota(jnp.int32, z.shape, 0)
    zs = jnp.where(row == 0, carry[...], pltpu.roll(z, 1, axis=0))
    carry[...] = z_ref[tm - 1:tm, :]
    zz = z + mu_ref[...] * (zs - z)
    r = zz[:, 0:BRANCH]
    k = zz[:, BRANCH:2 * BRANCH]
    v = zz[:, 2 * BRANCH:3 * BRANCH]
    lr = zz[:, 3 * BRANCH:3 * BRANCH + RW_LORA]
    lane = lax.broadcasted_iota(jnp.int32, lr.shape, 1)
    lr_in = jnp.where(lane < 32, jnp.tanh(lr), jnp.where(lane < 64, lr, _sigmoid(lr)))
    lo = jnp.dot(lr_in.astype(BF16), wlr_ref[...], preferred_element_type=F32)
    log_decay = -math.exp(-0.5) * _sigmoid(w0_ref[...] + lo[:, 0:BRANCH])
    a = _sigmoid(a0_ref[...] + lo[:, BRANCH:2 * BRANCH])
    bd = _head_blockdiag(BRANCH)
    kk = k * kks_ref[...]
    kk = kk * lax.rsqrt(jnp.maximum(_head_sum(kk * kk, bd), 1e-24))
    k2 = k * (1.0 + (a - 1.0) * kas_ref[...])
    r_o[...] = r
    w_o[...] = log_decay
    k_o[...] = k2
    v_o[...] = v
    kk_o[...] = kk
    kka_o[...] = kk * a
    bonus_o[...] = _head_sum(r * k2 * rk_ref[...], bd) * v
    g_o[...] = lo[:, 2 * BRANCH:3 * BRANCH]


def _rwkv_prep(z, prev, p, grp):
    b_, l_ = grp["B"], grp["L"]
    tm = _pick_tile(l_, 256)
    nt = l_ // tm
    zw = Z_WIDTH - Z_RWKV
    vec = pl.BlockSpec((1, BRANCH), lambda b, i: (0, 0))
    ospec = pl.BlockSpec((tm, BRANCH), lambda b, i: (b * nt + i, 0))
    return pl.pallas_call(
        functools.partial(_rwkv_prep_kernel, tm=tm),
        grid=(b_, nt),
        in_specs=[pl.BlockSpec((tm, zw), lambda b, i: (b * nt + i, Z_RWKV // zw)),
                  pl.BlockSpec((1, 1, zw), lambda b, i: (b, 0, 0)),
                  pl.BlockSpec((1, zw), lambda b, i: (0, 0)),
                  pl.BlockSpec((RW_LORA, 3 * BRANCH), lambda b, i: (0, 0)),
                  vec, vec, vec, vec, vec],
        out_specs=[ospec] * 8,
        out_shape=[jax.ShapeDtypeStruct((b_ * l_, BRANCH), F32)] * 8,
        scratch_shapes=[pltpu.VMEM((1, zw), F32)],
        compiler_params=_cparams("parallel", "arbitrary"),
        name="rwkv_prep",
    )(z, prev, p["mu"], p["wlr"], p["w0"], p["a0"], p["kk"], p["ka"], p["rk"])


def _rwkv_chunk_kernel(r_ref, lw_ref, k_ref, v_ref, kk_ref, kka_ref, h0_ref, y_ref, h_out, h_sc, *, chunk, bb):
    t_ = chunk
    ci = pl.program_id(1)

    @pl.when(ci == 0)
    def _():
        h_sc[...] = h0_ref[...]

    t4 = N_HEADS * t_
    rows = list(range(bb))
    each = lambda f, *xs: [f(*a) for a in zip(*xs)]
    tri = (lax.broadcasted_iota(jnp.int32, (t_, t_), 0) >= lax.broadcasted_iota(jnp.int32, (t_, t_), 1)).astype(F32)
    row1 = lax.broadcasted_iota(jnp.int32, (t_, 1), 0)
    head_mask = (jnp.concatenate([jnp.full((t_, BRANCH), h, jnp.int32) for h in range(N_HEADS)], axis=0)
                 == lax.broadcasted_iota(jnp.int32, (t4, BRANCH), 1) // HEAD_DIM)
    t_row = jnp.concatenate([lax.broadcasted_iota(jnp.int32, (t_, t4), 0)] * N_HEADS, axis=0)
    t_col = lax.broadcasted_iota(jnp.int32, (t4, t4), 1) & (t_ - 1)
    strict = t_row > t_col
    lower = t_row >= t_col
    stack = lambda x: jnp.where(head_mask, jnp.concatenate([x] * N_HEADS, axis=0), 0.0).astype(BF16)
    mm = lambda a, b: jnp.dot(a, b, preferred_element_type=F32)

    lw = [lw_ref[b] for b in rows]
    cum = each(lambda x: jnp.dot(tri, x, precision=HIGHEST, preferred_element_type=F32), lw)
    p_in = each(jnp.exp, cum)
    p_inv = each(lambda c: jnp.exp(-c), cum)
    p_end = each(lambda p: jnp.sum(jnp.where(row1 == t_ - 1, p, 0.0), axis=0, keepdims=True), p_in)
    kkm = [stack(kk_ref[b] * jnp.exp(cum[b] - lw[b])) for b in rows]
    rp = [stack(r_ref[b] * p_in[b]) for b in rows]
    kh = [k_ref[b] * p_inv[b] for b in rows]
    ah = [kka_ref[b] * p_inv[b] for b in rows]
    khe = [stack(kh[b] * p_end[b]) for b in rows]
    ahe = [stack(ah[b] * p_end[b]) for b in rows]
    khm, ahm = each(stack, kh), each(stack, ah)
    vm = [stack(v_ref[b]) for b in rows]
    lr = each(lambda a, b: jnp.concatenate([a, b], axis=0), kkm, rp)
    gk = each(_dot_nt, lr, khm)
    ga = each(_dot_nt, lr, ahm)
    ab_k = each(lambda g: jnp.concatenate([jnp.where(strict, g[:t4], 0.0), jnp.where(lower, g[t4:], 0.0)],
                                          axis=0).astype(BF16), gk)
    b_a = each(lambda g: jnp.where(lower, g[t4:], 0.0).astype(BF16), ga)
    e = each(lambda g: jnp.where(strict, -g[:t4], 0.0), ga)
    pw = e
    span = 2
    while span < t_:
        pw = each(lambda p: mm(p.astype(BF16), p.astype(BF16)), pw)
        e = each(lambda x, p: x + p + mm(x.astype(BF16), p.astype(BF16)), e, pw)
        span *= 2
    ht = [h_sc[b] for b in rows]
    x0 = each(lambda a, h: _dot_nt(a, h.astype(BF16)), lr, ht)
    kv = each(mm, ab_k, vm)
    x1 = each(lambda a, b: a[:t4] + b[:t4], x0, kv)
    u = each(lambda x, m: x + mm(m.astype(BF16), x.astype(BF16)), x1, e)
    ub = each(lambda x: x.astype(BF16), u)
    ybd = each(lambda a, b, m, x: a[t4:] + b[t4:] - mm(m, x), x0, kv, b_a, ub)
    for b in rows:
        y = ybd[b][0:t_]
        for h in range(1, N_HEADS):
            y = y + ybd[b][h * t_:(h + 1) * t_]
        y_ref[b] = y
        h_sc[b] = ht[b] * p_end[b] + _dot_tn(vm[b], khe[b]) - _dot_tn(ub[b], ahe[b])

    @pl.when(ci == pl.num_programs(1) - 1)
    def _():
        h_out[...] = h_sc[...]


def _rwkv_scan(r, lw, k, v, kk, kka, s0, grp):
    b_, l_ = grp["B"], grp["L"]
    chunk = _pick_tile(l_, RWKV_CHUNK)
    nc = l_ // chunk
    bb = _pick_tile(b_, RWKV_BATCH)
    seq = pl.BlockSpec((bb, chunk, BRANCH), lambda b, c: (b, c, 0))
    st = pl.BlockSpec((bb, BRANCH, BRANCH), lambda b, c: (b, 0, 0))
    as3 = lambda t: t.reshape(b_, l_, BRANCH)
    blocks = [slice(h * HEAD_DIM, (h + 1) * HEAD_DIM) for h in range(N_HEADS)]
    s0_bd = jnp.zeros((b_, BRANCH, BRANCH), F32)
    for h, hs in enumerate(blocks):
        s0_bd = s0_bd.at[:, hs, hs].set(s0[:, h])
    y, s1 = pl.pallas_call(
        functools.partial(_rwkv_chunk_kernel, chunk=chunk, bb=bb),
        grid=(b_ // bb, nc),
        in_specs=[seq] * 6 + [st],
        out_specs=[seq, st],
        out_shape=[jax.ShapeDtypeStruct((b_, l_, BRANCH), F32),
                   jax.ShapeDtypeStruct((b_, BRANCH, BRANCH), F32)],
        scratch_shapes=[pltpu.VMEM((bb, BRANCH, BRANCH), F32)],
        compiler_params=_cparams("parallel", "arbitrary"),
        name="rwkv_scan",
    )(as3(r), as3(lw), as3(k), as3(v), as3(kk), as3(kka), s0_bd)
    return y.reshape(b_ * l_, BRANCH), jnp.stack([s1[:, hs, hs] for hs in blocks], axis=1)


def _merge_kernel(ml_ref, sc_ref, mb_ref, y_ref, bonus_ref, g_ref, gn_ref, gates_ref, x_ref, gt_ref,
                  mp_ref, sp_ref, ap_ref, rp_ref, wo_ref, o_ref):
    bd = _head_blockdiag(BRANCH)
    y = y_ref[...]
    yc = y - _head_sum(y, bd) * (1.0 / HEAD_DIM)
    yn = yc * lax.rsqrt(_head_sum(yc * yc, bd) * (1.0 / HEAD_DIM) + GN_EPS)
    rw = (yn * gn_ref[...] + bonus_ref[...]) * g_ref[...]
    acts = (ml_ref[...], sc_ref[...], mb_ref[...], rw)
    projs = (mp_ref, sp_ref, ap_ref, rp_ref)
    merged = None
    for j in range(4):
        br = jnp.dot(acts[j].astype(BF16), projs[j][...], preferred_element_type=F32)
        term = _sigmoid(gates_ref[:, j * D_MODEL:(j + 1) * D_MODEL]) * br
        merged = term if merged is None else merged + term
    o_ref[...] = x_ref[...] + gt_ref[...] * jnp.dot(merged.astype(BF16), wo_ref[...], preferred_element_type=F32)


def _merge(ml, sc, mb, y, bonus, g, gn, z, x, gt, projs, w_out, grp):
    n, d = x.shape
    tm = min(256, grp["tm"])
    tpg = grp["tpg"] * (grp["tm"] // tm)
    r = gt.shape[1]
    if r > 1:
        assert r == grp["tm"] and tpg == grp["tm"] // tm
        gt = gt.reshape(r // tm, tm, d)
        gt_spec = pl.BlockSpec((None, tm, d), lambda i: (i, 0, 0))
    else:
        gt_spec = pl.BlockSpec((None, 1, d), lambda i: (i // tpg, 0, 0))
    act = pl.BlockSpec((tm, BRANCH), lambda i: (i, 0))
    proj = pl.BlockSpec((BRANCH, d), lambda i: (0, 0))
    return pl.pallas_call(
        _merge_kernel,
        grid=(n // tm,),
        in_specs=[act] * 6 + [pl.BlockSpec((1, BRANCH), lambda i: (0, 0)),
                              pl.BlockSpec((tm, 4 * d), lambda i: (i, 0)),
                              pl.BlockSpec((tm, d), lambda i: (i, 0)),
                              gt_spec, proj, proj, proj, proj,
                              pl.BlockSpec((d, d), lambda i: (0, 0))],
        out_specs=pl.BlockSpec((tm, d), lambda i: (i, 0)),
        out_shape=jax.ShapeDtypeStruct((n, d), F32),
        compiler_params=_cparams("parallel"),
        name="merge",
    )(ml, sc, mb, y, bonus, g, gn, z, x, gt, *projs, w_out)


def _ffn_act_kernel(ua_ref, ul_ref, w_ref, buf_ref, o_ref, carry, *, tm):
    @pl.when(pl.program_id(1) == 0)
    def _():
        carry[...] = buf_ref[0]

    ua = ua_ref[...]
    u = _conv3(ua, carry[0:1, :], carry[1:2, :], w_ref)
    carry[...] = ua_ref[tm - 2:tm, :]
    o_ref[...] = (u * _sigmoid(u) * ul_ref[...]).astype(BF16)


def _ffn_act(up, w, buf, grp):
    b_, l_ = grp["B"], grp["L"]
    tm = _pick_tile(l_, 256)
    nt = l_ // tm
    half = lambda j: pl.BlockSpec((tm, D_FF), lambda b, i: (b * nt + i, j))
    return pl.pallas_call(
        functools.partial(_ffn_act_kernel, tm=tm),
        grid=(b_, nt),
        in_specs=[half(0), half(1),
                  pl.BlockSpec((CONV_WIDTH, D_FF), lambda b, i: (0, 0)),
                  pl.BlockSpec((1, CONV_WIDTH - 1, D_FF), lambda b, i: (b, 0, 0))],
        out_specs=pl.BlockSpec((tm, D_FF), lambda b, i: (b * nt + i, 0)),
        out_shape=jax.ShapeDtypeStruct((b_ * l_, D_FF), BF16),
        scratch_shapes=[pltpu.VMEM((CONV_WIDTH - 1, D_FF), F32)],
        compiler_params=_cparams("parallel", "arbitrary"),
        name="ffn_act",
    )(up, up, w, buf)


def _mm_res_kernel(a_ref, w_ref, x_ref, gt_ref, o_ref):
    o_ref[...] = x_ref[...] + gt_ref[...] * jnp.dot(a_ref[...], w_ref[...], preferred_element_type=F32)


def _mm_res(a, w, x, gt, grp):
    n, d = x.shape
    kdim = a.shape[1]
    tm, tpg, r = min(512, grp["tm"]), grp["tpg"] * (grp["tm"] // min(512, grp["tm"])), gt.shape[1]
    if r > 1:
        gt = gt.reshape(r // tm, tm, d)
        gt_spec = pl.BlockSpec((None, tm, d), lambda i: (i, 0, 0))
    else:
        gt_spec = pl.BlockSpec((None, 1, d), lambda i: (i // tpg, 0, 0))
    return pl.pallas_call(
        _mm_res_kernel,
        grid=(n // tm,),
        in_specs=[pl.BlockSpec((tm, kdim), lambda i: (i, 0)),
                  pl.BlockSpec((kdim, d), lambda i: (0, 0)),
                  pl.BlockSpec((tm, d), lambda i: (i, 0)),
                  gt_spec],
        out_specs=pl.BlockSpec((tm, d), lambda i: (i, 0)),
        out_shape=jax.ShapeDtypeStruct((n, d), F32),
        compiler_params=_cparams("parallel"),
        name="mm_res",
    )(a, w, x, gt)


def _prep_layer(l, P):
    d = D_MODEL
    w_in = P["w_in"][l]
    o_mif = 4 * BRANCH
    o_sb = o_mif + 2 * N_HEADS
    o_aq = o_sb + 3 * BRANCH
    o_rw = o_aq + 3 * BRANCH
    o_g = o_rw + RW_COLS
    w_in_p = jnp.concatenate([
        w_in[:, o_g:o_g + 4 * d], w_in[:, 0:o_mif], w_in[:, o_sb:o_aq], w_in[:, o_aq:o_rw],
        w_in[:, o_mif:o_sb], jnp.zeros((d, Z_RWKV - Z_MIF - 2 * N_HEADS), F32),
        w_in[:, o_rw:o_g], jnp.zeros((d, Z_WIDTH - Z_RWKV - RW_COLS), F32)], axis=1).astype(BF16)
    bias_if = jnp.concatenate([P["m_bi"][l], P["m_bf"][l], jnp.zeros((LANES - 2 * N_HEADS,), F32)])[None, :]
    zw = Z_WIDTH - Z_RWKV
    wlr = jnp.zeros((RW_LORA, 3 * BRANCH), F32)
    wlr = wlr.at[0:32, 0:BRANCH].set(P["r_wB"][l])
    wlr = wlr.at[32:64, BRANCH:2 * BRANCH].set(P["r_aB"][l])
    wlr = wlr.at[64:128, 2 * BRANCH:].set(P["r_gB"][l])
    row = lambda t: t[None, :]
    rw = dict(mu=jnp.pad(P["r_mu"][l], (0, zw - RW_COLS))[None, :], wlr=wlr.astype(BF16),
              w0=row(P["r_w0"][l]), a0=row(P["r_a0"][l]), kk=row(P["r_kk"][l]), ka=row(P["r_ka"][l]),
              rk=row(P["r_rk"][l]))
    bf = lambda name: P[name][l].astype(BF16)
    return dict(
        w_ada=bf("w_ada"), b_ada=row(P["b_ada"][l]), w_in=w_in_p, bias_if=bias_if,
        norm_mix_g=row(P["norm_mix_g"][l]), norm_ffn_g=row(P["norm_ffn_g"][l]),
        m_norm_g=row(P["m_norm_g"][l]), s_conv=P["s_conv"][l],
        qg=row(jnp.tile(P["a_qnorm"][l], N_HEADS)), kg=row(jnp.tile(P["a_knorm"][l], N_HEADS)),
        rw=rw, r_norm_g=row(P["r_norm_g"][l]),
        projs=(bf("m_proj"), bf("s_proj"), bf("a_proj"), bf("r_proj")), w_out=bf("w_out"),
        f_up=bf("f_up"), f_conv=P["f_conv"][l], f_down=bf("f_down"))


def _layer(x, mod, W, state, tables, attend, grp):
    mc, mn, mm, sbuf, rs, rshift, fbuf = state
    b_, l_ = grp["B"], grp["L"]
    sh_m, sc_m, gt_m, sh_f, sc_f, gt_f = mod
    z = _norm_mod_matmul(x, W["norm_mix_g"], sc_m, sh_m, W["w_in"], grp, 1024)
    ml, mc, mn, mm = _mlstm(z, W["bias_if"], W["m_norm_g"], mc, mn, mm, grp)
    sc, sbuf = _sconv(z, W["s_conv"], sbuf, grp)
    q, k = _qk_prep(z, W["qg"], W["kg"], tables, grp)
    v = z[:, Z_MOBA + 2 * BRANCH:Z_MOBA + 3 * BRANCH]
    mb = attend(q, k, z, v)
    zw = Z_WIDTH - Z_RWKV
    prev = jnp.pad(rshift, ((0, 0), (0, zw - RW_COLS)))[:, None, :]
    r, w, k2, vv, kk, kka, bonus, g = _rwkv_prep(z, prev, W["rw"], grp)
    y, rs_t = _rwkv_scan(r, w, k2, vv, kk, kka, rs, grp)
    rshift = z.reshape(b_, l_, Z_WIDTH)[:, -1, Z_RWKV:Z_RWKV + RW_COLS]
    x = _merge(ml, sc, mb, y, bonus, g, W["r_norm_g"], z, x, gt_m, W["projs"], W["w_out"], grp)
    up = _norm_mod_matmul(x, W["norm_ffn_g"], sc_f, sh_f, W["f_up"], grp, 1408)
    act = _ffn_act(up, W["f_conv"], fbuf, grp)
    fbuf = up.reshape(b_, l_, 2 * D_FF)[:, l_ - (CONV_WIDTH - 1):, :D_FF]
    x = _mm_res(act, W["f_down"], x, gt_f, grp)
    return x, (k, v), (mc, mn, mm, sbuf, rs_t, rshift, fbuf)


def kernel(x_prompt, x_sample, c_prompt, c_sample, cache_k, cache_v, page_table, state_mlstm_c, state_mlstm_n, state_mlstm_m, state_conv, state_rwkv, state_rwkv_shift, state_ffn_conv, norm_mix_g, norm_ffn_g, w_ada, b_ada, w_in, m_bi, m_bf, m_norm_g, m_proj, s_conv, s_proj, a_qnorm, a_knorm, a_proj, r_mu, r_w0, r_wB, r_a0, r_aB, r_gB, r_kk, r_ka, r_rk, r_norm_g, r_proj, w_out, f_up, f_conv, f_down):
    P = dict(norm_mix_g=norm_mix_g, norm_ffn_g=norm_ffn_g, w_ada=w_ada, b_ada=b_ada, w_in=w_in,
             m_bi=m_bi, m_bf=m_bf, m_norm_g=m_norm_g, m_proj=m_proj, s_conv=s_conv, s_proj=s_proj,
             a_qnorm=a_qnorm, a_knorm=a_knorm, a_proj=a_proj, r_mu=r_mu, r_w0=r_w0, r_wB=r_wB,
             r_a0=r_a0, r_aB=r_aB, r_gB=r_gB, r_kk=r_kk, r_ka=r_ka, r_rk=r_rk, r_norm_g=r_norm_g,
             r_proj=r_proj, w_out=w_out, f_up=f_up, f_conv=f_conv, f_down=f_down)
    depth = w_in.shape[0]
    bp, lp, d = x_prompt.shape
    bs, ls, _ = x_sample.shape
    n_s = bs * ls
    past = page_table.shape[1] * PAGE_SIZE
    tm_p = _pick_tile(lp, 1024)
    grp_p = dict(B=bp, L=lp, tm=tm_p, tpg=lp // tm_p)
    grp_s = dict(B=bs, L=ls, tm=n_s, tpg=1)
    assert n_s % 8 == 0 and n_s <= 1024

    zeros = lambda *s: jnp.zeros(s, F32)
    st_p = (zeros(bp, N_HEADS, HEAD_DIM, HEAD_DIM), zeros(bp, N_HEADS, HEAD_DIM), zeros(bp, N_HEADS),
            zeros(bp, CONV_WIDTH - 1, BRANCH), zeros(bp, N_HEADS, HEAD_DIM, HEAD_DIM), zeros(bp, RW_COLS),
            zeros(bp, CONV_WIDTH - 1, D_FF))
    tab_p = _rope_tables(jnp.arange(lp, dtype=jnp.int32))
    tab_s = tuple(jnp.tile(t, (bs, 1)) for t in _rope_tables(past + jnp.arange(ls, dtype=jnp.int32)))

    c_all = jnp.concatenate([c_prompt, c_sample], axis=0)
    hp = x_prompt.reshape(bp * lp, d)
    hs = x_sample.reshape(n_s, d)
    kv_p, kv_s, sts_p, sts_s = [], [], [], []
    for l in range(depth):
        W = _prep_layer(l, P)
        mod = _ada(c_all, W["w_ada"], W["b_ada"])
        mods = [mod[:, j * d:(j + 1) * d] for j in range(6)]
        mod_p = [m[:bp][:, None, :] for m in mods]
        mod_s = [jnp.repeat(m[bp:], ls, axis=0)[None] for m in mods]

        attend_p = lambda q, k, z, v: _moba_prompt(q, k, z, grp_p)
        hp, kv, st = _layer(hp, mod_p, W, st_p, tab_p, attend_p, grp_p)
        kv_p.append(kv)
        sts_p.append(st)

        prev = (state_mlstm_c[l], state_mlstm_n[l], state_mlstm_m[l], state_conv[l], state_rwkv[l],
                state_rwkv_shift[l], state_ffn_conv[l])
        attend_s = lambda q, k, z, v: _moba_sample(q, k, v, cache_k, cache_v, page_table, l, grp_s)
        hs, kv, st = _layer(hs, mod_s, W, prev, tab_s, attend_s, grp_s)
        kv_s.append(kv)
        sts_s.append(st)

    heads = lambda t, b_, l_: t.reshape(b_, l_, N_HEADS, HEAD_DIM)
    k_prompt = jnp.stack([heads(k, bp, lp) for k, _ in kv_p])
    v_prompt = jnp.stack([heads(v, bp, lp) for _, v in kv_p])
    k_sample = jnp.stack([heads(k, bs, ls) for k, _ in kv_s])
    v_sample = jnp.stack([heads(v, bs, ls) for _, v in kv_s])
    stack = lambda sts: [jnp.stack(t) for t in zip(*sts)]
    return (hp.reshape(bp, lp, d), hs.reshape(bs, ls, d), k_prompt, v_prompt, k_sample, v_sample,
            *stack(sts_p), *stack(sts_s))
```

```python
import functools
import math

import jax
import jax.numpy as jnp
from jax import lax
from jax.experimental import pallas as pl
from jax.experimental.pallas import tpu as pltpu

F32 = jnp.float32
BF16 = jnp.bfloat16
HIGHEST = lax.Precision.HIGHEST

D_MODEL = 1024
HEAD_DIM = 64
N_HEADS = 4
BRANCH = N_HEADS * HEAD_DIM
CONV_WIDTH = 3
MB_BLOCK = 256
MB_TOPK = 3
PAGE_SIZE = 128
ROPE_DIMS = HEAD_DIM // 4
ROPE_THETA = 500000.0
RW_LORA = 128
RW_COLS = 3 * BRANCH + RW_LORA
D_FF = 11 * D_MODEL // 4
MLSTM_CHUNK = 64
RWKV_CHUNK = 64
FFN_HALO = 16
SCAN_BATCH = 2
RWKV_BATCH = 2
NORM_EPS = 1e-6
GN_EPS = 64e-5
LANES = 128

Z_GATES = 0
Z_MLSTM = 4096
Z_SCONV = 5120
Z_MOBA = 5888
Z_MIF = 6656
Z_RWKV = 7168
Z_WIDTH = 8192

VMEM_LIMIT = 48 * 1024 * 1024


def _cparams(*sem):
    return pltpu.CompilerParams(dimension_semantics=sem, vmem_limit_bytes=VMEM_LIMIT)


def _sigmoid(x):
    return 1.0 / (1.0 + jnp.exp(-x))


def _head_blockdiag(n):
    r = lax.broadcasted_iota(jnp.int32, (n, n), 0) // HEAD_DIM
    c = lax.broadcasted_iota(jnp.int32, (n, n), 1) // HEAD_DIM
    return (r == c).astype(F32)


def _head_sum(x, bd):
    return jnp.dot(x, bd, precision=HIGHEST, preferred_element_type=F32)


def _dot_nt(a, b):
    return lax.dot_general(a, b, (((1,), (1,)), ((), ())), preferred_element_type=F32)


def _dot_tn(a, b):
    return lax.dot_general(a, b, (((0,), (0,)), ((), ())), preferred_element_type=F32)


def _pick_tile(n, cap):
    t = cap
    while n % t:
        t //= 2
    return t


def _ada_kernel(c_ref, w_ref, b_ref, o_ref):
    c = c_ref[...]
    a = (c * _sigmoid(c)).astype(BF16)
    o_ref[...] = jnp.dot(a, w_ref[...], preferred_element_type=F32) + b_ref[...]


def _ada(c_all, w, b):
    m, d = c_all.shape
    n = w.shape[1]
    tn = 1536
    return pl.pallas_call(
        _ada_kernel,
        grid=(n // tn,),
        in_specs=[pl.BlockSpec((m, d), lambda j: (0, 0)),
                  pl.BlockSpec((d, tn), lambda j: (0, j)),
                  pl.BlockSpec((1, tn), lambda j: (0, j))],
        out_specs=pl.BlockSpec((m, tn), lambda j: (0, j)),
        out_shape=jax.ShapeDtypeStruct((m, n), F32),
        compiler_params=_cparams("parallel"),
        name="ada",
    )(c_all, w, b)


def _nmm_kernel(x_ref, g_ref, sc_ref, sh_ref, w_ref, o_ref, h_sc):
    @pl.when(pl.program_id(1) == 0)
    def _():
        x = x_ref[...]
        y = x * lax.rsqrt(jnp.mean(x * x, axis=-1, keepdims=True) + NORM_EPS) * g_ref[...]
        h_sc[...] = (y * (1.0 + sc_ref[...]) + sh_ref[...]).astype(BF16)

    o_ref[...] = jnp.dot(h_sc[...], w_ref[...], preferred_element_type=F32)


def _norm_mod_matmul(x, g, sc, sh, w, grp, tn):
    n, d = x.shape
    nout = w.shape[1]
    tm, tpg, r = grp["tm"], grp["tpg"], sc.shape[1]
    return pl.pallas_call(
        _nmm_kernel,
        grid=(n // tm, nout // tn),
        in_specs=[pl.BlockSpec((tm, d), lambda i, j: (i, 0)),
                  pl.BlockSpec((1, d), lambda i, j: (0, 0)),
                  pl.BlockSpec((None, r, d), lambda i, j: (i // tpg, 0, 0)),
                  pl.BlockSpec((None, r, d), lambda i, j: (i // tpg, 0, 0)),
                  pl.BlockSpec((d, tn), lambda i, j: (0, j))],
        out_specs=pl.BlockSpec((tm, tn), lambda i, j: (i, j)),
        out_shape=jax.ShapeDtypeStruct((n, nout), F32),
        scratch_shapes=[pltpu.VMEM((tm, d), BF16)],
        compiler_params=_cparams("parallel", "arbitrary"),
        name="norm_mod_matmul",
    )(x, g, sc, sh, w)


def _log_sigmoid(x):
    return jnp.minimum(x, 0.0) - jnp.log(1.0 + jnp.exp(-jnp.abs(x)))


def _mlstm_kernel(zq_ref, zif_ref, bias_ref, g_ref, c0_ref, n0_ref, m0_ref,
                  act_ref, c_out, n_out, m_out, c_sc, n_sc, m_sc, *, chunk, bb):
    t_ = chunk
    ci = pl.program_id(1)

    @pl.when(ci == 0)
    def _():
        c_sc[...] = c0_ref[...]
        n_sc[...] = n0_ref[...]
        m_sc[...] = m0_ref[...]

    row = lax.broadcasted_iota(jnp.int32, (t_, t_), 0)
    col = lax.broadcasted_iota(jnp.int32, (t_, t_), 1)
    causal = col <= row
    eye = row == col
    row1 = lax.broadcasted_iota(jnp.int32, (t_, 1), 0)
    chains = [(b, h) for b in range(bb) for h in range(N_HEADS)]
    idx = range(len(chains))
    each = lambda f, *xs: [f(*a) for a in zip(*xs)]
    mmf = lambda a, b: jnp.dot(a, b, preferred_element_type=F32)
    gates = [zif_ref[b] + bias_ref[...] for b in range(bb)]
    logf = each(_log_sigmoid, gates)
    part = lambda j: [zq_ref[b, :, j * BRANCH + h * HEAD_DIM:j * BRANCH + (h + 1) * HEAD_DIM] for b, h in chains]
    q, k, v, o = part(0), [x * (HEAD_DIM ** -0.5) for x in part(1)], part(2), part(3)
    li_col = [gates[b][:, h:h + 1] for b, h in chains]
    lf_col = [logf[b][:, N_HEADS + h:N_HEADS + h + 1] for b, h in chains]
    li_row = each(lambda x: jnp.sum(jnp.where(eye, x, 0.0), axis=0, keepdims=True), li_col)
    lf_row = each(lambda x: jnp.sum(jnp.where(eye, x, 0.0), axis=0, keepdims=True), lf_col)
    b_col = each(lambda x: jnp.sum(jnp.where(causal, x, 0.0), axis=1, keepdims=True), lf_row)
    b_row = each(lambda x: jnp.sum(jnp.where(row <= col, x, 0.0), axis=0, keepdims=True), lf_col)
    m_prev = [m_sc[b, h][:, 0:1] for b, h in chains]
    d = each(lambda bc, br, lr: jnp.where(causal, bc - br + lr, -jnp.inf), b_col, b_row, li_row)
    inter = each(lambda bc, m: bc + m, b_col, m_prev)
    mt = each(lambda x, y: jnp.maximum(jnp.max(x, axis=1, keepdims=True), y), d, inter)
    qb, kb, vb = (each(lambda x: x.astype(BF16), t) for t in (q, k, v))
    w = each(lambda x, m, a, b: jnp.exp(x - m) * _dot_nt(a, b), d, mt, qb, kb)
    a_int = each(lambda x, m: jnp.exp(x - m), inter, mt)
    c = [c_sc[b, h] for b, h in chains]
    n = [n_sc[b, h] for b, h in chains]
    num = each(lambda w_, v_, a, q_, c_: mmf(w_.astype(BF16), v_) + a * mmf(q_, c_.astype(BF16)), w, vb, a_int, qb, c)
    den = each(lambda w_, a, q_, n_: jnp.sum(w_, axis=1, keepdims=True) + a * jnp.sum(q_ * n_, axis=1, keepdims=True),
               w, a_int, q, n)
    hh = each(lambda x, y, m: x / jnp.maximum(jnp.abs(y), jnp.exp(-m)), num, den, mt)
    hn = each(lambda x: x * lax.rsqrt(jnp.mean(x * x, axis=-1, keepdims=True) + NORM_EPS), hh)
    b_end = each(lambda x: jnp.sum(jnp.where(row1 == t_ - 1, x, 0.0), axis=0, keepdims=True), b_col)
    g_col = each(lambda e, bc, l: e - bc + l, b_end, b_col, li_col)
    m_new = each(lambda e, m, g_: jnp.maximum(e + m, jnp.max(g_, axis=0, keepdims=True)), b_end, m_prev, g_col)
    wk = each(lambda g_, m, k_: jnp.exp(g_ - m) * k_, g_col, m_new, k)
    decay = each(lambda e, m, mn: jnp.exp(e + m - mn), b_end, m_prev, m_new)
    c_new = each(lambda dc, c_, wk_, v_: dc * c_ + _dot_tn(wk_.astype(BF16), v_), decay, c, wk, vb)
    for i in idx:
        b, h = chains[i]
        lo = h * HEAD_DIM
        act_ref[b, :, lo:lo + HEAD_DIM] = hn[i] * g_ref[:, lo:lo + HEAD_DIM] * _sigmoid(o[i])
        c_sc[b, h] = c_new[i]
        n_sc[b, h] = decay[i] * n[i] + jnp.sum(wk[i], axis=0, keepdims=True)
        m_sc[b, h] = jnp.broadcast_to(m_new[i], (1, LANES))

    @pl.when(ci == pl.num_programs(1) - 1)
    def _():
        c_out[...] = c_sc[...]
        n_out[...] = n_sc[...]
        m_out[...] = m_sc[...]


def _mlstm(z, bias_if, g, c0, n0, m0, grp):
    b_, l_ = grp["B"], grp["L"]
    t_ = min(MLSTM_CHUNK, l_)
    nc = l_ // t_
    h_ = N_HEADS
    bb = _pick_tile(b_, SCAN_BATCH)
    n0 = n0.reshape(b_, h_, 1, HEAD_DIM)
    m0 = jnp.broadcast_to(m0.reshape(b_, h_, 1, 1), (b_, h_, 1, LANES))
    z3 = z.reshape(b_, l_, Z_WIDTH)
    state_spec = lambda shp: pl.BlockSpec((bb,) + shp, lambda b, c: (b, 0, 0, 0))
    act, c1, n1, m1 = pl.pallas_call(
        functools.partial(_mlstm_kernel, chunk=t_, bb=bb),
        grid=(b_ // bb, nc),
        in_specs=[pl.BlockSpec((bb, t_, 4 * BRANCH), lambda b, c: (b, c, Z_MLSTM // (4 * BRANCH))),
                  pl.BlockSpec((bb, t_, LANES), lambda b, c: (b, c, Z_MIF // LANES)),
                  pl.BlockSpec((1, LANES), lambda b, c: (0, 0)),
                  pl.BlockSpec((1, BRANCH), lambda b, c: (0, 0)),
                  state_spec((h_, HEAD_DIM, HEAD_DIM)),
                  state_spec((h_, 1, HEAD_DIM)),
                  state_spec((h_, 1, LANES))],
        out_specs=[pl.BlockSpec((bb, t_, BRANCH), lambda b, c: (b, c, 0)),
                   state_spec((h_, HEAD_DIM, HEAD_DIM)),
                   state_spec((h_, 1, HEAD_DIM)),
                   state_spec((h_, 1, LANES))],
        out_shape=[jax.ShapeDtypeStruct((b_, l_, BRANCH), F32),
                   jax.ShapeDtypeStruct((b_, h_, HEAD_DIM, HEAD_DIM), F32),
                   jax.ShapeDtypeStruct((b_, h_, 1, HEAD_DIM), F32),
                   jax.ShapeDtypeStruct((b_, h_, 1, LANES), F32)],
        scratch_shapes=[pltpu.VMEM((bb, h_, HEAD_DIM, HEAD_DIM), F32),
                        pltpu.VMEM((bb, h_, 1, HEAD_DIM), F32),
                        pltpu.VMEM((bb, h_, 1, LANES), F32)],
        compiler_params=_cparams("parallel", "arbitrary"),
        name="mlstm",
    )(z3, z3, bias_if, g, c0, n0, m0)
    return act.reshape(b_ * l_, BRANCH), c1, n1.reshape(b_, h_, HEAD_DIM), m1[:, :, 0, 0]


def _conv3(p, prev2, prev1, w_ref):
    row = lax.broadcasted_iota(jnp.int32, p.shape, 0)
    p1 = jnp.where(row == 0, prev1, pltpu.roll(p, 1, axis=0))
    p2 = jnp.where(row == 0, prev2, jnp.where(row == 1, prev1, pltpu.roll(p, 2, axis=0)))
    return w_ref[0:1, :] * p2 + w_ref[1:2, :] * p1 + w_ref[2:3, :] * p


def _sconv_kernel(sb_ref, sc_ref, sh_ref, w_ref, buf_ref, out_ref, st_ref, carry, *, tm):
    @pl.when(pl.program_id(1) == 0)
    def _():
        carry[...] = buf_ref[0]

    p = sc_ref[...] * sh_ref[...]
    u = _conv3(p, carry[0:1, :], carry[1:2, :], w_ref)
    out_ref[...] = sb_ref[...] * u
    new = sc_ref[tm - 2:tm, :] * sh_ref[tm - 2:tm, :]
    carry[...] = new
    st_ref[0] = new


def _sconv(z, w, buf, grp):
    b_, l_ = grp["B"], grp["L"]
    tm = _pick_tile(l_, 512)
    nt = l_ // tm
    cb = Z_SCONV // BRANCH
    zspec = lambda j: pl.BlockSpec((tm, BRANCH), lambda b, i: (b * nt + i, cb + j))
    return pl.pallas_call(
        functools.partial(_sconv_kernel, tm=tm),
        grid=(b_, nt),
        in_specs=[zspec(0), zspec(1), zspec(2),
                  pl.BlockSpec((CONV_WIDTH, BRANCH), lambda b, i: (0, 0)),
                  pl.BlockSpec((1, CONV_WIDTH - 1, BRANCH), lambda b, i: (b, 0, 0))],
        out_specs=[pl.BlockSpec((tm, BRANCH), lambda b, i: (b * nt + i, 0)),
                   pl.BlockSpec((1, CONV_WIDTH - 1, BRANCH), lambda b, i: (b, 0, 0))],
        out_shape=[jax.ShapeDtypeStruct((b_ * l_, BRANCH), F32),
                   jax.ShapeDtypeStruct((b_, CONV_WIDTH - 1, BRANCH), F32)],
        scratch_shapes=[pltpu.VMEM((CONV_WIDTH - 1, BRANCH), F32)],
        compiler_params=_cparams("parallel", "arbitrary"),
        name="sconv",
    )(z, z, z, w, buf)


def _qk_prep_kernel(q_ref, k_ref, qg_ref, kg_ref, cos_ref, sa_ref, sb_ref, qo_ref, ko_ref):
    bd = _head_blockdiag(BRANCH)
    cos, sa, sb = cos_ref[...], sa_ref[...], sb_ref[...]
    half = ROPE_DIMS // 2

    def prep(x, g):
        y = x * lax.rsqrt(_head_sum(x * x, bd) * (1.0 / HEAD_DIM) + NORM_EPS) * g
        return y * cos + pltpu.roll(y, BRANCH - half, axis=1) * sa + pltpu.roll(y, half, axis=1) * sb

    qo_ref[...] = prep(q_ref[...], qg_ref[...])
    ko_ref[...] = prep(k_ref[...], kg_ref[...])


def _rope_tables(pos):
    half = ROPE_DIMS // 2
    inv = jnp.exp(-math.log(ROPE_THETA) * jnp.arange(0, ROPE_DIMS, 2, dtype=F32) / ROPE_DIMS)
    ang = pos.astype(F32)[:, None] * inv[None, :]
    cos, sin = jnp.cos(ang), jnp.sin(ang)
    n = pos.shape[0]
    rest = HEAD_DIM - ROPE_DIMS
    c_h = jnp.concatenate([cos, cos, jnp.ones((n, rest), F32)], axis=1)
    sa_h = jnp.concatenate([-sin, jnp.zeros((n, half + rest), F32)], axis=1)
    sb_h = jnp.concatenate([jnp.zeros((n, half), F32), sin, jnp.zeros((n, rest), F32)], axis=1)
    tile = lambda t: jnp.tile(t, (1, N_HEADS))
    return tile(c_h), tile(sa_h), tile(sb_h)


def _qk_prep(z, qg, kg, tables, grp):
    n = grp["B"] * grp["L"]
    ltab = tables[0].shape[0]
    tm = _pick_tile(ltab, 512)
    npos = ltab // tm
    cb = Z_MOBA // BRANCH
    tspec = pl.BlockSpec((tm, BRANCH), lambda i: (i % npos, 0))
    gspec = pl.BlockSpec((1, BRANCH), lambda i: (0, 0))
    ospec = pl.BlockSpec((tm, BRANCH), lambda i: (i, 0))
    return pl.pallas_call(
        _qk_prep_kernel,
        grid=(n // tm,),
        in_specs=[pl.BlockSpec((tm, BRANCH), lambda i: (i, cb)),
                  pl.BlockSpec((tm, BRANCH), lambda i: (i, cb + 1)),
                  gspec, gspec, tspec, tspec, tspec],
        out_specs=[ospec, ospec],
        out_shape=[jax.ShapeDtypeStruct((n, BRANCH), F32)] * 2,
        compiler_params=_cparams("parallel"),
        name="qk_prep",
    )(z, z, qg, kg, *tables)


def _moba_prompt_kernel(q_ref, k_ref, v_ref, o_ref, kmean, kx, vx, *, nb):
    qi = pl.program_id(1)
    tq = MB_BLOCK
    half = LANES // 2
    masked = float(jnp.finfo(BF16).min)
    lane = lax.broadcasted_iota(jnp.int32, (tq, LANES), 1)
    low = lane < half

    @pl.when(qi == 0)
    def _():
        kmean[...] = jnp.zeros_like(kmean)
        for n in range(nb):
            rows = slice(n * MB_BLOCK, (n + 1) * MB_BLOCK)
            kmean[n:n + 1, :] = jnp.mean(k_ref[rows, :], axis=0, keepdims=True)
            onehot = (lane == half + n).astype(F32)
            for h in range(N_HEADS):
                pair = slice((h // 2) * LANES, (h // 2 + 1) * LANES)
                k2, v2 = k_ref[rows, pair], v_ref[rows, pair]
                if h % 2:
                    k2, v2 = pltpu.roll(k2, half, axis=1), pltpu.roll(v2, half, axis=1)
                kx[h, rows, :] = jnp.where(low, k2, onehot).astype(BF16)
                vx[h, rows, :] = jnp.where(low, v2, 1.0).astype(BF16)

    row = lax.broadcasted_iota(jnp.int32, (tq, tq), 0)
    col = lax.broadcasted_iota(jnp.int32, (tq, tq), 1)
    scale = HEAD_DIM ** -0.5
    own0 = pl.multiple_of(qi * MB_BLOCK, MB_BLOCK)
    heads = range(N_HEADS)
    sub = 8
    assert nb <= sub
    blk = lax.broadcasted_iota(jnp.int32, (sub, tq), 0)
    place = (lax.broadcasted_iota(jnp.int32, (sub, LANES), 1)
             == lax.broadcasted_iota(jnp.int32, (sub, LANES), 0) + half).astype(BF16)
    qx = []
    for h in heads:
        hs = slice(h * HEAD_DIM, (h + 1) * HEAD_DIM)
        s = lax.dot_general(kmean[0:sub, hs], q_ref[:, hs], (((1,), (1,)), ((), ())),
                            precision=HIGHEST, preferred_element_type=F32)
        valid = blk < qi
        s = jnp.where(valid, s, -jnp.inf)
        rank = jnp.zeros((sub, tq), jnp.int32)
        for m in range(nb):
            sm = s[m:m + 1, :]
            rank += ((sm > s) | ((sm == s) & (m < blk))).astype(jnp.int32)
        keep = ((valid & (rank < MB_TOPK)) | (blk == qi)).astype(BF16)
        keep_q = _dot_tn(keep, place)
        q2 = q_ref[:, (h // 2) * LANES:(h // 2 + 1) * LANES] * scale
        if h % 2:
            q2 = pltpu.roll(q2, half, axis=1)
        qx.append(jnp.where(low, q2, jnp.where(keep_q > 0.5, 0.0, masked)).astype(BF16))
    sc = [jnp.where(col <= row, _dot_nt(qx[h], kx[h, pl.ds(own0, MB_BLOCK), :]), -jnp.inf) for h in heads]
    m_i = [jnp.max(sc[h], axis=1, keepdims=True) for h in heads]
    p = [jnp.exp(sc[h] - m_i[h]).astype(BF16) for h in heads]
    acc = [jnp.dot(p[h], vx[h, pl.ds(own0, MB_BLOCK), :], preferred_element_type=F32) for h in heads]

    def body(n, carry):
        m_i, acc = carry
        k0 = pl.multiple_of(n * MB_BLOCK, MB_BLOCK)
        sc = [_dot_nt(qx[h], kx[h, pl.ds(k0, MB_BLOCK), :]) for h in heads]
        m_new = [jnp.maximum(m_i[h], jnp.max(sc[h], axis=1, keepdims=True)) for h in heads]
        p = [jnp.exp(sc[h] - m_new[h]).astype(BF16) for h in heads]
        pv = [jnp.dot(p[h], vx[h, pl.ds(k0, MB_BLOCK), :], preferred_element_type=F32) for h in heads]
        return m_new, [jnp.exp(m_i[h] - m_new[h]) * acc[h] + pv[h] for h in heads]

    m_i, acc = lax.fori_loop(0, qi, body, (m_i, acc))
    for h in heads:
        o_ref[:, h * HEAD_DIM:(h + 1) * HEAD_DIM] = (acc[h] / pltpu.roll(acc[h], half, axis=1))[:, :half]


def _moba_prompt(q, k, z, grp):
    b_, l_ = grp["B"], grp["L"]
    assert l_ % MB_BLOCK == 0 and l_ // MB_BLOCK <= LANES // 2
    nb = l_ // MB_BLOCK
    return pl.pallas_call(
        functools.partial(_moba_prompt_kernel, nb=nb),
        grid=(b_, nb),
        in_specs=[pl.BlockSpec((MB_BLOCK, BRANCH), lambda b, i: (b * nb + i, 0)),
                  pl.BlockSpec((l_, BRANCH), lambda b, i: (b, 0)),
                  pl.BlockSpec((l_, BRANCH), lambda b, i: (b, Z_MOBA // BRANCH + 2))],
        out_specs=pl.BlockSpec((MB_BLOCK, BRANCH), lambda b, i: (b * nb + i, 0)),
        out_shape=jax.ShapeDtypeStruct((b_ * l_, BRANCH), F32),
        scratch_shapes=[pltpu.VMEM((LANES, BRANCH), F32),
                        pltpu.VMEM((N_HEADS, l_, LANES), BF16),
                        pltpu.VMEM((N_HEADS, l_, LANES), BF16)],
        compiler_params=_cparams("parallel", "arbitrary"),
        name="moba_prompt",
    )(q, k, z)


def _moba_sample_kernel(pt_ref, q_ref, kn_ref, vn_ref, *refs, nblk, dec, gsz):
    ppb = MB_BLOCK // PAGE_SIZE
    k_refs, v_refs = refs[:ppb * gsz], refs[ppb * gsz:2 * ppb * gsz]
    o_ref, kmean_s, m_s, l_s, o_s = refs[2 * ppb * gsz:]
    step = pl.program_id(1)
    nq = N_HEADS * dec
    scale = HEAD_DIM ** -0.5
    lane = lax.broadcasted_iota(jnp.int32, (nq, LANES), 1)
    lane_c = lax.broadcasted_iota(jnp.int32, (BRANCH, LANES), 1)

    @pl.when(step == 0)
    def _():
        kmean_s[...] = jnp.zeros_like(kmean_s)
        m_s[...] = jnp.full_like(m_s, -jnp.inf)
        l_s[...] = jnp.zeros_like(l_s)

    row_head = jnp.concatenate([jnp.full((dec, BRANCH), h, jnp.int32) for h in range(N_HEADS)], axis=0)
    lane_head = lax.broadcasted_iota(jnp.int32, (nq, BRANCH), 1) // HEAD_DIM
    qbd = jnp.where(row_head == lane_head, jnp.concatenate([q_ref[...]] * N_HEADS, axis=0), 0.0)
    qbd_b = qbd.astype(BF16)
    kmean, m_all, l_all = kmean_s[...], m_s[...], l_s[...]
    blocks = range(gsz)
    kt = [jnp.concatenate([k_refs[ppb * g + j][...] for j in range(ppb)], axis=1) for g in blocks]
    s = [jnp.dot(qbd_b, kt[g].astype(BF16), preferred_element_type=F32) * scale for g in blocks]
    mx = [jnp.max(s[g], axis=1, keepdims=True) for g in blocks]
    p = [jnp.exp(s[g] - mx[g]) for g in blocks]
    vt = [jnp.concatenate([v_refs[ppb * g + j][...] for j in range(ppb)], axis=1) for g in blocks]
    o = [_dot_nt(p[g].astype(BF16), vt[g].astype(BF16)) for g in blocks]
    for g in blocks:
        blk = step * gsz + g
        o_s[blk] = o[g]
        kmean = jnp.where(lane_c == blk, jnp.mean(kt[g], axis=1, keepdims=True), kmean)
        m_all = jnp.where(lane == blk, mx[g], m_all)
        l_all = jnp.where(lane == blk, jnp.sum(p[g], axis=1, keepdims=True), l_all)
    kmean_s[...] = kmean
    m_s[...] = m_all
    l_s[...] = l_all

    @pl.when(step == pl.num_programs(1) - 1)
    def _():
        zpad = jnp.zeros((LANES - dec, BRANCH), F32)
        knew = jnp.concatenate([kn_ref[...], zpad], axis=0)
        vnew = jnp.concatenate([vn_ref[...], zpad], axis=0)
        rowq = jnp.concatenate([lax.broadcasted_iota(jnp.int32, (dec, LANES), 0)] * N_HEADS, axis=0)
        lane_f = lane.astype(F32)
        s = jnp.dot(qbd, kmean_s[...], precision=HIGHEST, preferred_element_type=F32)
        s = jnp.where(lane < nblk, s, -jnp.inf)
        sel = jnp.zeros((nq, LANES), jnp.bool_)
        for _ in range(MB_TOPK):
            mxv = jnp.max(s, axis=1, keepdims=True)
            idx = jnp.min(jnp.where(s == mxv, lane_f, 2.0 * LANES), axis=1, keepdims=True)
            pick = lane_f == idx
            sel = sel | pick
            s = jnp.where(pick, -jnp.inf, s)
        sel = sel & (lane < nblk)
        s_own = jnp.where(lane <= rowq, _dot_nt(qbd_b, knew.astype(BF16)) * scale, -jnp.inf)
        m_own = jnp.max(s_own, axis=1, keepdims=True)
        p_own = jnp.exp(s_own - m_own)
        l_own = jnp.sum(p_own, axis=1, keepdims=True)
        o_own = jnp.dot(p_own.astype(BF16), vnew.astype(BF16), preferred_element_type=F32)
        mb = m_s[...]
        mtot = jnp.maximum(jnp.max(jnp.where(sel, mb, -jnp.inf), axis=1, keepdims=True), m_own)
        wgt = jnp.where(sel, jnp.exp(mb - mtot), 0.0)
        a_own = jnp.exp(m_own - mtot)
        ltot = jnp.sum(wgt * l_s[...], axis=1, keepdims=True) + l_own * a_own
        acc = o_own * a_own
        for j in range(nblk):
            acc = acc + wgt[:, j:j + 1] * o_s[j]
        res = acc / ltot
        for h in range(N_HEADS):
            hs = slice(h * HEAD_DIM, (h + 1) * HEAD_DIM)
            o_ref[:, hs] = res[h * dec:(h + 1) * dec, hs]


def _moba_sample(q, k, v, pool_k, pool_v, page_table, layer, grp):
    b_, dec = grp["B"], grp["L"]
    n_pages = page_table.shape[1]
    ppb = MB_BLOCK // PAGE_SIZE
    assert n_pages % ppb == 0
    nblk = n_pages // ppb
    assert MB_TOPK <= nblk <= LANES and dec <= LANES and dec % 8 == 0
    gsz = max(g for g in range(1, 9) if nblk % g == 0)
    as_pages = lambda t: t.transpose(0, 1, 3, 4, 2).reshape(t.shape[0], t.shape[1], BRANCH, PAGE_SIZE)
    pool_k, pool_v = as_pages(pool_k), as_pages(pool_v)
    pt = page_table.reshape(-1)

    def page(j):
        return pl.BlockSpec((None, None, BRANCH, PAGE_SIZE),
                            lambda b, s, pt: (layer, pt[b * n_pages + s * gsz * ppb + j], 0, 0))

    pages = [page(j) for j in range(gsz * ppb)]
    new = pl.BlockSpec((dec, BRANCH), lambda b, s, pt: (b, 0))
    nq = N_HEADS * dec
    return pl.pallas_call(
        functools.partial(_moba_sample_kernel, nblk=nblk, dec=dec, gsz=gsz),
        grid_spec=pltpu.PrefetchScalarGridSpec(
            num_scalar_prefetch=1,
            grid=(b_, nblk // gsz),
            in_specs=[new, new, new] + pages + pages,
            out_specs=new,
            scratch_shapes=[pltpu.VMEM((BRANCH, LANES), F32),
                            pltpu.VMEM((nq, LANES), F32),
                            pltpu.VMEM((nq, LANES), F32),
                            pltpu.VMEM((nblk, nq, BRANCH), F32)]),
        out_shape=jax.ShapeDtypeStruct((b_ * dec, BRANCH), F32),
        compiler_params=_cparams("parallel", "arbitrary"),
        name="moba_sample",
    )(pt, q, k, v, *([pool_k] * (gsz * ppb)), *([pool_v] * (gsz * ppb)))


def _rwkv_prep_kernel(z_ref, prev_ref, mu_ref, wlr_ref, w0_ref, a0_ref, kks_ref, kas_ref, rk_ref,
                      r_o, w_o, k_o, v_o, kk_o, kka_o, bonus_o, g_o, carry, *, tm):
    @pl.when(pl.program_id(1) == 0)
    def _():
        carry[...] = prev_ref[0]

    z = z_ref[...]
    row = lax.broadcasted_iota(jnp.int32, z.shape, 0)
    zs = jnp.where(row == 0, carry[...], pltpu.roll(z, 1, axis=0))
    carry[...] = z_ref[tm - 1:tm, :]
    zz = z + mu_ref[...] * (zs - z)
    r = zz[:, 0:BRANCH]
    k = zz[:, BRANCH:2 * BRANCH]
    v = zz[:, 2 * BRANCH:3 * BRANCH]
    lr = zz[:, 3 * BRANCH:3 * BRANCH + RW_LORA]
    lane = lax.broadcasted_iota(jnp.int32, lr.shape, 1)
    lr_in = jnp.where(lane < 32, jnp.tanh(lr), jnp.where(lane < 64, lr, _sigmoid(lr)))
    lo = jnp.dot(lr_in.astype(BF16), wlr_ref[...], preferred_element_type=F32)
    log_decay = -math.exp(-0.5) * _sigmoid(w0_ref[...] + lo[:, 0:BRANCH])
    a = _sigmoid(a0_ref[...] + lo[:, BRANCH:2 * BRANCH])
    bd = _head_blockdiag(BRANCH)
    kk = k * kks_ref[...]
    kk = kk * lax.rsqrt(jnp.maximum(_head_sum(kk * kk, bd), 1e-24))
    k2 = k * (1.0 + (a - 1.0) * kas_ref[...])
    r_o[...] = r
    w_o[...] = log_decay
    k_o[...] = k2
    v_o[...] = v
    kk_o[...] = kk
    kka_o[...] = kk * a
    bonus_o[...] = _head_sum(r * k2 * rk_ref[...], bd) * v
    g_o[...] = lo[:, 2 * BRANCH:3 * BRANCH]


def _rwkv_prep(z, prev, p, grp):
    b_, l_ = grp["B"], grp["L"]
    tm = _pick_tile(l_, 256)
    nt = l_ // tm
    zw = Z_WIDTH - Z_RWKV
    vec = pl.BlockSpec((1, BRANCH), lambda b, i: (0, 0))
    ospec = pl.BlockSpec((tm, BRANCH), lambda b, i: (b * nt + i, 0))
    return pl.pallas_call(
        functools.partial(_rwkv_prep_kernel, tm=tm),
        grid=(b_, nt),
        in_specs=[pl.BlockSpec((tm, zw), lambda b, i: (b * nt + i, Z_RWKV // zw)),
                  pl.BlockSpec((1, 1, zw), lambda b, i: (b, 0, 0)),
                  pl.BlockSpec((1, zw), lambda b, i: (0, 0)),
                  pl.BlockSpec((RW_LORA, 3 * BRANCH), lambda b, i: (0, 0)),
                  vec, vec, vec, vec, vec],
        out_specs=[ospec] * 8,
        out_shape=[jax.ShapeDtypeStruct((b_ * l_, BRANCH), F32)] * 8,
        scratch_shapes=[pltpu.VMEM((1, zw), F32)],
        compiler_params=_cparams("parallel", "arbitrary"),
        name="rwkv_prep",
    )(z, prev, p["mu"], p["wlr"], p["w0"], p["a0"], p["kk"], p["ka"], p["rk"])


def _rwkv_chunk_kernel(r_ref, lw_ref, k_ref, v_ref, kk_ref, kka_ref, h0_ref, y_ref, h_out, h_sc, *, chunk, bb):
    t_ = chunk
    ci = pl.program_id(1)

    @pl.when(ci == 0)
    def _():
        h_sc[...] = h0_ref[...]

    t4 = N_HEADS * t_
    rows = list(range(bb))
    each = lambda f, *xs: [f(*a) for a in zip(*xs)]
    tri = (lax.broadcasted_iota(jnp.int32, (t_, t_), 0) >= lax.broadcasted_iota(jnp.int32, (t_, t_), 1)).astype(F32)
    row1 = lax.broadcasted_iota(jnp.int32, (t_, 1), 0)
    head_mask = (jnp.concatenate([jnp.full((t_, BRANCH), h, jnp.int32) for h in range(N_HEADS)], axis=0)
                 == lax.broadcasted_iota(jnp.int32, (t4, BRANCH), 1) // HEAD_DIM)
    t_row = jnp.concatenate([lax.broadcasted_iota(jnp.int32, (t_, t4), 0)] * N_HEADS, axis=0)
    t_col = lax.broadcasted_iota(jnp.int32, (t4, t4), 1) & (t_ - 1)
    strict = t_row > t_col
    lower = t_row >= t_col
    stack = lambda x: jnp.where(head_mask, jnp.concatenate([x] * N_HEADS, axis=0), 0.0).astype(BF16)
    mm = lambda a, b: jnp.dot(a, b, preferred_element_type=F32)

    lw = [lw_ref[b] for b in rows]
    cum = each(lambda x: jnp.dot(tri, x, precision=HIGHEST, preferred_element_type=F32), lw)
    p_in = each(jnp.exp, cum)
    p_inv = each(lambda c: jnp.exp(-c), cum)
    p_end = each(lambda p: jnp.sum(jnp.where(row1 == t_ - 1, p, 0.0), axis=0, keepdims=True), p_in)
    kkm = [stack(kk_ref[b] * jnp.exp(cum[b] - lw[b])) for b in rows]
    rp = [stack(r_ref[b] * p_in[b]) for b in rows]
    kh = [k_ref[b] * p_inv[b] for b in rows]
    ah = [kka_ref[b] * p_inv[b] for b in rows]
    khe = [stack(kh[b] * p_end[b]) for b in rows]
    ahe = [stack(ah[b] * p_end[b]) for b in rows]
    khm, ahm = each(stack, kh), each(stack, ah)
    vm = [stack(v_ref[b]) for b in rows]
    lr = each(lambda a, b: jnp.concatenate([a, b], axis=0), kkm, rp)
    gk = each(_dot_nt, lr, khm)
    ga = each(_dot_nt, lr, ahm)
    ab_k = each(lambda g: jnp.concatenate([jnp.where(strict, g[:t4], 0.0), jnp.where(lower, g[t4:], 0.0)],
                                          axis=0).astype(BF16), gk)
    b_a = each(lambda g: jnp.where(lower, g[t4:], 0.0).astype(BF16), ga)
    e = each(lambda g: jnp.where(strict, -g[:t4], 0.0), ga)
    pw = e
    span = 2
    while span < t_:
        pw = each(lambda p: mm(p.astype(BF16), p.astype(BF16)), pw)
        e = each(lambda x, p: x + p + mm(x.astype(BF16), p.astype(BF16)), e, pw)
        span *= 2
    ht = [h_sc[b] for b in rows]
    x0 = each(lambda a, h: _dot_nt(a, h.astype(BF16)), lr, ht)
    kv = each(mm, ab_k, vm)
    x1 = each(lambda a, b: a[:t4] + b[:t4], x0, kv)
    u = each(lambda x, m: x + mm(m.astype(BF16), x.astype(BF16)), x1, e)
    ub = each(lambda x: x.astype(BF16), u)
    ybd = each(lambda a, b, m, x: a[t4:] + b[t4:] - mm(m, x), x0, kv, b_a, ub)
    for b in rows:
        y = ybd[b][0:t_]
        for h in range(1, N_HEADS):
            y = y + ybd[b][h * t_:(h + 1) * t_]
        y_ref[b] = y
        h_sc[b] = ht[b] * p_end[b] + _dot_tn(vm[b], khe[b]) - _dot_tn(ub[b], ahe[b])

    @pl.when(ci == pl.num_programs(1) - 1)
    def _():
        h_out[...] = h_sc[...]


def _rwkv_scan(r, lw, k, v, kk, kka, s0, grp):
    b_, l_ = grp["B"], grp["L"]
    chunk = _pick_tile(l_, RWKV_CHUNK)
    nc = l_ // chunk
    bb = _pick_tile(b_, RWKV_BATCH)
    seq = pl.BlockSpec((bb, chunk, BRANCH), lambda b, c: (b, c, 0))
    st = pl.BlockSpec((bb, BRANCH, BRANCH), lambda b, c: (b, 0, 0))
    as3 = lambda t: t.reshape(b_, l_, BRANCH)
    blocks = [slice(h * HEAD_DIM, (h + 1) * HEAD_DIM) for h in range(N_HEADS)]
    s0_bd = jnp.zeros((b_, BRANCH, BRANCH), F32)
    for h, hs in enumerate(blocks):
        s0_bd = s0_bd.at[:, hs, hs].set(s0[:, h])
    y, s1 = pl.pallas_call(
        functools.partial(_rwkv_chunk_kernel, chunk=chunk, bb=bb),
        grid=(b_ // bb, nc),
        in_specs=[seq] * 6 + [st],
        out_specs=[seq, st],
        out_shape=[jax.ShapeDtypeStruct((b_, l_, BRANCH), F32),
                   jax.ShapeDtypeStruct((b_, BRANCH, BRANCH), F32)],
        scratch_shapes=[pltpu.VMEM((bb, BRANCH, BRANCH), F32)],
        compiler_params=_cparams("parallel", "arbitrary"),
        name="rwkv_scan",
    )(as3(r), as3(lw), as3(k), as3(v), as3(kk), as3(kka), s0_bd)
    return y.reshape(b_ * l_, BRANCH), jnp.stack([s1[:, hs, hs] for hs in blocks], axis=1)


def _merge_kernel(ml_ref, sc_ref, mb_ref, y_ref, bonus_ref, g_ref, gn_ref, gates_ref, x_ref, gt_ref,
                  mp_ref, sp_ref, ap_ref, rp_ref, wo_ref, o_ref):
    bd = _head_blockdiag(BRANCH)
    y = y_ref[...]
    yc = y - _head_sum(y, bd) * (1.0 / HEAD_DIM)
    yn = yc * lax.rsqrt(_head_sum(yc * yc, bd) * (1.0 / HEAD_DIM) + GN_EPS)
    rw = (yn * gn_ref[...] + bonus_ref[...]) * g_ref[...]
    acts = (ml_ref[...], sc_ref[...], mb_ref[...], rw)
    projs = (mp_ref, sp_ref, ap_ref, rp_ref)
    merged = None
    for j in range(4):
        br = jnp.dot(acts[j].astype(BF16), projs[j][...], preferred_element_type=F32)
        term = _sigmoid(gates_ref[:, j * D_MODEL:(j + 1) * D_MODEL]) * br
        merged = term if merged is None else merged + term
    o_ref[...] = x_ref[...] + gt_ref[...] * jnp.dot(merged.astype(BF16), wo_ref[...], preferred_element_type=F32)


def _merge(ml, sc, mb, y, bonus, g, gn, z, x, gt, projs, w_out, grp):
    n, d = x.shape
    tm = min(256, grp["tm"])
    tpg = grp["tpg"] * (grp["tm"] // tm)
    r = gt.shape[1]
    if r > 1:
        assert r == grp["tm"] and tpg == grp["tm"] // tm
        gt = gt.reshape(r // tm, tm, d)
        gt_spec = pl.BlockSpec((None, tm, d), lambda i: (i, 0, 0))
    else:
        gt_spec = pl.BlockSpec((None, 1, d), lambda i: (i // tpg, 0, 0))
    act = pl.BlockSpec((tm, BRANCH), lambda i: (i, 0))
    proj = pl.BlockSpec((BRANCH, d), lambda i: (0, 0))
    return pl.pallas_call(
        _merge_kernel,
        grid=(n // tm,),
        in_specs=[act] * 6 + [pl.BlockSpec((1, BRANCH), lambda i: (0, 0)),
                              pl.BlockSpec((tm, 4 * d), lambda i: (i, 0)),
                              pl.BlockSpec((tm, d), lambda i: (i, 0)),
                              gt_spec, proj, proj, proj, proj,
                              pl.BlockSpec((d, d), lambda i: (0, 0))],
        out_specs=pl.BlockSpec((tm, d), lambda i: (i, 0)),
        out_shape=jax.ShapeDtypeStruct((n, d), F32),
        compiler_params=_cparams("parallel"),
        name="merge",
    )(ml, sc, mb, y, bonus, g, gn, z, x, gt, *projs, w_out)


def _ffn_act_kernel(ua_ref, ul_ref, w_ref, buf_ref, o_ref, carry, *, tm):
    @pl.when(pl.program_id(1) == 0)
    def _():
        carry[...] = buf_ref[0]

    ua = ua_ref[...]
    u = _conv3(ua, carry[0:1, :], carry[1:2, :], w_ref)
    carry[...] = ua_ref[tm - 2:tm, :]
    o_ref[...] = (u * _sigmoid(u) * ul_ref[...]).astype(BF16)


def _ffn_act(up, w, buf, grp):
    b_, l_ = grp["B"], grp["L"]
    tm = _pick_tile(l_, 256)
    nt = l_ // tm
    half = lambda j: pl.BlockSpec((tm, D_FF), lambda b, i: (b * nt + i, j))
    return pl.pallas_call(
        functools.partial(_ffn_act_kernel, tm=tm),
        grid=(b_, nt),
        in_specs=[half(0), half(1),
                  pl.BlockSpec((CONV_WIDTH, D_FF), lambda b, i: (0, 0)),
                  pl.BlockSpec((1, CONV_WIDTH - 1, D_FF), lambda b, i: (b, 0, 0))],
        out_specs=pl.BlockSpec((tm, D_FF), lambda b, i: (b * nt + i, 0)),
        out_shape=jax.ShapeDtypeStruct((b_ * l_, D_FF), BF16),
        scratch_shapes=[pltpu.VMEM((CONV_WIDTH - 1, D_FF), F32)],
        compiler_params=_cparams("parallel", "arbitrary"),
        name="ffn_act",
    )(up, up, w, buf)


def _ffn_up_kernel(x_ref, xp_ref, g_ref, sc_ref, sh_ref, wa_ref, wl_ref, cw_ref, buf_ref, o_ref, st_ref,
                   h_sc, hp_sc, prev_sc, tail_sc, *, tm, tpg):
    halo = FFN_HALO
    i = pl.program_id(0)

    @pl.when(pl.program_id(1) == 0)
    def _():
        def norm_mod(x):
            y = x * lax.rsqrt(jnp.mean(x * x, axis=-1, keepdims=True) + NORM_EPS) * g_ref[...]
            return (y * (1.0 + sc_ref[...]) + sh_ref[...]).astype(BF16)

        h_sc[...] = norm_mod(x_ref[...])
        hp_sc[...] = norm_mod(xp_ref[...])

    ua = jnp.dot(h_sc[...], wa_ref[...], preferred_element_type=F32)
    ul = jnp.dot(h_sc[...], wl_ref[...], preferred_element_type=F32)
    prev_sc[...] = jnp.dot(hp_sc[...], wa_ref[...], preferred_element_type=F32)
    tail_sc[...] = jnp.dot(h_sc[tm - halo:tm, :], wa_ref[...], preferred_element_type=F32)
    first = i % tpg == 0
    prev2 = jnp.where(first, buf_ref[0, 0:1, :], prev_sc[halo - 2:halo - 1, :])
    prev1 = jnp.where(first, buf_ref[0, 1:2, :], prev_sc[halo - 1:halo, :])
    u = _conv3(ua, prev2, prev1, cw_ref)
    o_ref[...] = (u * _sigmoid(u) * ul).astype(BF16)
    st_ref[0] = tail_sc[halo - 2:halo, :]


def _ffn_up(x, g, sc, sh, w, cw, buf, grp):
    n, d = x.shape
    l_ = grp["L"]
    tm = _pick_tile(l_, 512)
    tpg = l_ // tm
    tn = 1408
    nj = D_FF // tn
    halo = FFN_HALO
    assert sc.shape[1] == 1 and D_FF % tn == 0 and tm % halo == 0
    vec = lambda r: pl.BlockSpec((None, r, d), lambda i, j: (i // tpg, 0, 0))
    act, tails = pl.pallas_call(
        functools.partial(_ffn_up_kernel, tm=tm, tpg=tpg),
        grid=(n // tm, nj),
        in_specs=[pl.BlockSpec((tm, d), lambda i, j: (i, 0)),
                  pl.BlockSpec((halo, d), lambda i, j: (jnp.maximum(i * (tm // halo) - 1, 0), 0)),
                  pl.BlockSpec((1, d), lambda i, j: (0, 0)),
                  vec(1), vec(1),
                  pl.BlockSpec((d, tn), lambda i, j: (0, j)),
                  pl.BlockSpec((d, tn), lambda i, j: (0, nj + j)),
                  pl.BlockSpec((CONV_WIDTH, tn), lambda i, j: (0, j)),
                  pl.BlockSpec((1, CONV_WIDTH - 1, tn), lambda i, j: (i // tpg, 0, j))],
        out_specs=[pl.BlockSpec((tm, tn), lambda i, j: (i, j)),
                   pl.BlockSpec((1, CONV_WIDTH - 1, tn), lambda i, j: (i, 0, j))],
        out_shape=[jax.ShapeDtypeStruct((n, D_FF), BF16),
                   jax.ShapeDtypeStruct((n // tm, CONV_WIDTH - 1, D_FF), F32)],
        scratch_shapes=[pltpu.VMEM((tm, d), BF16), pltpu.VMEM((halo, d), BF16),
                        pltpu.VMEM((halo, tn), F32), pltpu.VMEM((halo, tn), F32)],
        compiler_params=_cparams("parallel", "arbitrary"),
        name="ffn_up",
    )(x, x, g, sc, sh, w, w, cw, buf)
    return act, tails[tpg - 1::tpg]


def _mm_res_kernel(a_ref, w_ref, x_ref, gt_ref, o_ref):
    o_ref[...] = x_ref[...] + gt_ref[...] * jnp.dot(a_ref[...], w_ref[...], preferred_element_type=F32)


def _mm_res(a, w, x, gt, grp):
    n, d = x.shape
    kdim = a.shape[1]
    tm, tpg, r = min(512, grp["tm"]), grp["tpg"] * (grp["tm"] // min(512, grp["tm"])), gt.shape[1]
    if r > 1:
        gt = gt.reshape(r // tm, tm, d)
        gt_spec = pl.BlockSpec((None, tm, d), lambda i: (i, 0, 0))
    else:
        gt_spec = pl.BlockSpec((None, 1, d), lambda i: (i // tpg, 0, 0))
    return pl.pallas_call(
        _mm_res_kernel,
        grid=(n // tm,),
        in_specs=[pl.BlockSpec((tm, kdim), lambda i: (i, 0)),
                  pl.BlockSpec((kdim, d), lambda i: (0, 0)),
                  pl.BlockSpec((tm, d), lambda i: (i, 0)),
                  gt_spec],
        out_specs=pl.BlockSpec((tm, d), lambda i: (i, 0)),
        out_shape=jax.ShapeDtypeStruct((n, d), F32),
        compiler_params=_cparams("parallel"),
        name="mm_res",
    )(a, w, x, gt)


def _prep_layer(l, P):
    d = D_MODEL
    w_in = P["w_in"][l]
    o_mif = 4 * BRANCH
    o_sb = o_mif + 2 * N_HEADS
    o_aq = o_sb + 3 * BRANCH
    o_rw = o_aq + 3 * BRANCH
    o_g = o_rw + RW_COLS
    w_in_p = jnp.concatenate([
        w_in[:, o_g:o_g + 4 * d], w_in[:, 0:o_mif], w_in[:, o_sb:o_aq], w_in[:, o_aq:o_rw],
        w_in[:, o_mif:o_sb], jnp.zeros((d, Z_RWKV - Z_MIF - 2 * N_HEADS), F32),
        w_in[:, o_rw:o_g], jnp.zeros((d, Z_WIDTH - Z_RWKV - RW_COLS), F32)], axis=1).astype(BF16)
    bias_if = jnp.concatenate([P["m_bi"][l], P["m_bf"][l], jnp.zeros((LANES - 2 * N_HEADS,), F32)])[None, :]
    zw = Z_WIDTH - Z_RWKV
    wlr = jnp.zeros((RW_LORA, 3 * BRANCH), F32)
    wlr = wlr.at[0:32, 0:BRANCH].set(P["r_wB"][l])
    wlr = wlr.at[32:64, BRANCH:2 * BRANCH].set(P["r_aB"][l])
    wlr = wlr.at[64:128, 2 * BRANCH:].set(P["r_gB"][l])
    row = lambda t: t[None, :]
    rw = dict(mu=jnp.pad(P["r_mu"][l], (0, zw - RW_COLS))[None, :], wlr=wlr.astype(BF16),
              w0=row(P["r_w0"][l]), a0=row(P["r_a0"][l]), kk=row(P["r_kk"][l]), ka=row(P["r_ka"][l]),
              rk=row(P["r_rk"][l]))
    bf = lambda name: P[name][l].astype(BF16)
    return dict(
        w_ada=bf("w_ada"), b_ada=row(P["b_ada"][l]), w_in=w_in_p, bias_if=bias_if,
        norm_mix_g=row(P["norm_mix_g"][l]), norm_ffn_g=row(P["norm_ffn_g"][l]),
        m_norm_g=row(P["m_norm_g"][l]), s_conv=P["s_conv"][l],
        qg=row(jnp.tile(P["a_qnorm"][l], N_HEADS)), kg=row(jnp.tile(P["a_knorm"][l], N_HEADS)),
        rw=rw, r_norm_g=row(P["r_norm_g"][l]),
        projs=(bf("m_proj"), bf("s_proj"), bf("a_proj"), bf("r_proj")), w_out=bf("w_out"),
        f_up=bf("f_up"), f_conv=P["f_conv"][l], f_down=bf("f_down"))


def _layer(x, mod, W, state, tables, attend, grp):
    mc, mn, mm, sbuf, rs, rshift, fbuf = state
    b_, l_ = grp["B"], grp["L"]
    sh_m, sc_m, gt_m, sh_f, sc_f, gt_f = mod
    z = _norm_mod_matmul(x, W["norm_mix_g"], sc_m, sh_m, W["w_in"], grp, 2048)
    ml, mc, mn, mm = _mlstm(z, W["bias_if"], W["m_norm_g"], mc, mn, mm, grp)
    sc, sbuf = _sconv(z, W["s_conv"], sbuf, grp)
    q, k = _qk_prep(z, W["qg"], W["kg"], tables, grp)
    v = z[:, Z_MOBA + 2 * BRANCH:Z_MOBA + 3 * BRANCH]
    mb = attend(q, k, z, v)
    zw = Z_WIDTH - Z_RWKV
    prev = jnp.pad(rshift, ((0, 0), (0, zw - RW_COLS)))[:, None, :]
    r, w, k2, vv, kk, kka, bonus, g = _rwkv_prep(z, prev, W["rw"], grp)
    y, rs_t = _rwkv_scan(r, w, k2, vv, kk, kka, rs, grp)
    rshift = z.reshape(b_, l_, Z_WIDTH)[:, -1, Z_RWKV:Z_RWKV + RW_COLS]
    x = _merge(ml, sc, mb, y, bonus, g, W["r_norm_g"], z, x, gt_m, W["projs"], W["w_out"], grp)
    if sc_f.shape[1] == 1 and l_ % FFN_HALO == 0:
        act, fbuf = _ffn_up(x, W["norm_ffn_g"], sc_f, sh_f, W["f_up"], W["f_conv"], fbuf, grp)
    else:
        up = _norm_mod_matmul(x, W["norm_ffn_g"], sc_f, sh_f, W["f_up"], grp, 1408)
        act = _ffn_act(up, W["f_conv"], fbuf, grp)
        fbuf = up.reshape(b_, l_, 2 * D_FF)[:, l_ - (CONV_WIDTH - 1):, :D_FF]
    x = _mm_res(act, W["f_down"], x, gt_f, grp)
    return x, (k, v), (mc, mn, mm, sbuf, rs_t, rshift, fbuf)


def kernel(x_prompt, x_sample, c_prompt, c_sample, cache_k, cache_v, page_table, state_mlstm_c, state_mlstm_n, state_mlstm_m, state_conv, state_rwkv, state_rwkv_shift, state_ffn_conv, norm_mix_g, norm_ffn_g, w_ada, b_ada, w_in, m_bi, m_bf, m_norm_g, m_proj, s_conv, s_proj, a_qnorm, a_knorm, a_proj, r_mu, r_w0, r_wB, r_a0, r_aB, r_gB, r_kk, r_ka, r_rk, r_norm_g, r_proj, w_out, f_up, f_conv, f_down):
    P = dict(norm_mix_g=norm_mix_g, norm_ffn_g=norm_ffn_g, w_ada=w_ada, b_ada=b_ada, w_in=w_in,
             m_bi=m_bi, m_bf=m_bf, m_norm_g=m_norm_g, m_proj=m_proj, s_conv=s_conv, s_proj=s_proj,
             a_qnorm=a_qnorm, a_knorm=a_knorm, a_proj=a_proj, r_mu=r_mu, r_w0=r_w0, r_wB=r_wB,
             r_a0=r_a0, r_aB=r_aB, r_gB=r_gB, r_kk=r_kk, r_ka=r_ka, r_rk=r_rk, r_norm_g=r_norm_g,
             r_proj=r_proj, w_out=w_out, f_up=f_up, f_conv=f_conv, f_down=f_down)
    depth = w_in.shape[0]
    bp, lp, d = x_prompt.shape
    bs, ls, _ = x_sample.shape
    n_s = bs * ls
    past = page_table.shape[1] * PAGE_SIZE
    tm_p = _pick_tile(lp, 1024)
    grp_p = dict(B=bp, L=lp, tm=tm_p, tpg=lp // tm_p)
    grp_s = dict(B=bs, L=ls, tm=n_s, tpg=1)
    assert n_s % 8 == 0 and n_s <= 1024

    zeros = lambda *s: jnp.zeros(s, F32)
    st_p = (zeros(bp, N_HEADS, HEAD_DIM, HEAD_DIM), zeros(bp, N_HEADS, HEAD_DIM), zeros(bp, N_HEADS),
            zeros(bp, CONV_WIDTH - 1, BRANCH), zeros(bp, N_HEADS, HEAD_DIM, HEAD_DIM), zeros(bp, RW_COLS),
            zeros(bp, CONV_WIDTH - 1, D_FF))
    tab_p = _rope_tables(jnp.arange(lp, dtype=jnp.int32))
    tab_s = tuple(jnp.tile(t, (bs, 1)) for t in _rope_tables(past + jnp.arange(ls, dtype=jnp.int32)))

    c_all = jnp.concatenate([c_prompt, c_sample], axis=0)
    hp = x_prompt.reshape(bp * lp, d)
    hs = x_sample.reshape(n_s, d)
    kv_p, kv_s, sts_p, sts_s = [], [], [], []
    for l in range(depth):
        W = _prep_layer(l, P)
        mod = _ada(c_all, W["w_ada"], W["b_ada"])
        mods = [mod[:, j * d:(j + 1) * d] for j in range(6)]
        mod_p = [m[:bp][:, None, :] for m in mods]
        mod_s = [jnp.repeat(m[bp:], ls, axis=0)[None] for m in mods]

        attend_p = lambda q, k, z, v: _moba_prompt(q, k, z, grp_p)
        hp, kv, st = _layer(hp, mod_p, W, st_p, tab_p, attend_p, grp_p)
        kv_p.append(kv)
        sts_p.append(st)

        prev = (state_mlstm_c[l], state_mlstm_n[l], state_mlstm_m[l], state_conv[l], state_rwkv[l],
                state_rwkv_shift[l], state_ffn_conv[l])
        attend_s = lambda q, k, z, v: _moba_sample(q, k, v, cache_k, cache_v, page_table, l, grp_s)
        hs, kv, st = _layer(hs, mod_s, W, prev, tab_s, attend_s, grp_s)
        kv_s.append(kv)
        sts_s.append(st)

    heads = lambda t, b_, l_: t.reshape(b_, l_, N_HEADS, HEAD_DIM)
    k_prompt = jnp.stack([heads(k, bp, lp) for k, _ in kv_p])
    v_prompt = jnp.stack([heads(v, bp, lp) for _, v in kv_p])
    k_sample = jnp.stack([heads(k, bs, ls) for k, _ in kv_s])
    v_sample = jnp.stack([heads(v, bs, ls) for _, v in kv_s])
    stack = lambda sts: [jnp.stack(t) for t in zip(*sts)]
    return (hp.reshape(bp, lp, d), hs.reshape(bs, ls, d), k_prompt, v_prompt, k_sample, v_sample,
            *stack(sts_p), *stack(sts_s))
```

```python
import functools
import math

import jax
import jax.numpy as jnp
from jax import lax
from jax.experimental import pallas as pl
from jax.experimental.pallas import tpu as pltpu

F32 = jnp.float32
BF16 = jnp.bfloat16
HIGHEST = lax.Precision.HIGHEST

D_MODEL = 1024
HEAD_DIM = 64
N_HEADS = 4
BRANCH = N_HEADS * HEAD_DIM
CONV_WIDTH = 3
MB_BLOCK = 256
MB_TOPK = 3
PAGE_SIZE = 128
ROPE_DIMS = HEAD_DIM // 4
ROPE_THETA = 500000.0
RW_LORA = 128
RW_COLS = 3 * BRANCH + RW_LORA
D_FF = 11 * D_MODEL // 4
MLSTM_CHUNK = 64
RWKV_CHUNK = 64
FFN_HALO = 16
SCAN_BATCH = 4
RWKV_BATCH = 4
NORM_EPS = 1e-6
GN_EPS = 64e-5
LANES = 128

Z_GATES = 0
Z_RWKV = 4096
Z_RWKV_W = 1024
Z_MIF = Z_RWKV + RW_COLS
Z_MLSTM = 5120
Z_SCONV = 6144
Z_MOBA = 6912
Z_WIDTH = 7680

VMEM_LIMIT = 48 * 1024 * 1024


def _cparams(*sem):
    return pltpu.CompilerParams(dimension_semantics=sem, vmem_limit_bytes=VMEM_LIMIT)


def _sigmoid(x):
    return 1.0 / (1.0 + jnp.exp(-x))


def _head_blockdiag(n):
    r = lax.broadcasted_iota(jnp.int32, (n, n), 0) // HEAD_DIM
    c = lax.broadcasted_iota(jnp.int32, (n, n), 1) // HEAD_DIM
    return (r == c).astype(F32)


def _head_sum(x, bd):
    return jnp.dot(x, bd, precision=HIGHEST, preferred_element_type=F32)


def _dot_nt(a, b):
    return lax.dot_general(a, b, (((1,), (1,)), ((), ())), preferred_element_type=F32)


def _dot_tn(a, b):
    return lax.dot_general(a, b, (((0,), (0,)), ((), ())), preferred_element_type=F32)


def _pick_tile(n, cap):
    t = cap
    while n % t:
        t //= 2
    return t


def _ada_kernel(c_ref, w_ref, b_ref, o_ref):
    c = c_ref[...]
    a = (c * _sigmoid(c)).astype(BF16)
    o_ref[...] = jnp.dot(a, w_ref[...], preferred_element_type=F32) + b_ref[...]


def _ada(c_all, w, b):
    m, d = c_all.shape
    n = w.shape[1]
    tn = 1536
    return pl.pallas_call(
        _ada_kernel,
        grid=(n // tn,),
        in_specs=[pl.BlockSpec((m, d), lambda j: (0, 0)),
                  pl.BlockSpec((d, tn), lambda j: (0, j)),
                  pl.BlockSpec((1, tn), lambda j: (0, j))],
        out_specs=pl.BlockSpec((m, tn), lambda j: (0, j)),
        out_shape=jax.ShapeDtypeStruct((m, n), F32),
        compiler_params=_cparams("parallel"),
        name="ada",
    )(c_all, w, b)


def _nmm_kernel(x_ref, g_ref, sc_ref, sh_ref, w_ref, o_ref, h_sc):
    @pl.when(pl.program_id(1) == 0)
    def _():
        x = x_ref[...]
        y = x * lax.rsqrt(jnp.mean(x * x, axis=-1, keepdims=True) + NORM_EPS) * g_ref[...]
        h_sc[...] = (y * (1.0 + sc_ref[...]) + sh_ref[...]).astype(BF16)

    o_ref[...] = jnp.dot(h_sc[...], w_ref[...], preferred_element_type=F32)


def _norm_mod_matmul(x, g, sc, sh, w, grp, tn):
    n, d = x.shape
    nout = w.shape[1]
    tm, tpg, r = grp["tm"], grp["tpg"], sc.shape[1]
    return pl.pallas_call(
        _nmm_kernel,
        grid=(n // tm, nout // tn),
        in_specs=[pl.BlockSpec((tm, d), lambda i, j: (i, 0)),
                  pl.BlockSpec((1, d), lambda i, j: (0, 0)),
                  pl.BlockSpec((None, r, d), lambda i, j: (i // tpg, 0, 0)),
                  pl.BlockSpec((None, r, d), lambda i, j: (i // tpg, 0, 0)),
                  pl.BlockSpec((d, tn), lambda i, j: (0, j))],
        out_specs=pl.BlockSpec((tm, tn), lambda i, j: (i, j)),
        out_shape=jax.ShapeDtypeStruct((n, nout), F32),
        scratch_shapes=[pltpu.VMEM((tm, d), BF16)],
        compiler_params=_cparams("parallel", "arbitrary"),
        name="norm_mod_matmul",
    )(x, g, sc, sh, w)


def _log_sigmoid(x):
    return jnp.minimum(x, 0.0) - jnp.log(1.0 + jnp.exp(-jnp.abs(x)))


def _mlstm_kernel(zq_ref, zif_ref, bias_ref, g_ref, c0_ref, n0_ref, m0_ref,
                  act_ref, c_out, n_out, m_out, c_sc, n_sc, m_sc, *, chunk, bb):
    t_ = chunk
    ci = pl.program_id(1)

    @pl.when(ci == 0)
    def _():
        c_sc[...] = c0_ref[...]
        n_sc[...] = n0_ref[...]
        m_sc[...] = m0_ref[...]

    row = lax.broadcasted_iota(jnp.int32, (t_, t_), 0)
    col = lax.broadcasted_iota(jnp.int32, (t_, t_), 1)
    causal = col <= row
    eye = row == col
    row1 = lax.broadcasted_iota(jnp.int32, (t_, 1), 0)
    chains = [(b, h) for b in range(bb) for h in range(N_HEADS)]
    idx = range(len(chains))
    each = lambda f, *xs: [f(*a) for a in zip(*xs)]
    mmf = lambda a, b: jnp.dot(a, b, preferred_element_type=F32)
    gates = [zif_ref[b] + bias_ref[...] for b in range(bb)]
    logf = each(_log_sigmoid, gates)
    part = lambda j: [zq_ref[b, :, j * BRANCH + h * HEAD_DIM:j * BRANCH + (h + 1) * HEAD_DIM] for b, h in chains]
    q, k, v, o = part(0), [x * (HEAD_DIM ** -0.5) for x in part(1)], part(2), part(3)
    li_col = [gates[b][:, h:h + 1] for b, h in chains]
    lf_col = [logf[b][:, N_HEADS + h:N_HEADS + h + 1] for b, h in chains]
    li_row = each(lambda x: jnp.sum(jnp.where(eye, x, 0.0), axis=0, keepdims=True), li_col)
    lf_row = each(lambda x: jnp.sum(jnp.where(eye, x, 0.0), axis=0, keepdims=True), lf_col)
    b_col = each(lambda x: jnp.sum(jnp.where(causal, x, 0.0), axis=1, keepdims=True), lf_row)
    b_row = each(lambda x: jnp.sum(jnp.where(row <= col, x, 0.0), axis=0, keepdims=True), lf_col)
    m_prev = [m_sc[b, h][:, 0:1] for b, h in chains]
    d = each(lambda bc, br, lr: jnp.where(causal, bc - br + lr, -jnp.inf), b_col, b_row, li_row)
    inter = each(lambda bc, m: bc + m, b_col, m_prev)
    mt = each(lambda x, y: jnp.maximum(jnp.max(x, axis=1, keepdims=True), y), d, inter)
    qb, kb, vb = (each(lambda x: x.astype(BF16), t) for t in (q, k, v))
    w = each(lambda x, m, a, b: jnp.exp(x - m) * _dot_nt(a, b), d, mt, qb, kb)
    a_int = each(lambda x, m: jnp.exp(x - m), inter, mt)
    c = [c_sc[b, h] for b, h in chains]
    n = [n_sc[b, h] for b, h in chains]
    num = each(lambda w_, v_, a, q_, c_: mmf(w_.astype(BF16), v_) + a * mmf(q_, c_.astype(BF16)), w, vb, a_int, qb, c)
    den = each(lambda w_, a, q_, n_: jnp.sum(w_, axis=1, keepdims=True) + a * jnp.sum(q_ * n_, axis=1, keepdims=True),
               w, a_int, q, n)
    hh = each(lambda x, y, m: x / jnp.maximum(jnp.abs(y), jnp.exp(-m)), num, den, mt)
    hn = each(lambda x: x * lax.rsqrt(jnp.mean(x * x, axis=-1, keepdims=True) + NORM_EPS), hh)
    b_end = each(lambda x: jnp.sum(jnp.where(row1 == t_ - 1, x, 0.0), axis=0, keepdims=True), b_col)
    g_col = each(lambda e, bc, l: e - bc + l, b_end, b_col, li_col)
    m_new = each(lambda e, m, g_: jnp.maximum(e + m, jnp.max(g_, axis=0, keepdims=True)), b_end, m_prev, g_col)
    wk = each(lambda g_, m, k_: jnp.exp(g_ - m) * k_, g_col, m_new, k)
    decay = each(lambda e, m, mn: jnp.exp(e + m - mn), b_end, m_prev, m_new)
    c_new = each(lambda dc, c_, wk_, v_: dc * c_ + _dot_tn(wk_.astype(BF16), v_), decay, c, wk, vb)
    for i in idx:
        b, h = chains[i]
        lo = h * HEAD_DIM
        act_ref[b, :, lo:lo + HEAD_DIM] = hn[i] * g_ref[:, lo:lo + HEAD_DIM] * _sigmoid(o[i])
        c_sc[b, h] = c_new[i]
        n_sc[b, h] = decay[i] * n[i] + jnp.sum(wk[i], axis=0, keepdims=True)
        m_sc[b, h] = jnp.broadcast_to(m_new[i], (1, LANES))

    @pl.when(ci == pl.num_programs(1) - 1)
    def _():
        c_out[...] = c_sc[...]
        n_out[...] = n_sc[...]
        m_out[...] = m_sc[...]


def _mlstm(z, bias_if, g, c0, n0, m0, grp):
    b_, l_ = grp["B"], grp["L"]
    t_ = min(MLSTM_CHUNK, l_)
    nc = l_ // t_
    h_ = N_HEADS
    bb = _pick_tile(b_, SCAN_BATCH)
    n0 = n0.reshape(b_, h_, 1, HEAD_DIM)
    m0 = jnp.broadcast_to(m0.reshape(b_, h_, 1, 1), (b_, h_, 1, LANES))
    z3 = z.reshape(b_, l_, Z_WIDTH)
    state_spec = lambda shp: pl.BlockSpec((bb,) + shp, lambda b, c: (b, 0, 0, 0))
    act, c1, n1, m1 = pl.pallas_call(
        functools.partial(_mlstm_kernel, chunk=t_, bb=bb),
        grid=(b_ // bb, nc),
        in_specs=[pl.BlockSpec((bb, t_, 4 * BRANCH), lambda b, c: (b, c, Z_MLSTM // (4 * BRANCH))),
                  pl.BlockSpec((bb, t_, LANES), lambda b, c: (b, c, Z_MIF // LANES)),
                  pl.BlockSpec((1, LANES), lambda b, c: (0, 0)),
                  pl.BlockSpec((1, BRANCH), lambda b, c: (0, 0)),
                  state_spec((h_, HEAD_DIM, HEAD_DIM)),
                  state_spec((h_, 1, HEAD_DIM)),
                  state_spec((h_, 1, LANES))],
        out_specs=[pl.BlockSpec((bb, t_, BRANCH), lambda b, c: (b, c, 0)),
                   state_spec((h_, HEAD_DIM, HEAD_DIM)),
                   state_spec((h_, 1, HEAD_DIM)),
                   state_spec((h_, 1, LANES))],
        out_shape=[jax.ShapeDtypeStruct((b_, l_, BRANCH), F32),
                   jax.ShapeDtypeStruct((b_, h_, HEAD_DIM, HEAD_DIM), F32),
                   jax.ShapeDtypeStruct((b_, h_, 1, HEAD_DIM), F32),
                   jax.ShapeDtypeStruct((b_, h_, 1, LANES), F32)],
        scratch_shapes=[pltpu.VMEM((bb, h_, HEAD_DIM, HEAD_DIM), F32),
                        pltpu.VMEM((bb, h_, 1, HEAD_DIM), F32),
                        pltpu.VMEM((bb, h_, 1, LANES), F32)],
        compiler_params=_cparams("parallel", "arbitrary"),
        name="mlstm",
    )(z3, z3, bias_if, g, c0, n0, m0)
    return act.reshape(b_ * l_, BRANCH), c1, n1.reshape(b_, h_, HEAD_DIM), m1[:, :, 0, 0]


def _conv3(p, prev2, prev1, w_ref):
    row = lax.broadcasted_iota(jnp.int32, p.shape, 0)
    p1 = jnp.where(row == 0, prev1, pltpu.roll(p, 1, axis=0))
    p2 = jnp.where(row == 0, prev2, jnp.where(row == 1, prev1, pltpu.roll(p, 2, axis=0)))
    return w_ref[0:1, :] * p2 + w_ref[1:2, :] * p1 + w_ref[2:3, :] * p


def _sconv_kernel(sb_ref, sc_ref, sh_ref, w_ref, buf_ref, out_ref, st_ref, carry, *, tm):
    @pl.when(pl.program_id(1) == 0)
    def _():
        carry[...] = buf_ref[0]

    p = sc_ref[...] * sh_ref[...]
    u = _conv3(p, carry[0:1, :], carry[1:2, :], w_ref)
    out_ref[...] = sb_ref[...] * u
    new = sc_ref[tm - 2:tm, :] * sh_ref[tm - 2:tm, :]
    carry[...] = new
    st_ref[0] = new


def _sconv(z, w, buf, grp):
    b_, l_ = grp["B"], grp["L"]
    tm = _pick_tile(l_, 512)
    nt = l_ // tm
    cb = Z_SCONV // BRANCH
    zspec = lambda j: pl.BlockSpec((tm, BRANCH), lambda b, i: (b * nt + i, cb + j))
    return pl.pallas_call(
        functools.partial(_sconv_kernel, tm=tm),
        grid=(b_, nt),
        in_specs=[zspec(0), zspec(1), zspec(2),
                  pl.BlockSpec((CONV_WIDTH, BRANCH), lambda b, i: (0, 0)),
                  pl.BlockSpec((1, CONV_WIDTH - 1, BRANCH), lambda b, i: (b, 0, 0))],
        out_specs=[pl.BlockSpec((tm, BRANCH), lambda b, i: (b * nt + i, 0)),
                   pl.BlockSpec((1, CONV_WIDTH - 1, BRANCH), lambda b, i: (b, 0, 0))],
        out_shape=[jax.ShapeDtypeStruct((b_ * l_, BRANCH), F32),
                   jax.ShapeDtypeStruct((b_, CONV_WIDTH - 1, BRANCH), F32)],
        scratch_shapes=[pltpu.VMEM((CONV_WIDTH - 1, BRANCH), F32)],
        compiler_params=_cparams("parallel", "arbitrary"),
        name="sconv",
    )(z, z, z, w, buf)


def _qk_prep_kernel(q_ref, k_ref, qg_ref, kg_ref, cos_ref, sa_ref, sb_ref, qo_ref, ko_ref):
    bd = _head_blockdiag(BRANCH)
    cos, sa, sb = cos_ref[...], sa_ref[...], sb_ref[...]
    half = ROPE_DIMS // 2

    def prep(x, g):
        y = x * lax.rsqrt(_head_sum(x * x, bd) * (1.0 / HEAD_DIM) + NORM_EPS) * g
        return y * cos + pltpu.roll(y, BRANCH - half, axis=1) * sa + pltpu.roll(y, half, axis=1) * sb

    qo_ref[...] = prep(q_ref[...], qg_ref[...])
    ko_ref[...] = prep(k_ref[...], kg_ref[...])


def _rope_tables(pos):
    half = ROPE_DIMS // 2
    inv = jnp.exp(-math.log(ROPE_THETA) * jnp.arange(0, ROPE_DIMS, 2, dtype=F32) / ROPE_DIMS)
    ang = pos.astype(F32)[:, None] * inv[None, :]
    cos, sin = jnp.cos(ang), jnp.sin(ang)
    n = pos.shape[0]
    rest = HEAD_DIM - ROPE_DIMS
    c_h = jnp.concatenate([cos, cos, jnp.ones((n, rest), F32)], axis=1)
    sa_h = jnp.concatenate([-sin, jnp.zeros((n, half + rest), F32)], axis=1)
    sb_h = jnp.concatenate([jnp.zeros((n, half), F32), sin, jnp.zeros((n, rest), F32)], axis=1)
    tile = lambda t: jnp.tile(t, (1, N_HEADS))
    return tile(c_h), tile(sa_h), tile(sb_h)


def _qk_prep(z, qg, kg, tables, grp):
    n = grp["B"] * grp["L"]
    ltab = tables[0].shape[0]
    tm = _pick_tile(ltab, 512)
    npos = ltab // tm
    cb = Z_MOBA // BRANCH
    tspec = pl.BlockSpec((tm, BRANCH), lambda i: (i % npos, 0))
    gspec = pl.BlockSpec((1, BRANCH), lambda i: (0, 0))
    ospec = pl.BlockSpec((tm, BRANCH), lambda i: (i, 0))
    return pl.pallas_call(
        _qk_prep_kernel,
        grid=(n // tm,),
        in_specs=[pl.BlockSpec((tm, BRANCH), lambda i: (i, cb)),
                  pl.BlockSpec((tm, BRANCH), lambda i: (i, cb + 1)),
                  gspec, gspec, tspec, tspec, tspec],
        out_specs=[ospec, ospec],
        out_shape=[jax.ShapeDtypeStruct((n, BRANCH), F32)] * 2,
        compiler_params=_cparams("parallel"),
        name="qk_prep",
    )(z, z, qg, kg, *tables)


def _moba_prompt_kernel(q_ref, k_ref, v_ref, o_ref, kmean, kx, vx, *, nb):
    qi = pl.program_id(1)
    tq = MB_BLOCK
    half = LANES // 2
    masked = float(jnp.finfo(BF16).min)
    lane = lax.broadcasted_iota(jnp.int32, (tq, LANES), 1)
    low = lane < half

    @pl.when(qi == 0)
    def _():
        kmean[...] = jnp.zeros_like(kmean)
        for n in range(nb):
            rows = slice(n * MB_BLOCK, (n + 1) * MB_BLOCK)
            kmean[n:n + 1, :] = jnp.mean(k_ref[rows, :], axis=0, keepdims=True)
            onehot = (lane == half + n).astype(F32)
            for h in range(N_HEADS):
                pair = slice((h // 2) * LANES, (h // 2 + 1) * LANES)
                k2, v2 = k_ref[rows, pair], v_ref[rows, pair]
                if h % 2:
                    k2, v2 = pltpu.roll(k2, half, axis=1), pltpu.roll(v2, half, axis=1)
                kx[h, rows, :] = jnp.where(low, k2, onehot).astype(BF16)
                vx[h, rows, :] = jnp.where(low, v2, 1.0).astype(BF16)

    row = lax.broadcasted_iota(jnp.int32, (tq, tq), 0)
    col = lax.broadcasted_iota(jnp.int32, (tq, tq), 1)
    scale = HEAD_DIM ** -0.5
    own0 = pl.multiple_of(qi * MB_BLOCK, MB_BLOCK)
    heads = range(N_HEADS)
    sub = 8
    assert nb <= sub
    blk = lax.broadcasted_iota(jnp.int32, (sub, tq), 0)
    place = (lax.broadcasted_iota(jnp.int32, (sub, LANES), 1)
             == lax.broadcasted_iota(jnp.int32, (sub, LANES), 0) + half).astype(BF16)
    qx = []
    for h in heads:
        hs = slice(h * HEAD_DIM, (h + 1) * HEAD_DIM)
        s = lax.dot_general(kmean[0:sub, hs], q_ref[:, hs], (((1,), (1,)), ((), ())),
                            precision=HIGHEST, preferred_element_type=F32)
        valid = blk < qi
        s = jnp.where(valid, s, -jnp.inf)
        rank = jnp.zeros((sub, tq), jnp.int32)
        for m in range(nb):
            sm = s[m:m + 1, :]
            rank += ((sm > s) | ((sm == s) & (m < blk))).astype(jnp.int32)
        keep = ((valid & (rank < MB_TOPK)) | (blk == qi)).astype(BF16)
        keep_q = _dot_tn(keep, place)
        q2 = q_ref[:, (h // 2) * LANES:(h // 2 + 1) * LANES] * scale
        if h % 2:
            q2 = pltpu.roll(q2, half, axis=1)
        qx.append(jnp.where(low, q2, jnp.where(keep_q > 0.5, 0.0, masked)).astype(BF16))
    sc = [jnp.where(col <= row, _dot_nt(qx[h], kx[h, pl.ds(own0, MB_BLOCK), :]), -jnp.inf) for h in heads]
    m_i = [jnp.max(sc[h], axis=1, keepdims=True) for h in heads]
    p = [jnp.exp(sc[h] - m_i[h]).astype(BF16) for h in heads]
    acc = [jnp.dot(p[h], vx[h, pl.ds(own0, MB_BLOCK), :], preferred_element_type=F32) for h in heads]

    def body(n, carry):
        m_i, acc = carry
        k0 = pl.multiple_of(n * MB_BLOCK, MB_BLOCK)
        sc = [_dot_nt(qx[h], kx[h, pl.ds(k0, MB_BLOCK), :]) for h in heads]
        m_new = [jnp.maximum(m_i[h], jnp.max(sc[h], axis=1, keepdims=True)) for h in heads]
        p = [jnp.exp(sc[h] - m_new[h]).astype(BF16) for h in heads]
        pv = [jnp.dot(p[h], vx[h, pl.ds(k0, MB_BLOCK), :], preferred_element_type=F32) for h in heads]
        return m_new, [jnp.exp(m_i[h] - m_new[h]) * acc[h] + pv[h] for h in heads]

    m_i, acc = lax.fori_loop(0, qi, body, (m_i, acc))
    for h in heads:
        o_ref[:, h * HEAD_DIM:(h + 1) * HEAD_DIM] = (acc[h] / pltpu.roll(acc[h], half, axis=1))[:, :half]


def _moba_prompt(q, k, z, grp):
    b_, l_ = grp["B"], grp["L"]
    assert l_ % MB_BLOCK == 0 and l_ // MB_BLOCK <= LANES // 2
    nb = l_ // MB_BLOCK
    return pl.pallas_call(
        functools.partial(_moba_prompt_kernel, nb=nb),
        grid=(b_, nb),
        in_specs=[pl.BlockSpec((MB_BLOCK, BRANCH), lambda b, i: (b * nb + i, 0)),
                  pl.BlockSpec((l_, BRANCH), lambda b, i: (b, 0)),
                  pl.BlockSpec((l_, BRANCH), lambda b, i: (b, Z_MOBA // BRANCH + 2))],
        out_specs=pl.BlockSpec((MB_BLOCK, BRANCH), lambda b, i: (b * nb + i, 0)),
        out_shape=jax.ShapeDtypeStruct((b_ * l_, BRANCH), F32),
        scratch_shapes=[pltpu.VMEM((LANES, BRANCH), F32),
                        pltpu.VMEM((N_HEADS, l_, LANES), BF16),
                        pltpu.VMEM((N_HEADS, l_, LANES), BF16)],
        compiler_params=_cparams("parallel", "arbitrary"),
        name="moba_prompt",
    )(q, k, z)


def _moba_sample_kernel(pt_ref, q_ref, kn_ref, vn_ref, *refs, nblk, dec, gsz):
    ppb = MB_BLOCK // PAGE_SIZE
    k_refs, v_refs = refs[:ppb * gsz], refs[ppb * gsz:2 * ppb * gsz]
    o_ref, kmean_s, m_s, l_s, o_s = refs[2 * ppb * gsz:]
    step = pl.program_id(1)
    nq = N_HEADS * dec
    scale = HEAD_DIM ** -0.5
    lane = lax.broadcasted_iota(jnp.int32, (nq, LANES), 1)
    lane_c = lax.broadcasted_iota(jnp.int32, (BRANCH, LANES), 1)

    @pl.when(step == 0)
    def _():
        kmean_s[...] = jnp.zeros_like(kmean_s)
        m_s[...] = jnp.full_like(m_s, -jnp.inf)
        l_s[...] = jnp.zeros_like(l_s)

    row_head = jnp.concatenate([jnp.full((dec, BRANCH), h, jnp.int32) for h in range(N_HEADS)], axis=0)
    lane_head = lax.broadcasted_iota(jnp.int32, (nq, BRANCH), 1) // HEAD_DIM
    qbd = jnp.where(row_head == lane_head, jnp.concatenate([q_ref[...]] * N_HEADS, axis=0), 0.0)
    qbd_b = qbd.astype(BF16)
    kmean, m_all, l_all = kmean_s[...], m_s[...], l_s[...]
    blocks = range(gsz)
    kt = [jnp.concatenate([k_refs[ppb * g + j][...] for j in range(ppb)], axis=1) for g in blocks]
    s = [jnp.dot(qbd_b, kt[g].astype(BF16), preferred_element_type=F32) * scale for g in blocks]
    mx = [jnp.max(s[g], axis=1, keepdims=True) for g in blocks]
    p = [jnp.exp(s[g] - mx[g]) for g in blocks]
    vt = [jnp.concatenate([v_refs[ppb * g + j][...] for j in range(ppb)], axis=1) for g in blocks]
    o = [_dot_nt(p[g].astype(BF16), vt[g].astype(BF16)) for g in blocks]
    for g in blocks:
        blk = step * gsz + g
        o_s[blk] = o[g]
        kmean = jnp.where(lane_c == blk, jnp.mean(kt[g], axis=1, keepdims=True), kmean)
        m_all = jnp.where(lane == blk, mx[g], m_all)
        l_all = jnp.where(lane == blk, jnp.sum(p[g], axis=1, keepdims=True), l_all)
    kmean_s[...] = kmean
    m_s[...] = m_all
    l_s[...] = l_all

    @pl.when(step == pl.num_programs(1) - 1)
    def _():
        zpad = jnp.zeros((LANES - dec, BRANCH), F32)
        knew = jnp.concatenate([kn_ref[...], zpad], axis=0)
        vnew = jnp.concatenate([vn_ref[...], zpad], axis=0)
        rowq = jnp.concatenate([lax.broadcasted_iota(jnp.int32, (dec, LANES), 0)] * N_HEADS, axis=0)
        lane_f = lane.astype(F32)
        s = jnp.dot(qbd, kmean_s[...], precision=HIGHEST, preferred_element_type=F32)
        s = jnp.where(lane < nblk, s, -jnp.inf)
        sel = jnp.zeros((nq, LANES), jnp.bool_)
        for _ in range(MB_TOPK):
            mxv = jnp.max(s, axis=1, keepdims=True)
            idx = jnp.min(jnp.where(s == mxv, lane_f, 2.0 * LANES), axis=1, keepdims=True)
            pick = lane_f == idx
            sel = sel | pick
            s = jnp.where(pick, -jnp.inf, s)
        sel = sel & (lane < nblk)
        s_own = jnp.where(lane <= rowq, _dot_nt(qbd_b, knew.astype(BF16)) * scale, -jnp.inf)
        m_own = jnp.max(s_own, axis=1, keepdims=True)
        p_own = jnp.exp(s_own - m_own)
        l_own = jnp.sum(p_own, axis=1, keepdims=True)
        o_own = jnp.dot(p_own.astype(BF16), vnew.astype(BF16), preferred_element_type=F32)
        mb = m_s[...]
        mtot = jnp.maximum(jnp.max(jnp.where(sel, mb, -jnp.inf), axis=1, keepdims=True), m_own)
        wgt = jnp.where(sel, jnp.exp(mb - mtot), 0.0)
        a_own = jnp.exp(m_own - mtot)
        ltot = jnp.sum(wgt * l_s[...], axis=1, keepdims=True) + l_own * a_own
        acc = o_own * a_own
        for j in range(nblk):
            acc = acc + wgt[:, j:j + 1] * o_s[j]
        res = acc / ltot
        for h in range(N_HEADS):
            hs = slice(h * HEAD_DIM, (h + 1) * HEAD_DIM)
            o_ref[:, hs] = res[h * dec:(h + 1) * dec, hs]


def _moba_sample(q, k, v, pool_k, pool_v, page_table, layer, grp):
    b_, dec = grp["B"], grp["L"]
    n_pages = page_table.shape[1]
    ppb = MB_BLOCK // PAGE_SIZE
    assert n_pages % ppb == 0
    nblk = n_pages // ppb
    assert MB_TOPK <= nblk <= LANES and dec <= LANES and dec % 8 == 0
    gsz = max(g for g in range(1, 9) if nblk % g == 0)
    as_pages = lambda t: t.transpose(0, 1, 3, 4, 2).reshape(t.shape[0], t.shape[1], BRANCH, PAGE_SIZE)
    pool_k, pool_v = as_pages(pool_k), as_pages(pool_v)
    pt = page_table.reshape(-1)

    def page(j):
        return pl.BlockSpec((None, None, BRANCH, PAGE_SIZE),
                            lambda b, s, pt: (layer, pt[b * n_pages + s * gsz * ppb + j], 0, 0))

    pages = [page(j) for j in range(gsz * ppb)]
    new = pl.BlockSpec((dec, BRANCH), lambda b, s, pt: (b, 0))
    nq = N_HEADS * dec
    return pl.pallas_call(
        functools.partial(_moba_sample_kernel, nblk=nblk, dec=dec, gsz=gsz),
        grid_spec=pltpu.PrefetchScalarGridSpec(
            num_scalar_prefetch=1,
            grid=(b_, nblk // gsz),
            in_specs=[new, new, new] + pages + pages,
            out_specs=new,
            scratch_shapes=[pltpu.VMEM((BRANCH, LANES), F32),
                            pltpu.VMEM((nq, LANES), F32),
                            pltpu.VMEM((nq, LANES), F32),
                            pltpu.VMEM((nblk, nq, BRANCH), F32)]),
        out_shape=jax.ShapeDtypeStruct((b_ * dec, BRANCH), F32),
        compiler_params=_cparams("parallel", "arbitrary"),
        name="moba_sample",
    )(pt, q, k, v, *([pool_k] * (gsz * ppb)), *([pool_v] * (gsz * ppb)))


def _rwkv_prep_kernel(z_ref, prev_ref, mu_ref, wlr_ref, w0_ref, a0_ref, kks_ref, kas_ref, rk_ref,
                      r_o, w_o, k_o, v_o, kk_o, kka_o, bonus_o, g_o, carry, *, tm):
    @pl.when(pl.program_id(1) == 0)
    def _():
        carry[...] = prev_ref[0]

    z = z_ref[...]
    row = lax.broadcasted_iota(jnp.int32, z.shape, 0)
    zs = jnp.where(row == 0, carry[...], pltpu.roll(z, 1, axis=0))
    carry[...] = z_ref[tm - 1:tm, :]
    zz = z + mu_ref[...] * (zs - z)
    r = zz[:, 0:BRANCH]
    k = zz[:, BRANCH:2 * BRANCH]
    v = zz[:, 2 * BRANCH:3 * BRANCH]
    lr = zz[:, 3 * BRANCH:3 * BRANCH + RW_LORA]
    lane = lax.broadcasted_iota(jnp.int32, lr.shape, 1)
    lr_in = jnp.where(lane < 32, jnp.tanh(lr), jnp.where(lane < 64, lr, _sigmoid(lr)))
    lo = jnp.dot(lr_in.astype(BF16), wlr_ref[...], preferred_element_type=F32)
    log_decay = -math.exp(-0.5) * _sigmoid(w0_ref[...] + lo[:, 0:BRANCH])
    a = _sigmoid(a0_ref[...] + lo[:, BRANCH:2 * BRANCH])
    bd = _head_blockdiag(BRANCH)
    kk = k * kks_ref[...]
    kk = kk * lax.rsqrt(jnp.maximum(_head_sum(kk * kk, bd), 1e-24))
    k2 = k * (1.0 + (a - 1.0) * kas_ref[...])
    r_o[...] = r
    w_o[...] = log_decay
    k_o[...] = k2
    v_o[...] = v
    kk_o[...] = kk
    kka_o[...] = kk * a
    bonus_o[...] = _head_sum(r * k2 * rk_ref[...], bd) * v
    g_o[...] = lo[:, 2 * BRANCH:3 * BRANCH]


def _rwkv_prep(z, prev, p, grp):
    b_, l_ = grp["B"], grp["L"]
    tm = _pick_tile(l_, 256)
    nt = l_ // tm
    zw = Z_RWKV_W
    vec = pl.BlockSpec((1, BRANCH), lambda b, i: (0, 0))
    ospec = pl.BlockSpec((tm, BRANCH), lambda b, i: (b * nt + i, 0))
    return pl.pallas_call(
        functools.partial(_rwkv_prep_kernel, tm=tm),
        grid=(b_, nt),
        in_specs=[pl.BlockSpec((tm, zw), lambda b, i: (b * nt + i, Z_RWKV // zw)),
                  pl.BlockSpec((1, 1, zw), lambda b, i: (b, 0, 0)),
                  pl.BlockSpec((1, zw), lambda b, i: (0, 0)),
                  pl.BlockSpec((RW_LORA, 3 * BRANCH), lambda b, i: (0, 0)),
                  vec, vec, vec, vec, vec],
        out_specs=[ospec] * 8,
        out_shape=[jax.ShapeDtypeStruct((b_ * l_, BRANCH), F32)] * 8,
        scratch_shapes=[pltpu.VMEM((1, zw), F32)],
        compiler_params=_cparams("parallel", "arbitrary"),
        name="rwkv_prep",
    )(z, prev, p["mu"], p["wlr"], p["w0"], p["a0"], p["kk"], p["ka"], p["rk"])


def _rwkv_chunk_kernel(r_ref, lw_ref, k_ref, v_ref, kk_ref, kka_ref, h0_ref, y_ref, h_out, h_sc, *, chunk, bb):
    t_ = chunk
    ci = pl.program_id(1)

    head_blocks = [slice(h * HEAD_DIM, (h + 1) * HEAD_DIM) for h in range(N_HEADS)]

    @pl.when(ci == 0)
    def _():
        h_sc[...] = jnp.zeros_like(h_sc)
        for b in range(bb):
            for h, hs in enumerate(head_blocks):
                h_sc[b, hs, hs] = h0_ref[b, h]

    t4 = N_HEADS * t_
    rows = list(range(bb))
    each = lambda f, *xs: [f(*a) for a in zip(*xs)]
    tri = (lax.broadcasted_iota(jnp.int32, (t_, t_), 0) >= lax.broadcasted_iota(jnp.int32, (t_, t_), 1)).astype(F32)
    row1 = lax.broadcasted_iota(jnp.int32, (t_, 1), 0)
    head_mask = (jnp.concatenate([jnp.full((t_, BRANCH), h, jnp.int32) for h in range(N_HEADS)], axis=0)
                 == lax.broadcasted_iota(jnp.int32, (t4, BRANCH), 1) // HEAD_DIM)
    t_row = jnp.concatenate([lax.broadcasted_iota(jnp.int32, (t_, t4), 0)] * N_HEADS, axis=0)
    t_col = lax.broadcasted_iota(jnp.int32, (t4, t4), 1) & (t_ - 1)
    strict = t_row > t_col
    lower = t_row >= t_col
    stack = lambda x: jnp.where(head_mask, jnp.concatenate([x] * N_HEADS, axis=0), 0.0).astype(BF16)
    mm = lambda a, b: jnp.dot(a, b, preferred_element_type=F32)

    lw = [lw_ref[b] for b in rows]
    cum = each(lambda x: jnp.dot(tri, x, precision=HIGHEST, preferred_element_type=F32), lw)
    p_in = each(jnp.exp, cum)
    p_inv = each(lambda c: jnp.exp(-c), cum)
    p_end = each(lambda p: jnp.sum(jnp.where(row1 == t_ - 1, p, 0.0), axis=0, keepdims=True), p_in)
    kkm = [stack(kk_ref[b] * jnp.exp(cum[b] - lw[b])) for b in rows]
    rp = [stack(r_ref[b] * p_in[b]) for b in rows]
    kh = [k_ref[b] * p_inv[b] for b in rows]
    ah = [kka_ref[b] * p_inv[b] for b in rows]
    khe = [stack(kh[b] * p_end[b]) for b in rows]
    ahe = [stack(ah[b] * p_end[b]) for b in rows]
    khm, ahm = each(stack, kh), each(stack, ah)
    vm = [stack(v_ref[b]) for b in rows]
    lr = each(lambda a, b: jnp.concatenate([a, b], axis=0), kkm, rp)
    gk = each(_dot_nt, lr, khm)
    ga = each(_dot_nt, lr, ahm)
    ab_k = each(lambda g: jnp.concatenate([jnp.where(strict, g[:t4], 0.0), jnp.where(lower, g[t4:], 0.0)],
                                          axis=0).astype(BF16), gk)
    b_a = each(lambda g: jnp.where(lower, g[t4:], 0.0).astype(BF16), ga)
    e = each(lambda g: jnp.where(strict, -g[:t4], 0.0), ga)
    pw = e
    span = 2
    while span < t_:
        pw = each(lambda p: mm(p.astype(BF16), p.astype(BF16)), pw)
        e = each(lambda x, p: x + p + mm(x.astype(BF16), p.astype(BF16)), e, pw)
        span *= 2
    ht = [h_sc[b] for b in rows]
    x0 = each(lambda a, h: _dot_nt(a, h.astype(BF16)), lr, ht)
    kv = each(mm, ab_k, vm)
    x1 = each(lambda a, b: a[:t4] + b[:t4], x0, kv)
    u = each(lambda x, m: x + mm(m.astype(BF16), x.astype(BF16)), x1, e)
    ub = each(lambda x: x.astype(BF16), u)
    ybd = each(lambda a, b, m, x: a[t4:] + b[t4:] - mm(m, x), x0, kv, b_a, ub)
    for b in rows:
        y = ybd[b][0:t_]
        for h in range(1, N_HEADS):
            y = y + ybd[b][h * t_:(h + 1) * t_]
        y_ref[b] = y
        h_sc[b] = ht[b] * p_end[b] + _dot_tn(vm[b], khe[b]) - _dot_tn(ub[b], ahe[b])

    @pl.when(ci == pl.num_programs(1) - 1)
    def _():
        for b in range(bb):
            for h, hs in enumerate(head_blocks):
                h_out[b, h] = h_sc[b, hs, hs]


def _rwkv_scan(r, lw, k, v, kk, kka, s0, grp):
    b_, l_ = grp["B"], grp["L"]
    chunk = _pick_tile(l_, RWKV_CHUNK)
    nc = l_ // chunk
    bb = _pick_tile(b_, RWKV_BATCH)
    seq = pl.BlockSpec((bb, chunk, BRANCH), lambda b, c: (b, c, 0))
    st = pl.BlockSpec((bb, N_HEADS, HEAD_DIM, HEAD_DIM), lambda b, c: (b, 0, 0, 0))
    as3 = lambda t: t.reshape(b_, l_, BRANCH)
    y, s1 = pl.pallas_call(
        functools.partial(_rwkv_chunk_kernel, chunk=chunk, bb=bb),
        grid=(b_ // bb, nc),
        in_specs=[seq] * 6 + [st],
        out_specs=[seq, st],
        out_shape=[jax.ShapeDtypeStruct((b_, l_, BRANCH), F32),
                   jax.ShapeDtypeStruct((b_, N_HEADS, HEAD_DIM, HEAD_DIM), F32)],
        scratch_shapes=[pltpu.VMEM((bb, BRANCH, BRANCH), F32)],
        compiler_params=_cparams("parallel", "arbitrary"),
        name="rwkv_scan",
    )(as3(r), as3(lw), as3(k), as3(v), as3(kk), as3(kka), s0)
    return y.reshape(b_ * l_, BRANCH), s1


def _merge_kernel(ml_ref, sc_ref, mb_ref, y_ref, bonus_ref, g_ref, gn_ref, gates_ref, x_ref, gt_ref,
                  mp_ref, sp_ref, ap_ref, rp_ref, wo_ref, o_ref):
    bd = _head_blockdiag(BRANCH)
    y = y_ref[...]
    yc = y - _head_sum(y, bd) * (1.0 / HEAD_DIM)
    yn = yc * lax.rsqrt(_head_sum(yc * yc, bd) * (1.0 / HEAD_DIM) + GN_EPS)
    rw = (yn * gn_ref[...] + bonus_ref[...]) * g_ref[...]
    acts = (ml_ref[...], sc_ref[...], mb_ref[...], rw)
    projs = (mp_ref, sp_ref, ap_ref, rp_ref)
    merged = None
    for j in range(4):
        br = jnp.dot(acts[j].astype(BF16), projs[j][...], preferred_element_type=F32)
        term = _sigmoid(gates_ref[:, j * D_MODEL:(j + 1) * D_MODEL]) * br
        merged = term if merged is None else merged + term
    o_ref[...] = x_ref[...] + gt_ref[...] * jnp.dot(merged.astype(BF16), wo_ref[...], preferred_element_type=F32)


def _merge(ml, sc, mb, y, bonus, g, gn, z, x, gt, projs, w_out, grp):
    n, d = x.shape
    tm = min(256, grp["tm"])
    tpg = grp["tpg"] * (grp["tm"] // tm)
    r = gt.shape[1]
    if r > 1:
        assert r == grp["tm"] and tpg == grp["tm"] // tm
        gt = gt.reshape(r // tm, tm, d)
        gt_spec = pl.BlockSpec((None, tm, d), lambda i: (i, 0, 0))
    else:
        gt_spec = pl.BlockSpec((None, 1, d), lambda i: (i // tpg, 0, 0))
    act = pl.BlockSpec((tm, BRANCH), lambda i: (i, 0))
    proj = pl.BlockSpec((BRANCH, d), lambda i: (0, 0))
    return pl.pallas_call(
        _merge_kernel,
        grid=(n // tm,),
        in_specs=[act] * 6 + [pl.BlockSpec((1, BRANCH), lambda i: (0, 0)),
                              pl.BlockSpec((tm, 4 * d), lambda i: (i, 0)),
                              pl.BlockSpec((tm, d), lambda i: (i, 0)),
                              gt_spec, proj, proj, proj, proj,
                              pl.BlockSpec((d, d), lambda i: (0, 0))],
        out_specs=pl.BlockSpec((tm, d), lambda i: (i, 0)),
        out_shape=jax.ShapeDtypeStruct((n, d), F32),
        compiler_params=_cparams("parallel"),
        name="merge",
    )(ml, sc, mb, y, bonus, g, gn, z, x, gt, *projs, w_out)


def _ffn_act_kernel(ua_ref, ul_ref, w_ref, buf_ref, o_ref, carry, *, tm):
    @pl.when(pl.program_id(1) == 0)
    def _():
        carry[...] = buf_ref[0]

    ua = ua_ref[...]
    u = _conv3(ua, carry[0:1, :], carry[1:2, :], w_ref)
    carry[...] = ua_ref[tm - 2:tm, :]
    o_ref[...] = (u * _sigmoid(u) * ul_ref[...]).astype(BF16)


def _ffn_act(up, w, buf, grp):
    b_, l_ = grp["B"], grp["L"]
    tm = _pick_tile(l_, 256)
    nt = l_ // tm
    half = lambda j: pl.BlockSpec((tm, D_FF), lambda b, i: (b * nt + i, j))
    return pl.pallas_call(
        functools.partial(_ffn_act_kernel, tm=tm),
        grid=(b_, nt),
        in_specs=[half(0), half(1),
                  pl.BlockSpec((CONV_WIDTH, D_FF), lambda b, i: (0, 0)),
                  pl.BlockSpec((1, CONV_WIDTH - 1, D_FF), lambda b, i: (b, 0, 0))],
        out_specs=pl.BlockSpec((tm, D_FF), lambda b, i: (b * nt + i, 0)),
        out_shape=jax.ShapeDtypeStruct((b_ * l_, D_FF), BF16),
        scratch_shapes=[pltpu.VMEM((CONV_WIDTH - 1, D_FF), F32)],
        compiler_params=_cparams("parallel", "arbitrary"),
        name="ffn_act",
    )(up, up, w, buf)


def _ffn_up_kernel(x_ref, xp_ref, g_ref, sc_ref, sh_ref, wa_ref, wl_ref, cw_ref, buf_ref, o_ref, st_ref,
                   h_sc, hp_sc, prev_sc, tail_sc, *, tm, tpg):
    halo = FFN_HALO
    i = pl.program_id(0)

    @pl.when(pl.program_id(1) == 0)
    def _():
        def norm_mod(x):
            y = x * lax.rsqrt(jnp.mean(x * x, axis=-1, keepdims=True) + NORM_EPS) * g_ref[...]
            return (y * (1.0 + sc_ref[...]) + sh_ref[...]).astype(BF16)

        h_sc[...] = norm_mod(x_ref[...])
        hp_sc[...] = norm_mod(xp_ref[...])

    ua = jnp.dot(h_sc[...], wa_ref[...], preferred_element_type=F32)
    ul = jnp.dot(h_sc[...], wl_ref[...], preferred_element_type=F32)
    prev_sc[...] = jnp.dot(hp_sc[...], wa_ref[...], preferred_element_type=F32)
    tail_sc[...] = jnp.dot(h_sc[tm - halo:tm, :], wa_ref[...], preferred_element_type=F32)
    first = i % tpg == 0
    prev2 = jnp.where(first, buf_ref[0, 0:1, :], prev_sc[halo - 2:halo - 1, :])
    prev1 = jnp.where(first, buf_ref[0, 1:2, :], prev_sc[halo - 1:halo, :])
    u = _conv3(ua, prev2, prev1, cw_ref)
    o_ref[...] = (u * _sigmoid(u) * ul).astype(BF16)
    st_ref[0] = tail_sc[halo - 2:halo, :]


def _ffn_up(x, g, sc, sh, w, cw, buf, grp):
    n, d = x.shape
    l_ = grp["L"]
    tm = _pick_tile(l_, 512)
    tpg = l_ // tm
    tn = 1408
    nj = D_FF // tn
    halo = FFN_HALO
    assert sc.shape[1] == 1 and D_FF % tn == 0 and tm % halo == 0
    vec = lambda r: pl.BlockSpec((None, r, d), lambda i, j: (i // tpg, 0, 0))
    act, tails = pl.pallas_call(
        functools.partial(_ffn_up_kernel, tm=tm, tpg=tpg),
        grid=(n // tm, nj),
        in_specs=[pl.BlockSpec((tm, d), lambda i, j: (i, 0)),
                  pl.BlockSpec((halo, d), lambda i, j: (jnp.maximum(i * (tm // halo) - 1, 0), 0)),
                  pl.BlockSpec((1, d), lambda i, j: (0, 0)),
                  vec(1), vec(1),
                  pl.BlockSpec((d, tn), lambda i, j: (0, j)),
                  pl.BlockSpec((d, tn), lambda i, j: (0, nj + j)),
                  pl.BlockSpec((CONV_WIDTH, tn), lambda i, j: (0, j)),
                  pl.BlockSpec((1, CONV_WIDTH - 1, tn), lambda i, j: (i // tpg, 0, j))],
        out_specs=[pl.BlockSpec((tm, tn), lambda i, j: (i, j)),
                   pl.BlockSpec((1, CONV_WIDTH - 1, tn), lambda i, j: (i, 0, j))],
        out_shape=[jax.ShapeDtypeStruct((n, D_FF), BF16),
                   jax.ShapeDtypeStruct((n // tm, CONV_WIDTH - 1, D_FF), F32)],
        scratch_shapes=[pltpu.VMEM((tm, d), BF16), pltpu.VMEM((halo, d), BF16),
                        pltpu.VMEM((halo, tn), F32), pltpu.VMEM((halo, tn), F32)],
        compiler_params=_cparams("parallel", "arbitrary"),
        name="ffn_up",
    )(x, x, g, sc, sh, w, w, cw, buf)
    return act, tails[tpg - 1::tpg]


def _mm_res_kernel(a_ref, w_ref, x_ref, gt_ref, o_ref):
    o_ref[...] = x_ref[...] + gt_ref[...] * jnp.dot(a_ref[...], w_ref[...], preferred_element_type=F32)


def _mm_res(a, w, x, gt, grp):
    n, d = x.shape
    kdim = a.shape[1]
    tm, tpg, r = min(512, grp["tm"]), grp["tpg"] * (grp["tm"] // min(512, grp["tm"])), gt.shape[1]
    if r > 1:
        gt = gt.reshape(r // tm, tm, d)
        gt_spec = pl.BlockSpec((None, tm, d), lambda i: (i, 0, 0))
    else:
        gt_spec = pl.BlockSpec((None, 1, d), lambda i: (i // tpg, 0, 0))
    return pl.pallas_call(
        _mm_res_kernel,
        grid=(n // tm,),
        in_specs=[pl.BlockSpec((tm, kdim), lambda i: (i, 0)),
                  pl.BlockSpec((kdim, d), lambda i: (0, 0)),
                  pl.BlockSpec((tm, d), lambda i: (i, 0)),
                  gt_spec],
        out_specs=pl.BlockSpec((tm, d), lambda i: (i, 0)),
        out_shape=jax.ShapeDtypeStruct((n, d), F32),
        compiler_params=_cparams("parallel"),
        name="mm_res",
    )(a, w, x, gt)


def _prep_layer(l, P):
    d = D_MODEL
    w_in = P["w_in"][l]
    o_mif = 4 * BRANCH
    o_sb = o_mif + 2 * N_HEADS
    o_aq = o_sb + 3 * BRANCH
    o_rw = o_aq + 3 * BRANCH
    o_g = o_rw + RW_COLS
    w_in_p = jnp.concatenate([
        w_in[:, o_g:o_g + 4 * d],
        w_in[:, o_rw:o_g], w_in[:, o_mif:o_sb], jnp.zeros((d, Z_RWKV_W - RW_COLS - 2 * N_HEADS), F32),
        w_in[:, 0:o_mif], w_in[:, o_sb:o_aq], w_in[:, o_aq:o_rw]], axis=1).astype(BF16)
    assert w_in_p.shape[1] == Z_WIDTH
    bias_if = jnp.concatenate([P["m_bi"][l], P["m_bf"][l], jnp.zeros((LANES - 2 * N_HEADS,), F32)])[None, :]
    zw = Z_RWKV_W
    wlr = jnp.zeros((RW_LORA, 3 * BRANCH), F32)
    wlr = wlr.at[0:32, 0:BRANCH].set(P["r_wB"][l])
    wlr = wlr.at[32:64, BRANCH:2 * BRANCH].set(P["r_aB"][l])
    wlr = wlr.at[64:128, 2 * BRANCH:].set(P["r_gB"][l])
    row = lambda t: t[None, :]
    rw = dict(mu=jnp.pad(P["r_mu"][l], (0, zw - RW_COLS))[None, :], wlr=wlr.astype(BF16),
              w0=row(P["r_w0"][l]), a0=row(P["r_a0"][l]), kk=row(P["r_kk"][l]), ka=row(P["r_ka"][l]),
              rk=row(P["r_rk"][l]))
    bf = lambda name: P[name][l].astype(BF16)
    return dict(
        w_ada=bf("w_ada"), b_ada=row(P["b_ada"][l]), w_in=w_in_p, bias_if=bias_if,
        norm_mix_g=row(P["norm_mix_g"][l]), norm_ffn_g=row(P["norm_ffn_g"][l]),
        m_norm_g=row(P["m_norm_g"][l]), s_conv=P["s_conv"][l],
        qg=row(jnp.tile(P["a_qnorm"][l], N_HEADS)), kg=row(jnp.tile(P["a_knorm"][l], N_HEADS)),
        rw=rw, r_norm_g=row(P["r_norm_g"][l]),
        projs=(bf("m_proj"), bf("s_proj"), bf("a_proj"), bf("r_proj")), w_out=bf("w_out"),
        f_up=bf("f_up"), f_conv=P["f_conv"][l], f_down=bf("f_down"))


def _layer(x, mod, W, state, tables, attend, grp):
    mc, mn, mm, sbuf, rs, rshift, fbuf = state
    b_, l_ = grp["B"], grp["L"]
    sh_m, sc_m, gt_m, sh_f, sc_f, gt_f = mod
    z = _norm_mod_matmul(x, W["norm_mix_g"], sc_m, sh_m, W["w_in"], grp, Z_WIDTH // 3)
    ml, mc, mn, mm = _mlstm(z, W["bias_if"], W["m_norm_g"], mc, mn, mm, grp)
    sc, sbuf = _sconv(z, W["s_conv"], sbuf, grp)
    q, k = _qk_prep(z, W["qg"], W["kg"], tables, grp)
    v = z[:, Z_MOBA + 2 * BRANCH:Z_MOBA + 3 * BRANCH]
    mb = attend(q, k, z, v)
    zw = Z_RWKV_W
    prev = jnp.pad(rshift, ((0, 0), (0, zw - RW_COLS)))[:, None, :]
    r, w, k2, vv, kk, kka, bonus, g = _rwkv_prep(z, prev, W["rw"], grp)
    y, rs_t = _rwkv_scan(r, w, k2, vv, kk, kka, rs, grp)
    rshift = z.reshape(b_, l_, Z_WIDTH)[:, -1, Z_RWKV:Z_RWKV + RW_COLS]
    x = _merge(ml, sc, mb, y, bonus, g, W["r_norm_g"], z, x, gt_m, W["projs"], W["w_out"], grp)
    if sc_f.shape[1] == 1 and l_ % FFN_HALO == 0:
        act, fbuf = _ffn_up(x, W["norm_ffn_g"], sc_f, sh_f, W["f_up"], W["f_conv"], fbuf, grp)
    else:
        up = _norm_mod_matmul(x, W["norm_ffn_g"], sc_f, sh_f, W["f_up"], grp, 1408)
        act = _ffn_act(up, W["f_conv"], fbuf, grp)
        fbuf = up.reshape(b_, l_, 2 * D_FF)[:, l_ - (CONV_WIDTH - 1):, :D_FF]
    x = _mm_res(act, W["f_down"], x, gt_f, grp)
    return x, (k, v), (mc, mn, mm, sbuf, rs_t, rshift, fbuf)


def kernel(x_prompt, x_sample, c_prompt, c_sample, cache_k, cache_v, page_table, state_mlstm_c, state_mlstm_n, state_mlstm_m, state_conv, state_rwkv, state_rwkv_shift, state_ffn_conv, norm_mix_g, norm_ffn_g, w_ada, b_ada, w_in, m_bi, m_bf, m_norm_g, m_proj, s_conv, s_proj, a_qnorm, a_knorm, a_proj, r_mu, r_w0, r_wB, r_a0, r_aB, r_gB, r_kk, r_ka, r_rk, r_norm_g, r_proj, w_out, f_up, f_conv, f_down):
    P = dict(norm_mix_g=norm_mix_g, norm_ffn_g=norm_ffn_g, w_ada=w_ada, b_ada=b_ada, w_in=w_in,
             m_bi=m_bi, m_bf=m_bf, m_norm_g=m_norm_g, m_proj=m_proj, s_conv=s_conv, s_proj=s_proj,
             a_qnorm=a_qnorm, a_knorm=a_knorm, a_proj=a_proj, r_mu=r_mu, r_w0=r_w0, r_wB=r_wB,
             r_a0=r_a0, r_aB=r_aB, r_gB=r_gB, r_kk=r_kk, r_ka=r_ka, r_rk=r_rk, r_norm_g=r_norm_g,
             r_proj=r_proj, w_out=w_out, f_up=f_up, f_conv=f_conv, f_down=f_down)
    depth = w_in.shape[0]
    bp, lp, d = x_prompt.shape
    bs, ls, _ = x_sample.shape
    n_s = bs * ls
    past = page_table.shape[1] * PAGE_SIZE
    tm_p = _pick_tile(lp, 1024)
    grp_p = dict(B=bp, L=lp, tm=tm_p, tpg=lp // tm_p)
    grp_s = dict(B=bs, L=ls, tm=n_s, tpg=1)
    assert n_s % 8 == 0 and n_s <= 1024

    zeros = lambda *s: jnp.zeros(s, F32)
    st_p = (zeros(bp, N_HEADS, HEAD_DIM, HEAD_DIM), zeros(bp, N_HEADS, HEAD_DIM), zeros(bp, N_HEADS),
            zeros(bp, CONV_WIDTH - 1, BRANCH), zeros(bp, N_HEADS, HEAD_DIM, HEAD_DIM), zeros(bp, RW_COLS),
            zeros(bp, CONV_WIDTH - 1, D_FF))
    tab_p = _rope_tables(jnp.arange(lp, dtype=jnp.int32))
    tab_s = tuple(jnp.tile(t, (bs, 1)) for t in _rope_tables(past + jnp.arange(ls, dtype=jnp.int32)))

    c_all = jnp.concatenate([c_prompt, c_sample], axis=0)
    hp = x_prompt.reshape(bp * lp, d)
    hs = x_sample.reshape(n_s, d)
    kv_p, kv_s, sts_p, sts_s = [], [], [], []
    for l in range(depth):
        W = _prep_layer(l, P)
        mod = _ada(c_all, W["w_ada"], W["b_ada"])
        mods = [mod[:, j * d:(j + 1) * d] for j in range(6)]
        mod_p = [m[:bp][:, None, :] for m in mods]
        mod_s = [jnp.repeat(m[bp:], ls, axis=0)[None] for m in mods]

        attend_p = lambda q, k, z, v: _moba_prompt(q, k, z, grp_p)
        hp, kv, st = _layer(hp, mod_p, W, st_p, tab_p, attend_p, grp_p)
        kv_p.append(kv)
        sts_p.append(st)

        prev = (state_mlstm_c[l], state_mlstm_n[l], state_mlstm_m[l], state_conv[l], state_rwkv[l],
                state_rwkv_shift[l], state_ffn_conv[l])
        attend_s = lambda q, k, z, v: _moba_sample(q, k, v, cache_k, cache_v, page_table, l, grp_s)
        hs, kv, st = _layer(hs, mod_s, W, prev, tab_s, attend_s, grp_s)
        kv_s.append(kv)
        sts_s.append(st)

    heads = lambda t, b_, l_: t.reshape(b_, l_, N_HEADS, HEAD_DIM)
    k_prompt = jnp.stack([heads(k, bp, lp) for k, _ in kv_p])
    v_prompt = jnp.stack([heads(v, bp, lp) for _, v in kv_p])
    k_sample = jnp.stack([heads(k, bs, ls) for k, _ in kv_s])
    v_sample = jnp.stack([heads(v, bs, ls) for _, v in kv_s])
    stack = lambda sts: [jnp.stack(t) for t in zip(*sts)]
    return (hp.reshape(bp, lp, d), hs.reshape(bs, ls, d), k_prompt, v_prompt, k_sample, v_sample,
            *stack(sts_p), *stack(sts_s))
```

```python
import functools
import math

import jax
import jax.numpy as jnp
from jax import lax
from jax.experimental import pallas as pl
from jax.experimental.pallas import tpu as pltpu

F32 = jnp.float32
BF16 = jnp.bfloat16
HIGHEST = lax.Precision.HIGHEST

D_MODEL = 1024
HEAD_DIM = 64
N_HEADS = 4
BRANCH = N_HEADS * HEAD_DIM
CONV_WIDTH = 3
MB_BLOCK = 256
MB_TOPK = 3
PAGE_SIZE = 128
ROPE_DIMS = HEAD_DIM // 4
ROPE_THETA = 500000.0
RW_LORA = 128
RW_COLS = 3 * BRANCH + RW_LORA
D_FF = 11 * D_MODEL // 4
MLSTM_CHUNK = 128
RWKV_CHUNK = 64
FFN_HALO = 16
SCAN_BATCH = 4
RWKV_BATCH = 4
NORM_EPS = 1e-6
GN_EPS = 64e-5
LANES = 128

Z_GATES = 0
Z_RWKV = 4096
Z_RWKV_W = 1024
Z_MIF = Z_RWKV + RW_COLS
Z_MLSTM = 5120
Z_SCONV = 6144
Z_MOBA = 6912
Z_WIDTH = 7680

VMEM_LIMIT = 48 * 1024 * 1024


def _cparams(*sem):
    return pltpu.CompilerParams(dimension_semantics=sem, vmem_limit_bytes=VMEM_LIMIT)


def _sigmoid(x):
    return 0.5 * jnp.tanh(0.5 * x) + 0.5


def _head_blockdiag(n):
    r = lax.broadcasted_iota(jnp.int32, (n, n), 0) // HEAD_DIM
    c = lax.broadcasted_iota(jnp.int32, (n, n), 1) // HEAD_DIM
    return (r == c).astype(F32)


def _head_sum(x, bd):
    return jnp.dot(x, bd, precision=HIGHEST, preferred_element_type=F32)


def _dot_nt(a, b):
    return lax.dot_general(a, b, (((1,), (1,)), ((), ())), preferred_element_type=F32)


def _dot_tn(a, b):
    return lax.dot_general(a, b, (((0,), (0,)), ((), ())), preferred_element_type=F32)


def _pick_tile(n, cap):
    t = cap
    while n % t:
        t //= 2
    return t


def _ada_kernel(c_ref, w_ref, b_ref, o_ref):
    c = c_ref[...]
    a = (c * _sigmoid(c)).astype(BF16)
    o_ref[...] = jnp.dot(a, w_ref[...], preferred_element_type=F32) + b_ref[...]


def _ada(c_all, w, b):
    m, d = c_all.shape
    n = w.shape[1]
    tn = 1536
    return pl.pallas_call(
        _ada_kernel,
        grid=(n // tn,),
        in_specs=[pl.BlockSpec((m, d), lambda j: (0, 0)),
                  pl.BlockSpec((d, tn), lambda j: (0, j)),
                  pl.BlockSpec((1, tn), lambda j: (0, j))],
        out_specs=pl.BlockSpec((m, tn), lambda j: (0, j)),
        out_shape=jax.ShapeDtypeStruct((m, n), F32),
        compiler_params=_cparams("parallel"),
        name="ada",
    )(c_all, w, b)


def _nmm_kernel(x_ref, g_ref, sc_ref, sh_ref, w_ref, o_ref, h_sc):
    @pl.when(pl.program_id(1) == 0)
    def _():
        x = x_ref[...]
        y = x * lax.rsqrt(jnp.mean(x * x, axis=-1, keepdims=True) + NORM_EPS) * g_ref[...]
        h_sc[...] = (y * (1.0 + sc_ref[...]) + sh_ref[...]).astype(BF16)

    o_ref[...] = jnp.dot(h_sc[...], w_ref[...], preferred_element_type=F32)


def _norm_mod_matmul(x, g, sc, sh, w, grp, tn):
    n, d = x.shape
    nout = w.shape[1]
    tm, tpg, r = grp["tm"], grp["tpg"], sc.shape[1]
    return pl.pallas_call(
        _nmm_kernel,
        grid=(n // tm, nout // tn),
        in_specs=[pl.BlockSpec((tm, d), lambda i, j: (i, 0)),
                  pl.BlockSpec((1, d), lambda i, j: (0, 0)),
                  pl.BlockSpec((None, r, d), lambda i, j: (i // tpg, 0, 0)),
                  pl.BlockSpec((None, r, d), lambda i, j: (i // tpg, 0, 0)),
                  pl.BlockSpec((d, tn), lambda i, j: (0, j))],
        out_specs=pl.BlockSpec((tm, tn), lambda i, j: (i, j)),
        out_shape=jax.ShapeDtypeStruct((n, nout), F32),
        scratch_shapes=[pltpu.VMEM((tm, d), BF16)],
        compiler_params=_cparams("parallel", "arbitrary"),
        name="norm_mod_matmul",
    )(x, g, sc, sh, w)


def _log_sigmoid(x):
    return jnp.minimum(x, 0.0) - jnp.log(1.0 + jnp.exp(-jnp.abs(x)))


def _mlstm_kernel(zq_ref, zif_ref, bias_ref, g_ref, c0_ref, n0_ref, m0_ref,
                  act_ref, c_out, n_out, m_out, c_sc, n_sc, m_sc, *, chunk, bb):
    t_ = chunk
    ci = pl.program_id(1)

    @pl.when(ci == 0)
    def _():
        c_sc[...] = c0_ref[...]
        n_sc[...] = n0_ref[...]
        m_sc[...] = m0_ref[...]

    row = lax.broadcasted_iota(jnp.int32, (t_, t_), 0)
    col = lax.broadcasted_iota(jnp.int32, (t_, t_), 1)
    causal = col <= row
    eye = row == col
    row1 = lax.broadcasted_iota(jnp.int32, (t_, 1), 0)
    chains = [(b, h) for b in range(bb) for h in range(N_HEADS)]
    idx = range(len(chains))
    each = lambda f, *xs: [f(*a) for a in zip(*xs)]
    mmf = lambda a, b: jnp.dot(a, b, preferred_element_type=F32)
    gates = [zif_ref[b] + bias_ref[...] for b in range(bb)]
    logf = each(_log_sigmoid, gates)
    part = lambda j: [zq_ref[b, :, j * BRANCH + h * HEAD_DIM:j * BRANCH + (h + 1) * HEAD_DIM] for b, h in chains]
    q, k, v, o = part(0), [x * (HEAD_DIM ** -0.5) for x in part(1)], part(2), part(3)
    li_col = [gates[b][:, h:h + 1] for b, h in chains]
    lf_col = [logf[b][:, N_HEADS + h:N_HEADS + h + 1] for b, h in chains]
    li_row = each(lambda x: jnp.sum(jnp.where(eye, x, 0.0), axis=0, keepdims=True), li_col)
    lf_row = each(lambda x: jnp.sum(jnp.where(eye, x, 0.0), axis=0, keepdims=True), lf_col)
    b_col = each(lambda x: jnp.sum(jnp.where(causal, x, 0.0), axis=1, keepdims=True), lf_row)
    b_row = each(lambda x: jnp.sum(jnp.where(row <= col, x, 0.0), axis=0, keepdims=True), lf_col)
    m_prev = [m_sc[b, h][:, 0:1] for b, h in chains]
    d = each(lambda bc, br, lr: jnp.where(causal, bc - br + lr, -jnp.inf), b_col, b_row, li_row)
    inter = each(lambda bc, m: bc + m, b_col, m_prev)
    mt = each(lambda x, y: jnp.maximum(jnp.max(x, axis=1, keepdims=True), y), d, inter)
    qb, kb, vb = (each(lambda x: x.astype(BF16), t) for t in (q, k, v))
    w = each(lambda x, m, a, b: jnp.exp(x - m) * _dot_nt(a, b), d, mt, qb, kb)
    a_int = each(lambda x, m: jnp.exp(x - m), inter, mt)
    c = [c_sc[b, h] for b, h in chains]
    n = [n_sc[b, h] for b, h in chains]
    num = each(lambda w_, v_, a, q_, c_: mmf(w_.astype(BF16), v_) + a * mmf(q_, c_.astype(BF16)), w, vb, a_int, qb, c)
    den = each(lambda w_, a, q_, n_: jnp.sum(w_, axis=1, keepdims=True) + a * jnp.sum(q_ * n_, axis=1, keepdims=True),
               w, a_int, q, n)
    hh = each(lambda x, y, m: x / jnp.maximum(jnp.abs(y), jnp.exp(-m)), num, den, mt)
    hn = each(lambda x: x * lax.rsqrt(jnp.mean(x * x, axis=-1, keepdims=True) + NORM_EPS), hh)
    b_end = each(lambda x: jnp.sum(jnp.where(row1 == t_ - 1, x, 0.0), axis=0, keepdims=True), b_col)
    g_col = each(lambda e, bc, l: e - bc + l, b_end, b_col, li_col)
    m_new = each(lambda e, m, g_: jnp.maximum(e + m, jnp.max(g_, axis=0, keepdims=True)), b_end, m_prev, g_col)
    wk = each(lambda g_, m, k_: jnp.exp(g_ - m) * k_, g_col, m_new, k)
    decay = each(lambda e, m, mn: jnp.exp(e + m - mn), b_end, m_prev, m_new)
    c_new = each(lambda dc, c_, wk_, v_: dc * c_ + _dot_tn(wk_.astype(BF16), v_), decay, c, wk, vb)
    for i in idx:
        b, h = chains[i]
        lo = h * HEAD_DIM
        act_ref[b, :, lo:lo + HEAD_DIM] = hn[i] * g_ref[:, lo:lo + HEAD_DIM] * _sigmoid(o[i])
        c_sc[b, h] = c_new[i]
        n_sc[b, h] = decay[i] * n[i] + jnp.sum(wk[i], axis=0, keepdims=True)
        m_sc[b, h] = jnp.broadcast_to(m_new[i], (1, LANES))

    @pl.when(ci == pl.num_programs(1) - 1)
    def _():
        c_out[...] = c_sc[...]
        n_out[...] = n_sc[...]
        m_out[...] = m_sc[...]


def _mlstm(z, bias_if, g, c0, n0, m0, grp):
    b_, l_ = grp["B"], grp["L"]
    t_ = min(MLSTM_CHUNK, l_)
    nc = l_ // t_
    h_ = N_HEADS
    bb = _pick_tile(b_, SCAN_BATCH)
    n0 = n0.reshape(b_, h_, 1, HEAD_DIM)
    m0 = jnp.broadcast_to(m0.reshape(b_, h_, 1, 1), (b_, h_, 1, LANES))
    z3 = z.reshape(b_, l_, Z_WIDTH)
    state_spec = lambda shp: pl.BlockSpec((bb,) + shp, lambda b, c: (b, 0, 0, 0))
    act, c1, n1, m1 = pl.pallas_call(
        functools.partial(_mlstm_kernel, chunk=t_, bb=bb),
        grid=(b_ // bb, nc),
        in_specs=[pl.BlockSpec((bb, t_, 4 * BRANCH), lambda b, c: (b, c, Z_MLSTM // (4 * BRANCH))),
                  pl.BlockSpec((bb, t_, LANES), lambda b, c: (b, c, Z_MIF // LANES)),
                  pl.BlockSpec((1, LANES), lambda b, c: (0, 0)),
                  pl.BlockSpec((1, BRANCH), lambda b, c: (0, 0)),
                  state_spec((h_, HEAD_DIM, HEAD_DIM)),
                  state_spec((h_, 1, HEAD_DIM)),
                  state_spec((h_, 1, LANES))],
        out_specs=[pl.BlockSpec((bb, t_, BRANCH), lambda b, c: (b, c, 0)),
                   state_spec((h_, HEAD_DIM, HEAD_DIM)),
                   state_spec((h_, 1, HEAD_DIM)),
                   state_spec((h_, 1, LANES))],
        out_shape=[jax.ShapeDtypeStruct((b_, l_, BRANCH), F32),
                   jax.ShapeDtypeStruct((b_, h_, HEAD_DIM, HEAD_DIM), F32),
                   jax.ShapeDtypeStruct((b_, h_, 1, HEAD_DIM), F32),
                   jax.ShapeDtypeStruct((b_, h_, 1, LANES), F32)],
        scratch_shapes=[pltpu.VMEM((bb, h_, HEAD_DIM, HEAD_DIM), F32),
                        pltpu.VMEM((bb, h_, 1, HEAD_DIM), F32),
                        pltpu.VMEM((bb, h_, 1, LANES), F32)],
        compiler_params=_cparams("parallel", "arbitrary"),
        name="mlstm",
    )(z3, z3, bias_if, g, c0, n0, m0)
    return act.reshape(b_ * l_, BRANCH), c1, n1.reshape(b_, h_, HEAD_DIM), m1[:, :, 0, 0]


def _conv3(p, prev2, prev1, w_ref):
    row = lax.broadcasted_iota(jnp.int32, p.shape, 0)
    p1 = jnp.where(row == 0, prev1, pltpu.roll(p, 1, axis=0))
    p2 = jnp.where(row == 0, prev2, jnp.where(row == 1, prev1, pltpu.roll(p, 2, axis=0)))
    return w_ref[0:1, :] * p2 + w_ref[1:2, :] * p1 + w_ref[2:3, :] * p


def _sconv_kernel(sb_ref, sc_ref, sh_ref, w_ref, buf_ref, out_ref, st_ref, carry, *, tm):
    @pl.when(pl.program_id(1) == 0)
    def _():
        carry[...] = buf_ref[0]

    p = sc_ref[...] * sh_ref[...]
    u = _conv3(p, carry[0:1, :], carry[1:2, :], w_ref)
    out_ref[...] = sb_ref[...] * u
    new = sc_ref[tm - 2:tm, :] * sh_ref[tm - 2:tm, :]
    carry[...] = new
    st_ref[0] = new


def _sconv(z, w, buf, grp):
    b_, l_ = grp["B"], grp["L"]
    tm = _pick_tile(l_, 512)
    nt = l_ // tm
    cb = Z_SCONV // BRANCH
    zspec = lambda j: pl.BlockSpec((tm, BRANCH), lambda b, i: (b * nt + i, cb + j))
    return pl.pallas_call(
        functools.partial(_sconv_kernel, tm=tm),
        grid=(b_, nt),
        in_specs=[zspec(0), zspec(1), zspec(2),
                  pl.BlockSpec((CONV_WIDTH, BRANCH), lambda b, i: (0, 0)),
                  pl.BlockSpec((1, CONV_WIDTH - 1, BRANCH), lambda b, i: (b, 0, 0))],
        out_specs=[pl.BlockSpec((tm, BRANCH), lambda b, i: (b * nt + i, 0)),
                   pl.BlockSpec((1, CONV_WIDTH - 1, BRANCH), lambda b, i: (b, 0, 0))],
        out_shape=[jax.ShapeDtypeStruct((b_ * l_, BRANCH), F32),
                   jax.ShapeDtypeStruct((b_, CONV_WIDTH - 1, BRANCH), F32)],
        scratch_shapes=[pltpu.VMEM((CONV_WIDTH - 1, BRANCH), F32)],
        compiler_params=_cparams("parallel", "arbitrary"),
        name="sconv",
    )(z, z, z, w, buf)


def _qk_prep_kernel(q_ref, k_ref, qg_ref, kg_ref, cos_ref, sa_ref, sb_ref, qo_ref, ko_ref):
    bd = _head_blockdiag(BRANCH)
    cos, sa, sb = cos_ref[...], sa_ref[...], sb_ref[...]
    half = ROPE_DIMS // 2

    def prep(x, g):
        y = x * lax.rsqrt(_head_sum(x * x, bd) * (1.0 / HEAD_DIM) + NORM_EPS) * g
        return y * cos + pltpu.roll(y, BRANCH - half, axis=1) * sa + pltpu.roll(y, half, axis=1) * sb

    qo_ref[...] = prep(q_ref[...], qg_ref[...])
    ko_ref[...] = prep(k_ref[...], kg_ref[...])


def _rope_tables(pos):
    half = ROPE_DIMS // 2
    inv = jnp.exp(-math.log(ROPE_THETA) * jnp.arange(0, ROPE_DIMS, 2, dtype=F32) / ROPE_DIMS)
    ang = pos.astype(F32)[:, None] * inv[None, :]
    cos, sin = jnp.cos(ang), jnp.sin(ang)
    n = pos.shape[0]
    rest = HEAD_DIM - ROPE_DIMS
    c_h = jnp.concatenate([cos, cos, jnp.ones((n, rest), F32)], axis=1)
    sa_h = jnp.concatenate([-sin, jnp.zeros((n, half + rest), F32)], axis=1)
    sb_h = jnp.concatenate([jnp.zeros((n, half), F32), sin, jnp.zeros((n, rest), F32)], axis=1)
    tile = lambda t: jnp.tile(t, (1, N_HEADS))
    return tile(c_h), tile(sa_h), tile(sb_h)


def _qk_prep(z, qg, kg, tables, grp):
    n = grp["B"] * grp["L"]
    ltab = tables[0].shape[0]
    tm = _pick_tile(ltab, 512)
    npos = ltab // tm
    cb = Z_MOBA // BRANCH
    tspec = pl.BlockSpec((tm, BRANCH), lambda i: (i % npos, 0))
    gspec = pl.BlockSpec((1, BRANCH), lambda i: (0, 0))
    ospec = pl.BlockSpec((tm, BRANCH), lambda i: (i, 0))
    return pl.pallas_call(
        _qk_prep_kernel,
        grid=(n // tm,),
        in_specs=[pl.BlockSpec((tm, BRANCH), lambda i: (i, cb)),
                  pl.BlockSpec((tm, BRANCH), lambda i: (i, cb + 1)),
                  gspec, gspec, tspec, tspec, tspec],
        out_specs=[ospec, ospec],
        out_shape=[jax.ShapeDtypeStruct((n, BRANCH), F32)] * 2,
        compiler_params=_cparams("parallel"),
        name="qk_prep",
    )(z, z, qg, kg, *tables)


def _moba_prompt_kernel(q_ref, k_ref, v_ref, o_ref, kmean, kx, vx, *, nb):
    qi = pl.program_id(1)
    tq = MB_BLOCK
    half = LANES // 2
    masked = float(jnp.finfo(BF16).min)
    lane = lax.broadcasted_iota(jnp.int32, (tq, LANES), 1)
    low = lane < half

    @pl.when(qi == 0)
    def _():
        kmean[...] = jnp.zeros_like(kmean)
        for n in range(nb):
            rows = slice(n * MB_BLOCK, (n + 1) * MB_BLOCK)
            kmean[n:n + 1, :] = jnp.mean(k_ref[rows, :], axis=0, keepdims=True)
            onehot = (lane == half + n).astype(F32)
            for h in range(N_HEADS):
                pair = slice((h // 2) * LANES, (h // 2 + 1) * LANES)
                k2, v2 = k_ref[rows, pair], v_ref[rows, pair]
                if h % 2:
                    k2, v2 = pltpu.roll(k2, half, axis=1), pltpu.roll(v2, half, axis=1)
                kx[h, rows, :] = jnp.where(low, k2, onehot).astype(BF16)
                vx[h, rows, :] = jnp.where(low, v2, 1.0).astype(BF16)

    row = lax.broadcasted_iota(jnp.int32, (tq, tq), 0)
    col = lax.broadcasted_iota(jnp.int32, (tq, tq), 1)
    scale = HEAD_DIM ** -0.5
    own0 = pl.multiple_of(qi * MB_BLOCK, MB_BLOCK)
    heads = range(N_HEADS)
    sub = 8
    assert nb <= sub
    blk = lax.broadcasted_iota(jnp.int32, (sub, tq), 0)
    place = (lax.broadcasted_iota(jnp.int32, (sub, LANES), 1)
             == lax.broadcasted_iota(jnp.int32, (sub, LANES), 0) + half).astype(BF16)
    qx = []
    for h in heads:
        hs = slice(h * HEAD_DIM, (h + 1) * HEAD_DIM)
        s = lax.dot_general(kmean[0:sub, hs], q_ref[:, hs], (((1,), (1,)), ((), ())),
                            precision=HIGHEST, preferred_element_type=F32)
        valid = blk < qi
        s = jnp.where(valid, s, -jnp.inf)
        rank = jnp.zeros((sub, tq), jnp.int32)
        for m in range(nb):
            sm = s[m:m + 1, :]
            rank += ((sm > s) | ((sm == s) & (m < blk))).astype(jnp.int32)
        keep = ((valid & (rank < MB_TOPK)) | (blk == qi)).astype(BF16)
        keep_q = _dot_tn(keep, place)
        q2 = q_ref[:, (h // 2) * LANES:(h // 2 + 1) * LANES] * scale
        if h % 2:
            q2 = pltpu.roll(q2, half, axis=1)
        qx.append(jnp.where(low, q2, jnp.where(keep_q > 0.5, 0.0, masked)).astype(BF16))
    sc = [jnp.where(col <= row, _dot_nt(qx[h], kx[h, pl.ds(own0, MB_BLOCK), :]), -jnp.inf) for h in heads]
    m_i = [jnp.max(sc[h], axis=1, keepdims=True) for h in heads]
    p = [jnp.exp(sc[h] - m_i[h]).astype(BF16) for h in heads]
    acc = [jnp.dot(p[h], vx[h, pl.ds(own0, MB_BLOCK), :], preferred_element_type=F32) for h in heads]

    def body(n, carry):
        m_i, acc = carry
        k0 = pl.multiple_of(n * MB_BLOCK, MB_BLOCK)
        sc = [_dot_nt(qx[h], kx[h, pl.ds(k0, MB_BLOCK), :]) for h in heads]
        m_new = [jnp.maximum(m_i[h], jnp.max(sc[h], axis=1, keepdims=True)) for h in heads]
        p = [jnp.exp(sc[h] - m_new[h]).astype(BF16) for h in heads]
        pv = [jnp.dot(p[h], vx[h, pl.ds(k0, MB_BLOCK), :], preferred_element_type=F32) for h in heads]
        return m_new, [jnp.exp(m_i[h] - m_new[h]) * acc[h] + pv[h] for h in heads]

    m_i, acc = lax.fori_loop(0, qi, body, (m_i, acc))
    for h in heads:
        o_ref[:, h * HEAD_DIM:(h + 1) * HEAD_DIM] = (acc[h] / pltpu.roll(acc[h], half, axis=1))[:, :half]


def _moba_prompt(q, k, z, grp):
    b_, l_ = grp["B"], grp["L"]
    assert l_ % MB_BLOCK == 0 and l_ // MB_BLOCK <= LANES // 2
    nb = l_ // MB_BLOCK
    return pl.pallas_call(
        functools.partial(_moba_prompt_kernel, nb=nb),
        grid=(b_, nb),
        in_specs=[pl.BlockSpec((MB_BLOCK, BRANCH), lambda b, i: (b * nb + i, 0)),
                  pl.BlockSpec((l_, BRANCH), lambda b, i: (b, 0)),
                  pl.BlockSpec((l_, BRANCH), lambda b, i: (b, Z_MOBA // BRANCH + 2))],
        out_specs=pl.BlockSpec((MB_BLOCK, BRANCH), lambda b, i: (b * nb + i, 0)),
        out_shape=jax.ShapeDtypeStruct((b_ * l_, BRANCH), F32),
        scratch_shapes=[pltpu.VMEM((LANES, BRANCH), F32),
                        pltpu.VMEM((N_HEADS, l_, LANES), BF16),
                        pltpu.VMEM((N_HEADS, l_, LANES), BF16)],
        compiler_params=_cparams("parallel", "arbitrary"),
        name="moba_prompt",
    )(q, k, z)


def _moba_sample_kernel(pt_ref, q_ref, kn_ref, vn_ref, *refs, nblk, dec, gsz):
    ppb = MB_BLOCK // PAGE_SIZE
    k_refs, v_refs = refs[:ppb * gsz], refs[ppb * gsz:2 * ppb * gsz]
    o_ref, kmean_s, m_s, l_s, o_s = refs[2 * ppb * gsz:]
    step = pl.program_id(1)
    nq = N_HEADS * dec
    scale = HEAD_DIM ** -0.5
    lane = lax.broadcasted_iota(jnp.int32, (nq, LANES), 1)
    lane_c = lax.broadcasted_iota(jnp.int32, (BRANCH, LANES), 1)

    @pl.when(step == 0)
    def _():
        kmean_s[...] = jnp.zeros_like(kmean_s)
        m_s[...] = jnp.full_like(m_s, -jnp.inf)
        l_s[...] = jnp.zeros_like(l_s)

    row_head = jnp.concatenate([jnp.full((dec, BRANCH), h, jnp.int32) for h in range(N_HEADS)], axis=0)
    lane_head = lax.broadcasted_iota(jnp.int32, (nq, BRANCH), 1) // HEAD_DIM
    qbd = jnp.where(row_head == lane_head, jnp.concatenate([q_ref[...]] * N_HEADS, axis=0), 0.0)
    qbd_b = qbd.astype(BF16)
    kmean, m_all, l_all = kmean_s[...], m_s[...], l_s[...]
    blocks = range(gsz)
    kt = [jnp.concatenate([k_refs[ppb * g + j][...] for j in range(ppb)], axis=1) for g in blocks]
    s = [jnp.dot(qbd_b, kt[g].astype(BF16), preferred_element_type=F32) * scale for g in blocks]
    mx = [jnp.max(s[g], axis=1, keepdims=True) for g in blocks]
    p = [jnp.exp(s[g] - mx[g]) for g in blocks]
    vt = [jnp.concatenate([v_refs[ppb * g + j][...] for j in range(ppb)], axis=1) for g in blocks]
    o = [_dot_nt(p[g].astype(BF16), vt[g].astype(BF16)) for g in blocks]
    for g in blocks:
        blk = step * gsz + g
        o_s[blk] = o[g]
        kmean = jnp.where(lane_c == blk, jnp.mean(kt[g], axis=1, keepdims=True), kmean)
        m_all = jnp.where(lane == blk, mx[g], m_all)
        l_all = jnp.where(lane == blk, jnp.sum(p[g], axis=1, keepdims=True), l_all)
    kmean_s[...] = kmean
    m_s[...] = m_all
    l_s[...] = l_all

    @pl.when(step == pl.num_programs(1) - 1)
    def _():
        zpad = jnp.zeros((LANES - dec, BRANCH), F32)
        knew = jnp.concatenate([kn_ref[...], zpad], axis=0)
        vnew = jnp.concatenate([vn_ref[...], zpad], axis=0)
        rowq = jnp.concatenate([lax.broadcasted_iota(jnp.int32, (dec, LANES), 0)] * N_HEADS, axis=0)
        lane_f = lane.astype(F32)
        s = jnp.dot(qbd, kmean_s[...], precision=HIGHEST, preferred_element_type=F32)
        s = jnp.where(lane < nblk, s, -jnp.inf)
        sel = jnp.zeros((nq, LANES), jnp.bool_)
        for _ in range(MB_TOPK):
            mxv = jnp.max(s, axis=1, keepdims=True)
            idx = jnp.min(jnp.where(s == mxv, lane_f, 2.0 * LANES), axis=1, keepdims=True)
            pick = lane_f == idx
            sel = sel | pick
            s = jnp.where(pick, -jnp.inf, s)
        sel = sel & (lane < nblk)
        s_own = jnp.where(lane <= rowq, _dot_nt(qbd_b, knew.astype(BF16)) * scale, -jnp.inf)
        m_own = jnp.max(s_own, axis=1, keepdims=True)
        p_own = jnp.exp(s_own - m_own)
        l_own = jnp.sum(p_own, axis=1, keepdims=True)
        o_own = jnp.dot(p_own.astype(BF16), vnew.astype(BF16), preferred_element_type=F32)
        mb = m_s[...]
        mtot = jnp.maximum(jnp.max(jnp.where(sel, mb, -jnp.inf), axis=1, keepdims=True), m_own)
        wgt = jnp.where(sel, jnp.exp(mb - mtot), 0.0)
        a_own = jnp.exp(m_own - mtot)
        ltot = jnp.sum(wgt * l_s[...], axis=1, keepdims=True) + l_own * a_own
        acc = o_own * a_own
        for j in range(nblk):
            acc = acc + wgt[:, j:j + 1] * o_s[j]
        res = acc / ltot
        for h in range(N_HEADS):
            hs = slice(h * HEAD_DIM, (h + 1) * HEAD_DIM)
            o_ref[:, hs] = res[h * dec:(h + 1) * dec, hs]


def _moba_sample(q, k, v, pool_k, pool_v, page_table, layer, grp):
    b_, dec = grp["B"], grp["L"]
    n_pages = page_table.shape[1]
    ppb = MB_BLOCK // PAGE_SIZE
    assert n_pages % ppb == 0
    nblk = n_pages // ppb
    assert MB_TOPK <= nblk <= LANES and dec <= LANES and dec % 8 == 0
    gsz = max(g for g in range(1, 17) if nblk % g == 0)
    as_pages = lambda t: t.transpose(0, 1, 3, 4, 2).reshape(t.shape[0], t.shape[1], BRANCH, PAGE_SIZE)
    pool_k, pool_v = as_pages(pool_k), as_pages(pool_v)
    pt = page_table.reshape(-1)

    def page(j):
        return pl.BlockSpec((None, None, BRANCH, PAGE_SIZE),
                            lambda b, s, pt: (layer, pt[b * n_pages + s * gsz * ppb + j], 0, 0))

    pages = [page(j) for j in range(gsz * ppb)]
    new = pl.BlockSpec((dec, BRANCH), lambda b, s, pt: (b, 0))
    nq = N_HEADS * dec
    return pl.pallas_call(
        functools.partial(_moba_sample_kernel, nblk=nblk, dec=dec, gsz=gsz),
        grid_spec=pltpu.PrefetchScalarGridSpec(
            num_scalar_prefetch=1,
            grid=(b_, nblk // gsz),
            in_specs=[new, new, new] + pages + pages,
            out_specs=new,
            scratch_shapes=[pltpu.VMEM((BRANCH, LANES), F32),
                            pltpu.VMEM((nq, LANES), F32),
                            pltpu.VMEM((nq, LANES), F32),
                            pltpu.VMEM((nblk, nq, BRANCH), F32)]),
        out_shape=jax.ShapeDtypeStruct((b_ * dec, BRANCH), F32),
        compiler_params=_cparams("parallel", "arbitrary"),
        name="moba_sample",
    )(pt, q, k, v, *([pool_k] * (gsz * ppb)), *([pool_v] * (gsz * ppb)))


def _rwkv_prep_kernel(z_ref, prev_ref, mu_ref, wlr_ref, w0_ref, a0_ref, kks_ref, kas_ref, rk_ref,
                      r_o, w_o, k_o, v_o, kk_o, kka_o, bonus_o, g_o, carry, *, tm):
    @pl.when(pl.program_id(1) == 0)
    def _():
        carry[...] = prev_ref[0]

    z = z_ref[...]
    row = lax.broadcasted_iota(jnp.int32, z.shape, 0)
    zs = jnp.where(row == 0, carry[...], pltpu.roll(z, 1, axis=0))
    carry[...] = z_ref[tm - 1:tm, :]
    zz = z + mu_ref[...] * (zs - z)
    r = zz[:, 0:BRANCH]
    k = zz[:, BRANCH:2 * BRANCH]
    v = zz[:, 2 * BRANCH:3 * BRANCH]
    lr = zz[:, 3 * BRANCH:3 * BRANCH + RW_LORA]
    lane = lax.broadcasted_iota(jnp.int32, lr.shape, 1)
    lr_in = jnp.where(lane < 32, jnp.tanh(lr), jnp.where(lane < 64, lr, _sigmoid(lr)))
    lo = jnp.dot(lr_in.astype(BF16), wlr_ref[...], preferred_element_type=F32)
    log_decay = -math.exp(-0.5) * _sigmoid(w0_ref[...] + lo[:, 0:BRANCH])
    a = _sigmoid(a0_ref[...] + lo[:, BRANCH:2 * BRANCH])
    bd = _head_blockdiag(BRANCH)
    kk = k * kks_ref[...]
    kk = kk * lax.rsqrt(jnp.maximum(_head_sum(kk * kk, bd), 1e-24))
    k2 = k * (1.0 + (a - 1.0) * kas_ref[...])
    r_o[...] = r
    w_o[...] = log_decay
    k_o[...] = k2
    v_o[...] = v
    kk_o[...] = kk
    kka_o[...] = kk * a
    bonus_o[...] = _head_sum(r * k2 * rk_ref[...], bd) * v
    g_o[...] = lo[:, 2 * BRANCH:3 * BRANCH]


def _rwkv_prep(z, prev, p, grp):
    b_, l_ = grp["B"], grp["L"]
    tm = _pick_tile(l_, 256)
    nt = l_ // tm
    zw = Z_RWKV_W
    vec = pl.BlockSpec((1, BRANCH), lambda b, i: (0, 0))
    ospec = pl.BlockSpec((tm, BRANCH), lambda b, i: (b * nt + i, 0))
    return pl.pallas_call(
        functools.partial(_rwkv_prep_kernel, tm=tm),
        grid=(b_, nt),
        in_specs=[pl.BlockSpec((tm, zw), lambda b, i: (b * nt + i, Z_RWKV // zw)),
                  pl.BlockSpec((1, 1, zw), lambda b, i: (b, 0, 0)),
                  pl.BlockSpec((1, zw), lambda b, i: (0, 0)),
                  pl.BlockSpec((RW_LORA, 3 * BRANCH), lambda b, i: (0, 0)),
                  vec, vec, vec, vec, vec],
        out_specs=[ospec] * 8,
        out_shape=[jax.ShapeDtypeStruct((b_ * l_, BRANCH), F32)] * 8,
        scratch_shapes=[pltpu.VMEM((1, zw), F32)],
        compiler_params=_cparams("parallel", "arbitrary"),
        name="rwkv_prep",
    )(z, prev, p["mu"], p["wlr"], p["w0"], p["a0"], p["kk"], p["ka"], p["rk"])


def _rwkv_chunk_kernel(r_ref, lw_ref, k_ref, v_ref, kk_ref, kka_ref, h0_ref, y_ref, h_out, h_sc, *, chunk, bb):
    t_ = chunk
    ci = pl.program_id(1)

    head_blocks = [slice(h * HEAD_DIM, (h + 1) * HEAD_DIM) for h in range(N_HEADS)]

    @pl.when(ci == 0)
    def _():
        h_sc[...] = jnp.zeros_like(h_sc)
        for b in range(bb):
            for h, hs in enumerate(head_blocks):
                h_sc[b, hs, hs] = h0_ref[b, h]

    t4 = N_HEADS * t_
    rows = list(range(bb))
    each = lambda f, *xs: [f(*a) for a in zip(*xs)]
    tri = (lax.broadcasted_iota(jnp.int32, (t_, t_), 0) >= lax.broadcasted_iota(jnp.int32, (t_, t_), 1)).astype(F32)
    row1 = lax.broadcasted_iota(jnp.int32, (t_, 1), 0)
    head_mask = (jnp.concatenate([jnp.full((t_, BRANCH), h, jnp.int32) for h in range(N_HEADS)], axis=0)
                 == lax.broadcasted_iota(jnp.int32, (t4, BRANCH), 1) // HEAD_DIM)
    t_row = jnp.concatenate([lax.broadcasted_iota(jnp.int32, (t_, t4), 0)] * N_HEADS, axis=0)
    t_col = lax.broadcasted_iota(jnp.int32, (t4, t4), 1) & (t_ - 1)
    strict = t_row > t_col
    lower = t_row >= t_col
    stack = lambda x: jnp.where(head_mask, jnp.concatenate([x] * N_HEADS, axis=0), 0.0).astype(BF16)
    mm = lambda a, b: jnp.dot(a, b, preferred_element_type=F32)

    lw = [lw_ref[b] for b in rows]
    cum = each(lambda x: jnp.dot(tri, x, precision=HIGHEST, preferred_element_type=F32), lw)
    p_in = each(jnp.exp, cum)
    p_inv = each(lambda c: jnp.exp(-c), cum)
    p_end = each(lambda p: jnp.sum(jnp.where(row1 == t_ - 1, p, 0.0), axis=0, keepdims=True), p_in)
    kkm = [stack(kk_ref[b] * jnp.exp(cum[b] - lw[b])) for b in rows]
    rp = [stack(r_ref[b] * p_in[b]) for b in rows]
    kh = [k_ref[b] * p_inv[b] for b in rows]
    ah = [kka_ref[b] * p_inv[b] for b in rows]
    khe = [stack(kh[b] * p_end[b]) for b in rows]
    ahe = [stack(ah[b] * p_end[b]) for b in rows]
    khm, ahm = each(stack, kh), each(stack, ah)
    vm = [stack(v_ref[b]) for b in rows]
    lr = each(lambda a, b: jnp.concatenate([a, b], axis=0), kkm, rp)
    gk = each(_dot_nt, lr, khm)
    ga = each(_dot_nt, lr, ahm)
    ab_k = each(lambda g: jnp.concatenate([jnp.where(strict, g[:t4], 0.0), jnp.where(lower, g[t4:], 0.0)],
                                          axis=0).astype(BF16), gk)
    b_a = each(lambda g: jnp.where(lower, g[t4:], 0.0).astype(BF16), ga)
    e = each(lambda g: jnp.where(strict, -g[:t4], 0.0), ga)
    pw = e
    span = 2
    while span < t_:
        pw = each(lambda p: mm(p.astype(BF16), p.astype(BF16)), pw)
        e = each(lambda x, p: x + p + mm(x.astype(BF16), p.astype(BF16)), e, pw)
        span *= 2
    ht = [h_sc[b] for b in rows]
    x0 = each(lambda a, h: _dot_nt(a, h.astype(BF16)), lr, ht)
    kv = each(mm, ab_k, vm)
    x1 = each(lambda a, b: a[:t4] + b[:t4], x0, kv)
    u = each(lambda x, m: x + mm(m.astype(BF16), x.astype(BF16)), x1, e)
    ub = each(lambda x: x.astype(BF16), u)
    ybd = each(lambda a, b, m, x: a[t4:] + b[t4:] - mm(m, x), x0, kv, b_a, ub)
    for b in rows:
        y = ybd[b][0:t_]
        for h in range(1, N_HEADS):
            y = y + ybd[b][h * t_:(h + 1) * t_]
        y_ref[b] = y
        h_sc[b] = ht[b] * p_end[b] + _dot_tn(vm[b], khe[b]) - _dot_tn(ub[b], ahe[b])

    @pl.when(ci == pl.num_programs(1) - 1)
    def _():
        for b in range(bb):
            for h, hs in enumerate(head_blocks):
                h_out[b, h] = h_sc[b, hs, hs]


def _rwkv_scan(r, lw, k, v, kk, kka, s0, grp):
    b_, l_ = grp["B"], grp["L"]
    chunk = _pick_tile(l_, RWKV_CHUNK)
    nc = l_ // chunk
    bb = _pick_tile(b_, RWKV_BATCH)
    seq = pl.BlockSpec((bb, chunk, BRANCH), lambda b, c: (b, c, 0))
    st = pl.BlockSpec((bb, N_HEADS, HEAD_DIM, HEAD_DIM), lambda b, c: (b, 0, 0, 0))
    as3 = lambda t: t.reshape(b_, l_, BRANCH)
    y, s1 = pl.pallas_call(
        functools.partial(_rwkv_chunk_kernel, chunk=chunk, bb=bb),
        grid=(b_ // bb, nc),
        in_specs=[seq] * 6 + [st],
        out_specs=[seq, st],
        out_shape=[jax.ShapeDtypeStruct((b_, l_, BRANCH), F32),
                   jax.ShapeDtypeStruct((b_, N_HEADS, HEAD_DIM, HEAD_DIM), F32)],
        scratch_shapes=[pltpu.VMEM((bb, BRANCH, BRANCH), F32)],
        compiler_params=_cparams("parallel", "arbitrary"),
        name="rwkv_scan",
    )(as3(r), as3(lw), as3(k), as3(v), as3(kk), as3(kka), s0)
    return y.reshape(b_ * l_, BRANCH), s1


def _merge_kernel(ml_ref, sc_ref, mb_ref, y_ref, bonus_ref, g_ref, gn_ref, gates_ref, x_ref, gt_ref,
                  mp_ref, sp_ref, ap_ref, rp_ref, wo_ref, o_ref):
    bd = _head_blockdiag(BRANCH)
    y = y_ref[...]
    yc = y - _head_sum(y, bd) * (1.0 / HEAD_DIM)
    yn = yc * lax.rsqrt(_head_sum(yc * yc, bd) * (1.0 / HEAD_DIM) + GN_EPS)
    rw = (yn * gn_ref[...] + bonus_ref[...]) * g_ref[...]
    acts = (ml_ref[...], sc_ref[...], mb_ref[...], rw)
    projs = (mp_ref, sp_ref, ap_ref, rp_ref)
    merged = None
    for j in range(4):
        br = jnp.dot(acts[j].astype(BF16), projs[j][...], preferred_element_type=F32)
        term = _sigmoid(gates_ref[:, j * D_MODEL:(j + 1) * D_MODEL]) * br
        merged = term if merged is None else merged + term
    o_ref[...] = x_ref[...] + gt_ref[...] * jnp.dot(merged.astype(BF16), wo_ref[...], preferred_element_type=F32)


def _merge(ml, sc, mb, y, bonus, g, gn, z, x, gt, projs, w_out, grp):
    n, d = x.shape
    tm = min(256, grp["tm"])
    tpg = grp["tpg"] * (grp["tm"] // tm)
    r = gt.shape[1]
    if r > 1:
        assert r == grp["tm"] and tpg == grp["tm"] // tm
        gt = gt.reshape(r // tm, tm, d)
        gt_spec = pl.BlockSpec((None, tm, d), lambda i: (i, 0, 0))
    else:
        gt_spec = pl.BlockSpec((None, 1, d), lambda i: (i // tpg, 0, 0))
    act = pl.BlockSpec((tm, BRANCH), lambda i: (i, 0))
    proj = pl.BlockSpec((BRANCH, d), lambda i: (0, 0))
    return pl.pallas_call(
        _merge_kernel,
        grid=(n // tm,),
        in_specs=[act] * 6 + [pl.BlockSpec((1, BRANCH), lambda i: (0, 0)),
                              pl.BlockSpec((tm, 4 * d), lambda i: (i, 0)),
                              pl.BlockSpec((tm, d), lambda i: (i, 0)),
                              gt_spec, proj, proj, proj, proj,
                              pl.BlockSpec((d, d), lambda i: (0, 0))],
        out_specs=pl.BlockSpec((tm, d), lambda i: (i, 0)),
        out_shape=jax.ShapeDtypeStruct((n, d), F32),
        compiler_params=_cparams("parallel"),
        name="merge",
    )(ml, sc, mb, y, bonus, g, gn, z, x, gt, *projs, w_out)


def _ffn_act_kernel(ua_ref, ul_ref, w_ref, buf_ref, o_ref, carry, *, tm):
    @pl.when(pl.program_id(1) == 0)
    def _():
        carry[...] = buf_ref[0]

    ua = ua_ref[...]
    u = _conv3(ua, carry[0:1, :], carry[1:2, :], w_ref)
    carry[...] = ua_ref[tm - 2:tm, :]
    o_ref[...] = (u * _sigmoid(u) * ul_ref[...]).astype(BF16)


def _ffn_act(up, w, buf, grp):
    b_, l_ = grp["B"], grp["L"]
    tm = _pick_tile(l_, 256)
    nt = l_ // tm
    half = lambda j: pl.BlockSpec((tm, D_FF), lambda b, i: (b * nt + i, j))
    return pl.pallas_call(
        functools.partial(_ffn_act_kernel, tm=tm),
        grid=(b_, nt),
        in_specs=[half(0), half(1),
                  pl.BlockSpec((CONV_WIDTH, D_FF), lambda b, i: (0, 0)),
                  pl.BlockSpec((1, CONV_WIDTH - 1, D_FF), lambda b, i: (b, 0, 0))],
        out_specs=pl.BlockSpec((tm, D_FF), lambda b, i: (b * nt + i, 0)),
        out_shape=jax.ShapeDtypeStruct((b_ * l_, D_FF), BF16),
        scratch_shapes=[pltpu.VMEM((CONV_WIDTH - 1, D_FF), F32)],
        compiler_params=_cparams("parallel", "arbitrary"),
        name="ffn_act",
    )(up, up, w, buf)


def _ffn_up_kernel(x_ref, xp_ref, g_ref, sc_ref, sh_ref, wa_ref, wl_ref, cw_ref, buf_ref, o_ref, st_ref,
                   h_sc, hp_sc, prev_sc, tail_sc, *, tm, tpg):
    halo = FFN_HALO
    i = pl.program_id(0)

    @pl.when(pl.program_id(1) == 0)
    def _():
        def norm_mod(x):
            y = x * lax.rsqrt(jnp.mean(x * x, axis=-1, keepdims=True) + NORM_EPS) * g_ref[...]
            return (y * (1.0 + sc_ref[...]) + sh_ref[...]).astype(BF16)

        h_sc[...] = norm_mod(x_ref[...])
        hp_sc[...] = norm_mod(xp_ref[...])

    ua = jnp.dot(h_sc[...], wa_ref[...], preferred_element_type=F32)
    ul = jnp.dot(h_sc[...], wl_ref[...], preferred_element_type=F32)
    prev_sc[...] = jnp.dot(hp_sc[...], wa_ref[...], preferred_element_type=F32)
    tail_sc[...] = jnp.dot(h_sc[tm - halo:tm, :], wa_ref[...], preferred_element_type=F32)
    first = i % tpg == 0
    prev2 = jnp.where(first, buf_ref[0, 0:1, :], prev_sc[halo - 2:halo - 1, :])
    prev1 = jnp.where(first, buf_ref[0, 1:2, :], prev_sc[halo - 1:halo, :])
    u = _conv3(ua, prev2, prev1, cw_ref)
    o_ref[...] = (u * _sigmoid(u) * ul).astype(BF16)
    st_ref[0] = tail_sc[halo - 2:halo, :]


def _ffn_up(x, g, sc, sh, w, cw, buf, grp):
    n, d = x.shape
    l_ = grp["L"]
    tm = _pick_tile(l_, 512)
    tpg = l_ // tm
    tn = 1408
    nj = D_FF // tn
    halo = FFN_HALO
    assert sc.shape[1] == 1 and D_FF % tn == 0 and tm % halo == 0
    vec = lambda r: pl.BlockSpec((None, r, d), lambda i, j: (i // tpg, 0, 0))
    act, tails = pl.pallas_call(
        functools.partial(_ffn_up_kernel, tm=tm, tpg=tpg),
        grid=(n // tm, nj),
        in_specs=[pl.BlockSpec((tm, d), lambda i, j: (i, 0)),
                  pl.BlockSpec((halo, d), lambda i, j: (jnp.maximum(i * (tm // halo) - 1, 0), 0)),
                  pl.BlockSpec((1, d), lambda i, j: (0, 0)),
                  vec(1), vec(1),
                  pl.BlockSpec((d, tn), lambda i, j: (0, j)),
                  pl.BlockSpec((d, tn), lambda i, j: (0, nj + j)),
                  pl.BlockSpec((CONV_WIDTH, tn), lambda i, j: (0, j)),
                  pl.BlockSpec((1, CONV_WIDTH - 1, tn), lambda i, j: (i // tpg, 0, j))],
        out_specs=[pl.BlockSpec((tm, tn), lambda i, j: (i, j)),
                   pl.BlockSpec((1, CONV_WIDTH - 1, tn), lambda i, j: (i, 0, j))],
        out_shape=[jax.ShapeDtypeStruct((n, D_FF), BF16),
                   jax.ShapeDtypeStruct((n // tm, CONV_WIDTH - 1, D_FF), F32)],
        scratch_shapes=[pltpu.VMEM((tm, d), BF16), pltpu.VMEM((halo, d), BF16),
                        pltpu.VMEM((halo, tn), F32), pltpu.VMEM((halo, tn), F32)],
        compiler_params=_cparams("parallel", "arbitrary"),
        name="ffn_up",
    )(x, x, g, sc, sh, w, w, cw, buf)
    return act, tails[tpg - 1::tpg]


def _mm_res_kernel(a_ref, w_ref, x_ref, gt_ref, o_ref):
    o_ref[...] = x_ref[...] + gt_ref[...] * jnp.dot(a_ref[...], w_ref[...], preferred_element_type=F32)


def _mm_res(a, w, x, gt, grp):
    n, d = x.shape
    kdim = a.shape[1]
    tm, tpg, r = min(512, grp["tm"]), grp["tpg"] * (grp["tm"] // min(512, grp["tm"])), gt.shape[1]
    if r > 1:
        gt = gt.reshape(r // tm, tm, d)
        gt_spec = pl.BlockSpec((None, tm, d), lambda i: (i, 0, 0))
    else:
        gt_spec = pl.BlockSpec((None, 1, d), lambda i: (i // tpg, 0, 0))
    return pl.pallas_call(
        _mm_res_kernel,
        grid=(n // tm,),
        in_specs=[pl.BlockSpec((tm, kdim), lambda i: (i, 0)),
                  pl.BlockSpec((kdim, d), lambda i: (0, 0)),
                  pl.BlockSpec((tm, d), lambda i: (i, 0)),
                  gt_spec],
        out_specs=pl.BlockSpec((tm, d), lambda i: (i, 0)),
        out_shape=jax.ShapeDtypeStruct((n, d), F32),
        compiler_params=_cparams("parallel"),
        name="mm_res",
    )(a, w, x, gt)


def _prep_layer(l, P):
    d = D_MODEL
    w_in = P["w_in"][l]
    o_mif = 4 * BRANCH
    o_sb = o_mif + 2 * N_HEADS
    o_aq = o_sb + 3 * BRANCH
    o_rw = o_aq + 3 * BRANCH
    o_g = o_rw + RW_COLS
    w_in_p = jnp.concatenate([
        w_in[:, o_g:o_g + 4 * d],
        w_in[:, o_rw:o_g], w_in[:, o_mif:o_sb], jnp.zeros((d, Z_RWKV_W - RW_COLS - 2 * N_HEADS), F32),
        w_in[:, 0:o_mif], w_in[:, o_sb:o_aq], w_in[:, o_aq:o_rw]], axis=1).astype(BF16)
    assert w_in_p.shape[1] == Z_WIDTH
    bias_if = jnp.concatenate([P["m_bi"][l], P["m_bf"][l], jnp.zeros((LANES - 2 * N_HEADS,), F32)])[None, :]
    zw = Z_RWKV_W
    wlr = jnp.zeros((RW_LORA, 3 * BRANCH), F32)
    wlr = wlr.at[0:32, 0:BRANCH].set(P["r_wB"][l])
    wlr = wlr.at[32:64, BRANCH:2 * BRANCH].set(P["r_aB"][l])
    wlr = wlr.at[64:128, 2 * BRANCH:].set(P["r_gB"][l])
    row = lambda t: t[None, :]
    rw = dict(mu=jnp.pad(P["r_mu"][l], (0, zw - RW_COLS))[None, :], wlr=wlr.astype(BF16),
              w0=row(P["r_w0"][l]), a0=row(P["r_a0"][l]), kk=row(P["r_kk"][l]), ka=row(P["r_ka"][l]),
              rk=row(P["r_rk"][l]))
    bf = lambda name: P[name][l].astype(BF16)
    return dict(
        w_ada=bf("w_ada"), b_ada=row(P["b_ada"][l]), w_in=w_in_p, bias_if=bias_if,
        norm_mix_g=row(P["norm_mix_g"][l]), norm_ffn_g=row(P["norm_ffn_g"][l]),
        m_norm_g=row(P["m_norm_g"][l]), s_conv=P["s_conv"][l],
        qg=row(jnp.tile(P["a_qnorm"][l], N_HEADS)), kg=row(jnp.tile(P["a_knorm"][l], N_HEADS)),
        rw=rw, r_norm_g=row(P["r_norm_g"][l]),
        projs=(bf("m_proj"), bf("s_proj"), bf("a_proj"), bf("r_proj")), w_out=bf("w_out"),
        f_up=bf("f_up"), f_conv=P["f_conv"][l], f_down=bf("f_down"))


def _layer(x, mod, W, state, tables, attend, grp):
    mc, mn, mm, sbuf, rs, rshift, fbuf = state
    b_, l_ = grp["B"], grp["L"]
    sh_m, sc_m, gt_m, sh_f, sc_f, gt_f = mod
    z = _norm_mod_matmul(x, W["norm_mix_g"], sc_m, sh_m, W["w_in"], grp, Z_WIDTH // 3)
    ml, mc, mn, mm = _mlstm(z, W["bias_if"], W["m_norm_g"], mc, mn, mm, grp)
    sc, sbuf = _sconv(z, W["s_conv"], sbuf, grp)
    q, k = _qk_prep(z, W["qg"], W["kg"], tables, grp)
    v = z[:, Z_MOBA + 2 * BRANCH:Z_MOBA + 3 * BRANCH]
    mb = attend(q, k, z, v)
    zw = Z_RWKV_W
    prev = jnp.pad(rshift, ((0, 0), (0, zw - RW_COLS)))[:, None, :]
    r, w, k2, vv, kk, kka, bonus, g = _rwkv_prep(z, prev, W["rw"], grp)
    y, rs_t = _rwkv_scan(r, w, k2, vv, kk, kka, rs, grp)
    rshift = z.reshape(b_, l_, Z_WIDTH)[:, -1, Z_RWKV:Z_RWKV + RW_COLS]
    x = _merge(ml, sc, mb, y, bonus, g, W["r_norm_g"], z, x, gt_m, W["projs"], W["w_out"], grp)
    if sc_f.shape[1] == 1 and l_ % FFN_HALO == 0:
        act, fbuf = _ffn_up(x, W["norm_ffn_g"], sc_f, sh_f, W["f_up"], W["f_conv"], fbuf, grp)
    else:
        up = _norm_mod_matmul(x, W["norm_ffn_g"], sc_f, sh_f, W["f_up"], grp, 1408)
        act = _ffn_act(up, W["f_conv"], fbuf, grp)
        fbuf = up.reshape(b_, l_, 2 * D_FF)[:, l_ - (CONV_WIDTH - 1):, :D_FF]
    x = _mm_res(act, W["f_down"], x, gt_f, grp)
    return x, (k, v), (mc, mn, mm, sbuf, rs_t, rshift, fbuf)


def kernel(x_prompt, x_sample, c_prompt, c_sample, cache_k, cache_v, page_table, state_mlstm_c, state_mlstm_n, state_mlstm_m, state_conv, state_rwkv, state_rwkv_shift, state_ffn_conv, norm_mix_g, norm_ffn_g, w_ada, b_ada, w_in, m_bi, m_bf, m_norm_g, m_proj, s_conv, s_proj, a_qnorm, a_knorm, a_proj, r_mu, r_w0, r_wB, r_a0, r_aB, r_gB, r_kk, r_ka, r_rk, r_norm_g, r_proj, w_out, f_up, f_conv, f_down):
    P = dict(norm_mix_g=norm_mix_g, norm_ffn_g=norm_ffn_g, w_ada=w_ada, b_ada=b_ada, w_in=w_in,
             m_bi=m_bi, m_bf=m_bf, m_norm_g=m_norm_g, m_proj=m_proj, s_conv=s_conv, s_proj=s_proj,
             a_qnorm=a_qnorm, a_knorm=a_knorm, a_proj=a_proj, r_mu=r_mu, r_w0=r_w0, r_wB=r_wB,
             r_a0=r_a0, r_aB=r_aB, r_gB=r_gB, r_kk=r_kk, r_ka=r_ka, r_rk=r_rk, r_norm_g=r_norm_g,
             r_proj=r_proj, w_out=w_out, f_up=f_up, f_conv=f_conv, f_down=f_down)
    depth = w_in.shape[0]
    bp, lp, d = x_prompt.shape
    bs, ls, _ = x_sample.shape
    n_s = bs * ls
    past = page_table.shape[1] * PAGE_SIZE
    tm_p = _pick_tile(lp, 1024)
    grp_p = dict(B=bp, L=lp, tm=tm_p, tpg=lp // tm_p)
    grp_s = dict(B=bs, L=ls, tm=n_s, tpg=1)
    assert n_s % 8 == 0 and n_s <= 1024

    zeros = lambda *s: jnp.zeros(s, F32)
    st_p = (zeros(bp, N_HEADS, HEAD_DIM, HEAD_DIM), zeros(bp, N_HEADS, HEAD_DIM), zeros(bp, N_HEADS),
            zeros(bp, CONV_WIDTH - 1, BRANCH), zeros(bp, N_HEADS, HEAD_DIM, HEAD_DIM), zeros(bp, RW_COLS),
            zeros(bp, CONV_WIDTH - 1, D_FF))
    tab_p = _rope_tables(jnp.arange(lp, dtype=jnp.int32))
    tab_s = tuple(jnp.tile(t, (bs, 1)) for t in _rope_tables(past + jnp.arange(ls, dtype=jnp.int32)))

    c_all = jnp.concatenate([c_prompt, c_sample], axis=0)
    hp = x_prompt.reshape(bp * lp, d)
    hs = x_sample.reshape(n_s, d)
    kv_p, kv_s, sts_p, sts_s = [], [], [], []
    for l in range(depth):
        W = _prep_layer(l, P)
        mod = _ada(c_all, W["w_ada"], W["b_ada"])
        mods = [mod[:, j * d:(j + 1) * d] for j in range(6)]
        mod_p = [m[:bp][:, None, :] for m in mods]
        mod_s = [jnp.repeat(m[bp:], ls, axis=0)[None] for m in mods]

        attend_p = lambda q, k, z, v: _moba_prompt(q, k, z, grp_p)
        hp, kv, st = _layer(hp, mod_p, W, st_p, tab_p, attend_p, grp_p)
        kv_p.append(kv)
        sts_p.append(st)

        prev = (state_mlstm_c[l], state_mlstm_n[l], state_mlstm_m[l], state_conv[l], state_rwkv[l],
                state_rwkv_shift[l], state_ffn_conv[l])
        attend_s = lambda q, k, z, v: _moba_sample(q, k, v, cache_k, cache_v, page_table, l, grp_s)
        hs, kv, st = _layer(hs, mod_s, W, prev, tab_s, attend_s, grp_s)
        kv_s.append(kv)
        sts_s.append(st)

    heads = lambda t, b_, l_: t.reshape(b_, l_, N_HEADS, HEAD_DIM)
    k_prompt = jnp.stack([heads(k, bp, lp) for k, _ in kv_p])
    v_prompt = jnp.stack([heads(v, bp, lp) for _, v in kv_p])
    k_sample = jnp.stack([heads(k, bs, ls) for k, _ in kv_s])
    v_sample = jnp.stack([heads(v, bs, ls) for _, v in kv_s])
    stack = lambda sts: [jnp.stack(t) for t in zip(*sts)]
    return (hp.reshape(bp, lp, d), hs.reshape(bs, ls, d), k_prompt, v_prompt, k_sample, v_sample,
            *stack(sts_p), *stack(sts_s))
```

```python
import functools
import math

import jax
import jax.numpy as jnp
from jax import lax
from jax.experimental import pallas as pl
from jax.experimental.pallas import tpu as pltpu

F32 = jnp.float32
BF16 = jnp.bfloat16
HIGHEST = lax.Precision.HIGHEST

D_MODEL = 1024
HEAD_DIM = 64
N_HEADS = 4
BRANCH = N_HEADS * HEAD_DIM
CONV_WIDTH = 3
MB_BLOCK = 256
MB_TOPK = 3
PAGE_SIZE = 128
ROPE_DIMS = HEAD_DIM // 4
ROPE_THETA = 500000.0
RW_LORA = 128
RW_COLS = 3 * BRANCH + RW_LORA
D_FF = 11 * D_MODEL // 4
MLSTM_CHUNK = 128
RWKV_CHUNK = 64
FFN_HALO = 16
SCAN_BATCH = 4
RWKV_BATCH = 8
NORM_EPS = 1e-6
GN_EPS = 64e-5
LANES = 128

Z_GATES = 0
Z_RWKV = 4096
Z_RWKV_W = 1024
Z_MIF = Z_RWKV + RW_COLS
Z_MLSTM = 5120
Z_SCONV = 6144
Z_MOBA = 6912
Z_WIDTH = 7680

VMEM_LIMIT = 48 * 1024 * 1024


def _cparams(*sem):
    return pltpu.CompilerParams(dimension_semantics=sem, vmem_limit_bytes=VMEM_LIMIT)


def _sigmoid(x):
    return 0.5 * jnp.tanh(0.5 * x) + 0.5


def _head_blockdiag(n):
    r = lax.broadcasted_iota(jnp.int32, (n, n), 0) // HEAD_DIM
    c = lax.broadcasted_iota(jnp.int32, (n, n), 1) // HEAD_DIM
    return (r == c).astype(F32)


def _head_sum(x, bd):
    hi = x.astype(BF16)
    lo = (x - hi.astype(F32)).astype(BF16)
    bd = bd.astype(BF16)
    return jnp.dot(hi, bd, preferred_element_type=F32) + jnp.dot(lo, bd, preferred_element_type=F32)


def _dot_nt(a, b):
    return lax.dot_general(a, b, (((1,), (1,)), ((), ())), preferred_element_type=F32)


def _dot_tn(a, b):
    return lax.dot_general(a, b, (((0,), (0,)), ((), ())), preferred_element_type=F32)


def _pick_tile(n, cap):
    t = cap
    while n % t:
        t //= 2
    return t


def _ada_kernel(c_ref, w_ref, b_ref, o_ref):
    c = c_ref[...]
    a = (c * _sigmoid(c)).astype(BF16)
    o_ref[...] = jnp.dot(a, w_ref[...], preferred_element_type=F32) + b_ref[...]


def _ada(c_all, w, b):
    m, d = c_all.shape
    n = w.shape[1]
    tn = 1536
    return pl.pallas_call(
        _ada_kernel,
        grid=(n // tn,),
        in_specs=[pl.BlockSpec((m, d), lambda j: (0, 0)),
                  pl.BlockSpec((d, tn), lambda j: (0, j)),
                  pl.BlockSpec((1, tn), lambda j: (0, j))],
        out_specs=pl.BlockSpec((m, tn), lambda j: (0, j)),
        out_shape=jax.ShapeDtypeStruct((m, n), F32),
        compiler_params=_cparams("parallel"),
        name="ada",
    )(c_all, w, b)


def _nmm_kernel(x_ref, g_ref, sc_ref, sh_ref, w_ref, o_ref, h_sc):
    @pl.when(pl.program_id(1) == 0)
    def _():
        x = x_ref[...]
        y = x * lax.rsqrt(jnp.mean(x * x, axis=-1, keepdims=True) + NORM_EPS) * g_ref[...]
        h_sc[...] = (y * (1.0 + sc_ref[...]) + sh_ref[...]).astype(BF16)

    o_ref[...] = jnp.dot(h_sc[...], w_ref[...], preferred_element_type=F32)


def _norm_mod_matmul(x, g, sc, sh, w, grp, tn):
    n, d = x.shape
    nout = w.shape[1]
    tm, tpg, r = grp["tm"], grp["tpg"], sc.shape[1]
    return pl.pallas_call(
        _nmm_kernel,
        grid=(n // tm, nout // tn),
        in_specs=[pl.BlockSpec((tm, d), lambda i, j: (i, 0)),
                  pl.BlockSpec((1, d), lambda i, j: (0, 0)),
                  pl.BlockSpec((None, r, d), lambda i, j: (i // tpg, 0, 0)),
                  pl.BlockSpec((None, r, d), lambda i, j: (i // tpg, 0, 0)),
                  pl.BlockSpec((d, tn), lambda i, j: (0, j))],
        out_specs=pl.BlockSpec((tm, tn), lambda i, j: (i, j)),
        out_shape=jax.ShapeDtypeStruct((n, nout), F32),
        scratch_shapes=[pltpu.VMEM((tm, d), BF16)],
        compiler_params=_cparams("parallel", "arbitrary"),
        name="norm_mod_matmul",
    )(x, g, sc, sh, w)


def _log_sigmoid(x):
    return jnp.minimum(x, 0.0) - jnp.log(1.0 + jnp.exp(-jnp.abs(x)))


def _mlstm_kernel(zq_ref, zif_ref, bias_ref, g_ref, c0_ref, n0_ref, m0_ref,
                  act_ref, c_out, n_out, m_out, c_sc, n_sc, m_sc, *, chunk, bb):
    t_ = chunk
    ci = pl.program_id(1)

    @pl.when(ci == 0)
    def _():
        c_sc[...] = c0_ref[...]
        n_sc[...] = n0_ref[...]
        m_sc[...] = m0_ref[...]

    row = lax.broadcasted_iota(jnp.int32, (t_, t_), 0)
    col = lax.broadcasted_iota(jnp.int32, (t_, t_), 1)
    causal = col <= row
    eye = row == col
    row1 = lax.broadcasted_iota(jnp.int32, (t_, 1), 0)
    chains = [(b, h) for b in range(bb) for h in range(N_HEADS)]
    idx = range(len(chains))
    each = lambda f, *xs: [f(*a) for a in zip(*xs)]
    mmf = lambda a, b: jnp.dot(a, b, preferred_element_type=F32)
    gates = [zif_ref[b] + bias_ref[...] for b in range(bb)]
    logf = each(_log_sigmoid, gates)
    part = lambda j: [zq_ref[b, :, j * BRANCH + h * HEAD_DIM:j * BRANCH + (h + 1) * HEAD_DIM] for b, h in chains]
    q, k, v, o = part(0), [x * (HEAD_DIM ** -0.5) for x in part(1)], part(2), part(3)
    li_col = [gates[b][:, h:h + 1] for b, h in chains]
    lf_col = [logf[b][:, N_HEADS + h:N_HEADS + h + 1] for b, h in chains]
    li_row = each(lambda x: jnp.sum(jnp.where(eye, x, 0.0), axis=0, keepdims=True), li_col)
    lf_row = each(lambda x: jnp.sum(jnp.where(eye, x, 0.0), axis=0, keepdims=True), lf_col)
    b_col = each(lambda x: jnp.sum(jnp.where(causal, x, 0.0), axis=1, keepdims=True), lf_row)
    b_row = each(lambda x: jnp.sum(jnp.where(row <= col, x, 0.0), axis=0, keepdims=True), lf_col)
    m_prev = [m_sc[b, h][:, 0:1] for b, h in chains]
    d = each(lambda bc, br, lr: jnp.where(causal, bc - br + lr, -jnp.inf), b_col, b_row, li_row)
    inter = each(lambda bc, m: bc + m, b_col, m_prev)
    mt = each(lambda x, y: jnp.maximum(jnp.max(x, axis=1, keepdims=True), y), d, inter)
    qb, kb, vb = (each(lambda x: x.astype(BF16), t) for t in (q, k, v))
    w = each(lambda x, m, a, b: jnp.exp(x - m) * _dot_nt(a, b), d, mt, qb, kb)
    a_int = each(lambda x, m: jnp.exp(x - m), inter, mt)
    c = [c_sc[b, h] for b, h in chains]
    n = [n_sc[b, h] for b, h in chains]
    num = each(lambda w_, v_, a, q_, c_: mmf(w_.astype(BF16), v_) + a * mmf(q_, c_.astype(BF16)), w, vb, a_int, qb, c)
    den = each(lambda w_, a, q_, n_: jnp.sum(w_, axis=1, keepdims=True) + a * jnp.sum(q_ * n_, axis=1, keepdims=True),
               w, a_int, q, n)
    hh = each(lambda x, y, m: x / jnp.maximum(jnp.abs(y), jnp.exp(-m)), num, den, mt)
    hn = each(lambda x: x * lax.rsqrt(jnp.mean(x * x, axis=-1, keepdims=True) + NORM_EPS), hh)
    b_end = each(lambda x: jnp.sum(jnp.where(row1 == t_ - 1, x, 0.0), axis=0, keepdims=True), b_col)
    g_col = each(lambda e, bc, l: e - bc + l, b_end, b_col, li_col)
    m_new = each(lambda e, m, g_: jnp.maximum(e + m, jnp.max(g_, axis=0, keepdims=True)), b_end, m_prev, g_col)
    wk = each(lambda g_, m, k_: jnp.exp(g_ - m) * k_, g_col, m_new, k)
    decay = each(lambda e, m, mn: jnp.exp(e + m - mn), b_end, m_prev, m_new)
    c_new = each(lambda dc, c_, wk_, v_: dc * c_ + _dot_tn(wk_.astype(BF16), v_), decay, c, wk, vb)
    for i in idx:
        b, h = chains[i]
        lo = h * HEAD_DIM
        act_ref[b, :, lo:lo + HEAD_DIM] = hn[i] * g_ref[:, lo:lo + HEAD_DIM] * _sigmoid(o[i])
        c_sc[b, h] = c_new[i]
        n_sc[b, h] = decay[i] * n[i] + jnp.sum(wk[i], axis=0, keepdims=True)
        m_sc[b, h] = jnp.broadcast_to(m_new[i], (1, LANES))

    @pl.when(ci == pl.num_programs(1) - 1)
    def _():
        c_out[...] = c_sc[...]
        n_out[...] = n_sc[...]
        m_out[...] = m_sc[...]


def _mlstm(z, bias_if, g, c0, n0, m0, grp):
    b_, l_ = grp["B"], grp["L"]
    t_ = min(MLSTM_CHUNK, l_)
    nc = l_ // t_
    h_ = N_HEADS
    bb = _pick_tile(b_, SCAN_BATCH)
    n0 = n0.reshape(b_, h_, 1, HEAD_DIM)
    m0 = jnp.broadcast_to(m0.reshape(b_, h_, 1, 1), (b_, h_, 1, LANES))
    z3 = z.reshape(b_, l_, Z_WIDTH)
    state_spec = lambda shp: pl.BlockSpec((bb,) + shp, lambda b, c: (b, 0, 0, 0))
    act, c1, n1, m1 = pl.pallas_call(
        functools.partial(_mlstm_kernel, chunk=t_, bb=bb),
        grid=(b_ // bb, nc),
        in_specs=[pl.BlockSpec((bb, t_, 4 * BRANCH), lambda b, c: (b, c, Z_MLSTM // (4 * BRANCH))),
                  pl.BlockSpec((bb, t_, LANES), lambda b, c: (b, c, Z_MIF // LANES)),
                  pl.BlockSpec((1, LANES), lambda b, c: (0, 0)),
                  pl.BlockSpec((1, BRANCH), lambda b, c: (0, 0)),
                  state_spec((h_, HEAD_DIM, HEAD_DIM)),
                  state_spec((h_, 1, HEAD_DIM)),
                  state_spec((h_, 1, LANES))],
        out_specs=[pl.BlockSpec((bb, t_, BRANCH), lambda b, c: (b, c, 0)),
                   state_spec((h_, HEAD_DIM, HEAD_DIM)),
                   state_spec((h_, 1, HEAD_DIM)),
                   state_spec((h_, 1, LANES))],
        out_shape=[jax.ShapeDtypeStruct((b_, l_, BRANCH), F32),
                   jax.ShapeDtypeStruct((b_, h_, HEAD_DIM, HEAD_DIM), F32),
                   jax.ShapeDtypeStruct((b_, h_, 1, HEAD_DIM), F32),
                   jax.ShapeDtypeStruct((b_, h_, 1, LANES), F32)],
        scratch_shapes=[pltpu.VMEM((bb, h_, HEAD_DIM, HEAD_DIM), F32),
                        pltpu.VMEM((bb, h_, 1, HEAD_DIM), F32),
                        pltpu.VMEM((bb, h_, 1, LANES), F32)],
        compiler_params=_cparams("parallel", "arbitrary"),
        name="mlstm",
    )(z3, z3, bias_if, g, c0, n0, m0)
    return act.reshape(b_ * l_, BRANCH), c1, n1.reshape(b_, h_, HEAD_DIM), m1[:, :, 0, 0]


def _conv3(p, prev2, prev1, w_ref):
    row = lax.broadcasted_iota(jnp.int32, p.shape, 0)
    p1 = jnp.where(row == 0, prev1, pltpu.roll(p, 1, axis=0))
    p2 = jnp.where(row == 0, prev2, jnp.where(row == 1, prev1, pltpu.roll(p, 2, axis=0)))
    return w_ref[0:1, :] * p2 + w_ref[1:2, :] * p1 + w_ref[2:3, :] * p


def _sconv_kernel(sb_ref, sc_ref, sh_ref, w_ref, buf_ref, out_ref, st_ref, carry, *, tm):
    @pl.when(pl.program_id(1) == 0)
    def _():
        carry[...] = buf_ref[0]

    p = sc_ref[...] * sh_ref[...]
    u = _conv3(p, carry[0:1, :], carry[1:2, :], w_ref)
    out_ref[...] = sb_ref[...] * u
    new = sc_ref[tm - 2:tm, :] * sh_ref[tm - 2:tm, :]
    carry[...] = new
    st_ref[0] = new


def _sconv(z, w, buf, grp):
    b_, l_ = grp["B"], grp["L"]
    tm = _pick_tile(l_, 512)
    nt = l_ // tm
    cb = Z_SCONV // BRANCH
    zspec = lambda j: pl.BlockSpec((tm, BRANCH), lambda b, i: (b * nt + i, cb + j))
    return pl.pallas_call(
        functools.partial(_sconv_kernel, tm=tm),
        grid=(b_, nt),
        in_specs=[zspec(0), zspec(1), zspec(2),
                  pl.BlockSpec((CONV_WIDTH, BRANCH), lambda b, i: (0, 0)),
                  pl.BlockSpec((1, CONV_WIDTH - 1, BRANCH), lambda b, i: (b, 0, 0))],
        out_specs=[pl.BlockSpec((tm, BRANCH), lambda b, i: (b * nt + i, 0)),
                   pl.BlockSpec((1, CONV_WIDTH - 1, BRANCH), lambda b, i: (b, 0, 0))],
        out_shape=[jax.ShapeDtypeStruct((b_ * l_, BRANCH), F32),
                   jax.ShapeDtypeStruct((b_, CONV_WIDTH - 1, BRANCH), F32)],
        scratch_shapes=[pltpu.VMEM((CONV_WIDTH - 1, BRANCH), F32)],
        compiler_params=_cparams("parallel", "arbitrary"),
        name="sconv",
    )(z, z, z, w, buf)


def _qk_prep_kernel(q_ref, k_ref, qg_ref, kg_ref, cos_ref, sa_ref, sb_ref, qo_ref, ko_ref):
    bd = _head_blockdiag(BRANCH)
    cos, sa, sb = cos_ref[...], sa_ref[...], sb_ref[...]
    half = ROPE_DIMS // 2

    def prep(x, g):
        y = x * lax.rsqrt(_head_sum(x * x, bd) * (1.0 / HEAD_DIM) + NORM_EPS) * g
        return y * cos + pltpu.roll(y, BRANCH - half, axis=1) * sa + pltpu.roll(y, half, axis=1) * sb

    qo_ref[...] = prep(q_ref[...], qg_ref[...])
    ko_ref[...] = prep(k_ref[...], kg_ref[...])


def _rope_tables(pos):
    half = ROPE_DIMS // 2
    inv = jnp.exp(-math.log(ROPE_THETA) * jnp.arange(0, ROPE_DIMS, 2, dtype=F32) / ROPE_DIMS)
    ang = pos.astype(F32)[:, None] * inv[None, :]
    cos, sin = jnp.cos(ang), jnp.sin(ang)
    n = pos.shape[0]
    rest = HEAD_DIM - ROPE_DIMS
    c_h = jnp.concatenate([cos, cos, jnp.ones((n, rest), F32)], axis=1)
    sa_h = jnp.concatenate([-sin, jnp.zeros((n, half + rest), F32)], axis=1)
    sb_h = jnp.concatenate([jnp.zeros((n, half), F32), sin, jnp.zeros((n, rest), F32)], axis=1)
    tile = lambda t: jnp.tile(t, (1, N_HEADS))
    return tile(c_h), tile(sa_h), tile(sb_h)


def _qk_prep(z, qg, kg, tables, grp):
    n = grp["B"] * grp["L"]
    ltab = tables[0].shape[0]
    tm = _pick_tile(ltab, 512)
    npos = ltab // tm
    cb = Z_MOBA // BRANCH
    tspec = pl.BlockSpec((tm, BRANCH), lambda i: (i % npos, 0))
    gspec = pl.BlockSpec((1, BRANCH), lambda i: (0, 0))
    ospec = pl.BlockSpec((tm, BRANCH), lambda i: (i, 0))
    return pl.pallas_call(
        _qk_prep_kernel,
        grid=(n // tm,),
        in_specs=[pl.BlockSpec((tm, BRANCH), lambda i: (i, cb)),
                  pl.BlockSpec((tm, BRANCH), lambda i: (i, cb + 1)),
                  gspec, gspec, tspec, tspec, tspec],
        out_specs=[ospec, ospec],
        out_shape=[jax.ShapeDtypeStruct((n, BRANCH), F32)] * 2,
        compiler_params=_cparams("parallel"),
        name="qk_prep",
    )(z, z, qg, kg, *tables)


def _moba_prompt_kernel(q_ref, k_ref, v_ref, o_ref, kmean, kx, vx, *, nb):
    qi = pl.program_id(1)
    tq = MB_BLOCK
    half = LANES // 2
    masked = float(jnp.finfo(BF16).min)
    lane = lax.broadcasted_iota(jnp.int32, (tq, LANES), 1)
    low = lane < half

    @pl.when(qi == 0)
    def _():
        kmean[...] = jnp.zeros_like(kmean)
        for n in range(nb):
            rows = slice(n * MB_BLOCK, (n + 1) * MB_BLOCK)
            kmean[n:n + 1, :] = jnp.mean(k_ref[rows, :], axis=0, keepdims=True)
            onehot = (lane == half + n).astype(F32)
            for h in range(N_HEADS):
                pair = slice((h // 2) * LANES, (h // 2 + 1) * LANES)
                k2, v2 = k_ref[rows, pair], v_ref[rows, pair]
                if h % 2:
                    k2, v2 = pltpu.roll(k2, half, axis=1), pltpu.roll(v2, half, axis=1)
                kx[h, rows, :] = jnp.where(low, k2, onehot).astype(BF16)
                vx[h, rows, :] = jnp.where(low, v2, 1.0).astype(BF16)

    row = lax.broadcasted_iota(jnp.int32, (tq, tq), 0)
    col = lax.broadcasted_iota(jnp.int32, (tq, tq), 1)
    scale = HEAD_DIM ** -0.5
    own0 = pl.multiple_of(qi * MB_BLOCK, MB_BLOCK)
    heads = range(N_HEADS)
    sub = 8
    assert nb <= sub
    blk = lax.broadcasted_iota(jnp.int32, (sub, tq), 0)
    place = (lax.broadcasted_iota(jnp.int32, (sub, LANES), 1)
             == lax.broadcasted_iota(jnp.int32, (sub, LANES), 0) + half).astype(BF16)
    qx = []
    for h in heads:
        hs = slice(h * HEAD_DIM, (h + 1) * HEAD_DIM)
        s = lax.dot_general(kmean[0:sub, hs], q_ref[:, hs], (((1,), (1,)), ((), ())),
                            precision=HIGHEST, preferred_element_type=F32)
        valid = blk < qi
        s = jnp.where(valid, s, -jnp.inf)
        rank = jnp.zeros((sub, tq), jnp.int32)
        for m in range(nb):
            sm = s[m:m + 1, :]
            rank += ((sm > s) | ((sm == s) & (m < blk))).astype(jnp.int32)
        keep = ((valid & (rank < MB_TOPK)) | (blk == qi)).astype(BF16)
        keep_q = _dot_tn(keep, place)
        q2 = q_ref[:, (h // 2) * LANES:(h // 2 + 1) * LANES] * scale
        if h % 2:
            q2 = pltpu.roll(q2, half, axis=1)
        qx.append(jnp.where(low, q2, jnp.where(keep_q > 0.5, 0.0, masked)).astype(BF16))
    sc = [jnp.where(col <= row, _dot_nt(qx[h], kx[h, pl.ds(own0, MB_BLOCK), :]), -jnp.inf) for h in heads]
    m_i = [jnp.max(sc[h], axis=1, keepdims=True) for h in heads]
    p = [jnp.exp(sc[h] - m_i[h]).astype(BF16) for h in heads]
    acc = [jnp.dot(p[h], vx[h, pl.ds(own0, MB_BLOCK), :], preferred_element_type=F32) for h in heads]

    def body(n, carry):
        m_i, acc = carry
        k0 = pl.multiple_of(n * MB_BLOCK, MB_BLOCK)
        sc = [_dot_nt(qx[h], kx[h, pl.ds(k0, MB_BLOCK), :]) for h in heads]
        m_new = [jnp.maximum(m_i[h], jnp.max(sc[h], axis=1, keepdims=True)) for h in heads]
        p = [jnp.exp(sc[h] - m_new[h]).astype(BF16) for h in heads]
        pv = [jnp.dot(p[h], vx[h, pl.ds(k0, MB_BLOCK), :], preferred_element_type=F32) for h in heads]
        return m_new, [jnp.exp(m_i[h] - m_new[h]) * acc[h] + pv[h] for h in heads]

    m_i, acc = lax.fori_loop(0, qi, body, (m_i, acc))
    for h in heads:
        o_ref[:, h * HEAD_DIM:(h + 1) * HEAD_DIM] = (acc[h] / pltpu.roll(acc[h], half, axis=1))[:, :half]


def _moba_prompt(q, k, z, grp):
    b_, l_ = grp["B"], grp["L"]
    assert l_ % MB_BLOCK == 0 and l_ // MB_BLOCK <= LANES // 2
    nb = l_ // MB_BLOCK
    return pl.pallas_call(
        functools.partial(_moba_prompt_kernel, nb=nb),
        grid=(b_, nb),
        in_specs=[pl.BlockSpec((MB_BLOCK, BRANCH), lambda b, i: (b * nb + i, 0)),
                  pl.BlockSpec((l_, BRANCH), lambda b, i: (b, 0)),
                  pl.BlockSpec((l_, BRANCH), lambda b, i: (b, Z_MOBA // BRANCH + 2))],
        out_specs=pl.BlockSpec((MB_BLOCK, BRANCH), lambda b, i: (b * nb + i, 0)),
        out_shape=jax.ShapeDtypeStruct((b_ * l_, BRANCH), F32),
        scratch_shapes=[pltpu.VMEM((LANES, BRANCH), F32),
                        pltpu.VMEM((N_HEADS, l_, LANES), BF16),
                        pltpu.VMEM((N_HEADS, l_, LANES), BF16)],
        compiler_params=_cparams("parallel", "arbitrary"),
        name="moba_prompt",
    )(q, k, z)


def _moba_sample_kernel(pt_ref, q_ref, kn_ref, vn_ref, *refs, nblk, dec, gsz):
    ppb = MB_BLOCK // PAGE_SIZE
    k_refs, v_refs = refs[:ppb * gsz], refs[ppb * gsz:2 * ppb * gsz]
    o_ref, kmean_s, m_s, l_s, o_s = refs[2 * ppb * gsz:]
    step = pl.program_id(1)
    nq = N_HEADS * dec
    scale = HEAD_DIM ** -0.5
    lane = lax.broadcasted_iota(jnp.int32, (nq, LANES), 1)
    lane_c = lax.broadcasted_iota(jnp.int32, (BRANCH, LANES), 1)

    @pl.when(step == 0)
    def _():
        kmean_s[...] = jnp.zeros_like(kmean_s)
        m_s[...] = jnp.full_like(m_s, -jnp.inf)
        l_s[...] = jnp.zeros_like(l_s)

    row_head = jnp.concatenate([jnp.full((dec, BRANCH), h, jnp.int32) for h in range(N_HEADS)], axis=0)
    lane_head = lax.broadcasted_iota(jnp.int32, (nq, BRANCH), 1) // HEAD_DIM
    qbd = jnp.where(row_head == lane_head, jnp.concatenate([q_ref[...]] * N_HEADS, axis=0), 0.0)
    qbd_b = qbd.astype(BF16)
    kmean, m_all, l_all = kmean_s[...], m_s[...], l_s[...]
    blocks = range(gsz)
    kt = [jnp.concatenate([k_refs[ppb * g + j][...] for j in range(ppb)], axis=1) for g in blocks]
    s = [jnp.dot(qbd_b, kt[g].astype(BF16), preferred_element_type=F32) * scale for g in blocks]
    mx = [jnp.max(s[g], axis=1, keepdims=True) for g in blocks]
    p = [jnp.exp(s[g] - mx[g]) for g in blocks]
    vt = [jnp.concatenate([v_refs[ppb * g + j][...] for j in range(ppb)], axis=1) for g in blocks]
    o = [_dot_nt(p[g].astype(BF16), vt[g].astype(BF16)) for g in blocks]
    for g in blocks:
        blk = step * gsz + g
        o_s[blk] = o[g]
        kmean = jnp.where(lane_c == blk, jnp.mean(kt[g], axis=1, keepdims=True), kmean)
        m_all = jnp.where(lane == blk, mx[g], m_all)
        l_all = jnp.where(lane == blk, jnp.sum(p[g], axis=1, keepdims=True), l_all)
    kmean_s[...] = kmean
    m_s[...] = m_all
    l_s[...] = l_all

    @pl.when(step == pl.num_programs(1) - 1)
    def _():
        zpad = jnp.zeros((LANES - dec, BRANCH), F32)
        knew = jnp.concatenate([kn_ref[...], zpad], axis=0)
        vnew = jnp.concatenate([vn_ref[...], zpad], axis=0)
        rowq = jnp.concatenate([lax.broadcasted_iota(jnp.int32, (dec, LANES), 0)] * N_HEADS, axis=0)
        lane_f = lane.astype(F32)
        s = jnp.dot(qbd, kmean_s[...], precision=HIGHEST, preferred_element_type=F32)
        s = jnp.where(lane < nblk, s, -jnp.inf)
        sel = jnp.zeros((nq, LANES), jnp.bool_)
        for _ in range(MB_TOPK):
            mxv = jnp.max(s, axis=1, keepdims=True)
            idx = jnp.min(jnp.where(s == mxv, lane_f, 2.0 * LANES), axis=1, keepdims=True)
            pick = lane_f == idx
            sel = sel | pick
            s = jnp.where(pick, -jnp.inf, s)
        sel = sel & (lane < nblk)
        s_own = jnp.where(lane <= rowq, _dot_nt(qbd_b, knew.astype(BF16)) * scale, -jnp.inf)
        m_own = jnp.max(s_own, axis=1, keepdims=True)
        p_own = jnp.exp(s_own - m_own)
        l_own = jnp.sum(p_own, axis=1, keepdims=True)
        o_own = jnp.dot(p_own.astype(BF16), vnew.astype(BF16), preferred_element_type=F32)
        mb = m_s[...]
        mtot = jnp.maximum(jnp.max(jnp.where(sel, mb, -jnp.inf), axis=1, keepdims=True), m_own)
        wgt = jnp.where(sel, jnp.exp(mb - mtot), 0.0)
        a_own = jnp.exp(m_own - mtot)
        ltot = jnp.sum(wgt * l_s[...], axis=1, keepdims=True) + l_own * a_own
        acc = o_own * a_own
        for j in range(nblk):
            acc = acc + wgt[:, j:j + 1] * o_s[j]
        res = acc / ltot
        for h in range(N_HEADS):
            hs = slice(h * HEAD_DIM, (h + 1) * HEAD_DIM)
            o_ref[:, hs] = res[h * dec:(h + 1) * dec, hs]


def _moba_sample(q, k, v, pool_k, pool_v, page_table, layer, grp):
    b_, dec = grp["B"], grp["L"]
    n_pages = page_table.shape[1]
    ppb = MB_BLOCK // PAGE_SIZE
    assert n_pages % ppb == 0
    nblk = n_pages // ppb
    assert MB_TOPK <= nblk <= LANES and dec <= LANES and dec % 8 == 0
    gsz = max(g for g in range(1, 17) if nblk % g == 0)
    as_pages = lambda t: t.transpose(0, 1, 3, 4, 2).reshape(t.shape[0], t.shape[1], BRANCH, PAGE_SIZE)
    pool_k, pool_v = as_pages(pool_k), as_pages(pool_v)
    pt = page_table.reshape(-1)

    def page(j):
        return pl.BlockSpec((None, None, BRANCH, PAGE_SIZE),
                            lambda b, s, pt: (layer, pt[b * n_pages + s * gsz * ppb + j], 0, 0))

    pages = [page(j) for j in range(gsz * ppb)]
    new = pl.BlockSpec((dec, BRANCH), lambda b, s, pt: (b, 0))
    nq = N_HEADS * dec
    return pl.pallas_call(
        functools.partial(_moba_sample_kernel, nblk=nblk, dec=dec, gsz=gsz),
        grid_spec=pltpu.PrefetchScalarGridSpec(
            num_scalar_prefetch=1,
            grid=(b_, nblk // gsz),
            in_specs=[new, new, new] + pages + pages,
            out_specs=new,
            scratch_shapes=[pltpu.VMEM((BRANCH, LANES), F32),
                            pltpu.VMEM((nq, LANES), F32),
                            pltpu.VMEM((nq, LANES), F32),
                            pltpu.VMEM((nblk, nq, BRANCH), F32)]),
        out_shape=jax.ShapeDtypeStruct((b_ * dec, BRANCH), F32),
        compiler_params=_cparams("parallel", "arbitrary"),
        name="moba_sample",
    )(pt, q, k, v, *([pool_k] * (gsz * ppb)), *([pool_v] * (gsz * ppb)))


def _rwkv_prep_kernel(z_ref, prev_ref, mu_ref, wlr_ref, w0_ref, a0_ref, kks_ref, kas_ref, rk_ref,
                      r_o, w_o, k_o, v_o, kk_o, kka_o, bonus_o, g_o, carry, *, tm):
    @pl.when(pl.program_id(1) == 0)
    def _():
        carry[...] = prev_ref[0]

    z = z_ref[...]
    row = lax.broadcasted_iota(jnp.int32, z.shape, 0)
    zs = jnp.where(row == 0, carry[...], pltpu.roll(z, 1, axis=0))
    carry[...] = z_ref[tm - 1:tm, :]
    zz = z + mu_ref[...] * (zs - z)
    r = zz[:, 0:BRANCH]
    k = zz[:, BRANCH:2 * BRANCH]
    v = zz[:, 2 * BRANCH:3 * BRANCH]
    lr = zz[:, 3 * BRANCH:3 * BRANCH + RW_LORA]
    lane = lax.broadcasted_iota(jnp.int32, lr.shape, 1)
    lr_in = jnp.where(lane < 32, jnp.tanh(lr), jnp.where(lane < 64, lr, _sigmoid(lr)))
    lo = jnp.dot(lr_in.astype(BF16), wlr_ref[...], preferred_element_type=F32)
    log_decay = -math.exp(-0.5) * _sigmoid(w0_ref[...] + lo[:, 0:BRANCH])
    a = _sigmoid(a0_ref[...] + lo[:, BRANCH:2 * BRANCH])
    bd = _head_blockdiag(BRANCH)
    kk = k * kks_ref[...]
    kk = kk * lax.rsqrt(jnp.maximum(_head_sum(kk * kk, bd), 1e-24))
    k2 = k * (1.0 + (a - 1.0) * kas_ref[...])
    r_o[...] = r
    w_o[...] = log_decay
    k_o[...] = k2
    v_o[...] = v
    kk_o[...] = kk
    kka_o[...] = kk * a
    bonus_o[...] = _head_sum(r * k2 * rk_ref[...], bd) * v
    g_o[...] = lo[:, 2 * BRANCH:3 * BRANCH]


def _rwkv_prep(z, prev, p, grp):
    b_, l_ = grp["B"], grp["L"]
    tm = _pick_tile(l_, 256)
    nt = l_ // tm
    zw = Z_RWKV_W
    vec = pl.BlockSpec((1, BRANCH), lambda b, i: (0, 0))
    ospec = pl.BlockSpec((tm, BRANCH), lambda b, i: (b * nt + i, 0))
    return pl.pallas_call(
        functools.partial(_rwkv_prep_kernel, tm=tm),
        grid=(b_, nt),
        in_specs=[pl.BlockSpec((tm, zw), lambda b, i: (b * nt + i, Z_RWKV // zw)),
                  pl.BlockSpec((1, 1, zw), lambda b, i: (b, 0, 0)),
                  pl.BlockSpec((1, zw), lambda b, i: (0, 0)),
                  pl.BlockSpec((RW_LORA, 3 * BRANCH), lambda b, i: (0, 0)),
                  vec, vec, vec, vec, vec],
        out_specs=[ospec] * 8,
        out_shape=[jax.ShapeDtypeStruct((b_ * l_, BRANCH), F32)] * 8,
        scratch_shapes=[pltpu.VMEM((1, zw), F32)],
        compiler_params=_cparams("parallel", "arbitrary"),
        name="rwkv_prep",
    )(z, prev, p["mu"], p["wlr"], p["w0"], p["a0"], p["kk"], p["ka"], p["rk"])


def _rwkv_chunk_kernel(r_ref, lw_ref, k_ref, v_ref, kk_ref, kka_ref, h0_ref, y_ref, h_out, h_sc, *, chunk, bb):
    t_ = chunk
    ci = pl.program_id(1)

    head_blocks = [slice(h * HEAD_DIM, (h + 1) * HEAD_DIM) for h in range(N_HEADS)]

    @pl.when(ci == 0)
    def _():
        h_sc[...] = jnp.zeros_like(h_sc)
        for b in range(bb):
            for h, hs in enumerate(head_blocks):
                h_sc[b, hs, hs] = h0_ref[b, h]

    t4 = N_HEADS * t_
    rows = list(range(bb))
    each = lambda f, *xs: [f(*a) for a in zip(*xs)]
    tri = (lax.broadcasted_iota(jnp.int32, (t_, t_), 0) >= lax.broadcasted_iota(jnp.int32, (t_, t_), 1)).astype(F32)
    row1 = lax.broadcasted_iota(jnp.int32, (t_, 1), 0)
    head_mask = (jnp.concatenate([jnp.full((t_, BRANCH), h, jnp.int32) for h in range(N_HEADS)], axis=0)
                 == lax.broadcasted_iota(jnp.int32, (t4, BRANCH), 1) // HEAD_DIM)
    t_row = jnp.concatenate([lax.broadcasted_iota(jnp.int32, (t_, t4), 0)] * N_HEADS, axis=0)
    t_col = lax.broadcasted_iota(jnp.int32, (t4, t4), 1) & (t_ - 1)
    strict = t_row > t_col
    lower = t_row >= t_col
    stack = lambda x: jnp.where(head_mask, jnp.concatenate([x] * N_HEADS, axis=0), 0.0).astype(BF16)
    mm = lambda a, b: jnp.dot(a, b, preferred_element_type=F32)

    lw = [lw_ref[b] for b in rows]
    cum = each(lambda x: jnp.dot(tri, x, precision=HIGHEST, preferred_element_type=F32), lw)
    p_in = each(jnp.exp, cum)
    p_inv = each(lambda c: jnp.exp(-c), cum)
    p_end = each(lambda p: jnp.sum(jnp.where(row1 == t_ - 1, p, 0.0), axis=0, keepdims=True), p_in)
    kkm = [stack(kk_ref[b] * jnp.exp(cum[b] - lw[b])) for b in rows]
    rp = [stack(r_ref[b] * p_in[b]) for b in rows]
    kh = [k_ref[b] * p_inv[b] for b in rows]
    ah = [kka_ref[b] * p_inv[b] for b in rows]
    khe = [stack(kh[b] * p_end[b]) for b in rows]
    ahe = [stack(ah[b] * p_end[b]) for b in rows]
    khm, ahm = each(stack, kh), each(stack, ah)
    vm = [stack(v_ref[b]) for b in rows]
    lr = each(lambda a, b: jnp.concatenate([a, b], axis=0), kkm, rp)
    gk = each(_dot_nt, lr, khm)
    ga = each(_dot_nt, lr, ahm)
    ab_k = each(lambda g: jnp.concatenate([jnp.where(strict, g[:t4], 0.0), jnp.where(lower, g[t4:], 0.0)],
                                          axis=0).astype(BF16), gk)
    b_a = each(lambda g: jnp.where(lower, g[t4:], 0.0).astype(BF16), ga)
    e = each(lambda g: jnp.where(strict, -g[:t4], 0.0), ga)
    pw = e
    span = 2
    while span < t_:
        pw = each(lambda p: mm(p.astype(BF16), p.astype(BF16)), pw)
        e = each(lambda x, p: x + p + mm(x.astype(BF16), p.astype(BF16)), e, pw)
        span *= 2
    ht = [h_sc[b] for b in rows]
    x0 = each(lambda a, h: _dot_nt(a, h.astype(BF16)), lr, ht)
    kv = each(mm, ab_k, vm)
    x1 = each(lambda a, b: a[:t4] + b[:t4], x0, kv)
    u = each(lambda x, m: x + mm(m.astype(BF16), x.astype(BF16)), x1, e)
    ub = each(lambda x: x.astype(BF16), u)
    ybd = each(lambda a, b, m, x: a[t4:] + b[t4:] - mm(m, x), x0, kv, b_a, ub)
    for b in rows:
        y = ybd[b][0:t_]
        for h in range(1, N_HEADS):
            y = y + ybd[b][h * t_:(h + 1) * t_]
        y_ref[b] = y
        h_sc[b] = ht[b] * p_end[b] + _dot_tn(vm[b], khe[b]) - _dot_tn(ub[b], ahe[b])

    @pl.when(ci == pl.num_programs(1) - 1)
    def _():
        for b in range(bb):
            for h, hs in enumerate(head_blocks):
                h_out[b, h] = h_sc[b, hs, hs]


def _rwkv_scan(r, lw, k, v, kk, kka, s0, grp):
    b_, l_ = grp["B"], grp["L"]
    chunk = _pick_tile(l_, RWKV_CHUNK)
    nc = l_ // chunk
    bb = _pick_tile(b_, RWKV_BATCH)
    seq = pl.BlockSpec((bb, chunk, BRANCH), lambda b, c: (b, c, 0))
    st = pl.BlockSpec((bb, N_HEADS, HEAD_DIM, HEAD_DIM), lambda b, c: (b, 0, 0, 0))
    as3 = lambda t: t.reshape(b_, l_, BRANCH)
    y, s1 = pl.pallas_call(
        functools.partial(_rwkv_chunk_kernel, chunk=chunk, bb=bb),
        grid=(b_ // bb, nc),
        in_specs=[seq] * 6 + [st],
        out_specs=[seq, st],
        out_shape=[jax.ShapeDtypeStruct((b_, l_, BRANCH), F32),
                   jax.ShapeDtypeStruct((b_, N_HEADS, HEAD_DIM, HEAD_DIM), F32)],
        scratch_shapes=[pltpu.VMEM((bb, BRANCH, BRANCH), F32)],
        compiler_params=_cparams("parallel", "arbitrary"),
        name="rwkv_scan",
    )(as3(r), as3(lw), as3(k), as3(v), as3(kk), as3(kka), s0)
    return y.reshape(b_ * l_, BRANCH), s1


def _merge_kernel(ml_ref, sc_ref, mb_ref, y_ref, bonus_ref, g_ref, gn_ref, gates_ref, x_ref, gt_ref,
                  mp_ref, sp_ref, ap_ref, rp_ref, wo_ref, o_ref):
    bd = _head_blockdiag(BRANCH)
    y = y_ref[...]
    yc = y - _head_sum(y, bd) * (1.0 / HEAD_DIM)
    yn = yc * lax.rsqrt(_head_sum(yc * yc, bd) * (1.0 / HEAD_DIM) + GN_EPS)
    rw = (yn * gn_ref[...] + bonus_ref[...]) * g_ref[...]
    acts = (ml_ref[...], sc_ref[...], mb_ref[...], rw)
    projs = (mp_ref, sp_ref, ap_ref, rp_ref)
    merged = None
    for j in range(4):
        br = jnp.dot(acts[j].astype(BF16), projs[j][...], preferred_element_type=F32)
        term = _sigmoid(gates_ref[:, j * D_MODEL:(j + 1) * D_MODEL]) * br
        merged = term if merged is None else merged + term
    o_ref[...] = x_ref[...] + gt_ref[...] * jnp.dot(merged.astype(BF16), wo_ref[...], preferred_element_type=F32)


def _merge(ml, sc, mb, y, bonus, g, gn, z, x, gt, projs, w_out, grp):
    n, d = x.shape
    tm = min(256, grp["tm"])
    tpg = grp["tpg"] * (grp["tm"] // tm)
    r = gt.shape[1]
    if r > 1:
        assert r == grp["tm"] and tpg == grp["tm"] // tm
        gt = gt.reshape(r // tm, tm, d)
        gt_spec = pl.BlockSpec((None, tm, d), lambda i: (i, 0, 0))
    else:
        gt_spec = pl.BlockSpec((None, 1, d), lambda i: (i // tpg, 0, 0))
    act = pl.BlockSpec((tm, BRANCH), lambda i: (i, 0))
    proj = pl.BlockSpec((BRANCH, d), lambda i: (0, 0))
    return pl.pallas_call(
        _merge_kernel,
        grid=(n // tm,),
        in_specs=[act] * 6 + [pl.BlockSpec((1, BRANCH), lambda i: (0, 0)),
                              pl.BlockSpec((tm, 4 * d), lambda i: (i, 0)),
                              pl.BlockSpec((tm, d), lambda i: (i, 0)),
                              gt_spec, proj, proj, proj, proj,
                              pl.BlockSpec((d, d), lambda i: (0, 0))],
        out_specs=pl.BlockSpec((tm, d), lambda i: (i, 0)),
        out_shape=jax.ShapeDtypeStruct((n, d), F32),
        compiler_params=_cparams("parallel"),
        name="merge",
    )(ml, sc, mb, y, bonus, g, gn, z, x, gt, *projs, w_out)


def _ffn_act_kernel(ua_ref, ul_ref, w_ref, buf_ref, o_ref, carry, *, tm):
    @pl.when(pl.program_id(1) == 0)
    def _():
        carry[...] = buf_ref[0]

    ua = ua_ref[...]
    u = _conv3(ua, carry[0:1, :], carry[1:2, :], w_ref)
    carry[...] = ua_ref[tm - 2:tm, :]
    o_ref[...] = (u * _sigmoid(u) * ul_ref[...]).astype(BF16)


def _ffn_act(up, w, buf, grp):
    b_, l_ = grp["B"], grp["L"]
    tm = _pick_tile(l_, 256)
    nt = l_ // tm
    half = lambda j: pl.BlockSpec((tm, D_FF), lambda b, i: (b * nt + i, j))
    return pl.pallas_call(
        functools.partial(_ffn_act_kernel, tm=tm),
        grid=(b_, nt),
        in_specs=[half(0), half(1),
                  pl.BlockSpec((CONV_WIDTH, D_FF), lambda b, i: (0, 0)),
                  pl.BlockSpec((1, CONV_WIDTH - 1, D_FF), lambda b, i: (b, 0, 0))],
        out_specs=pl.BlockSpec((tm, D_FF), lambda b, i: (b * nt + i, 0)),
        out_shape=jax.ShapeDtypeStruct((b_ * l_, D_FF), BF16),
        scratch_shapes=[pltpu.VMEM((CONV_WIDTH - 1, D_FF), F32)],
        compiler_params=_cparams("parallel", "arbitrary"),
        name="ffn_act",
    )(up, up, w, buf)


def _ffn_up_kernel(x_ref, xp_ref, g_ref, sc_ref, sh_ref, wa_ref, wl_ref, cw_ref, buf_ref, o_ref, st_ref,
                   h_sc, hp_sc, prev_sc, tail_sc, *, tm, tpg):
    halo = FFN_HALO
    i = pl.program_id(0)

    @pl.when(pl.program_id(1) == 0)
    def _():
        def norm_mod(x):
            y = x * lax.rsqrt(jnp.mean(x * x, axis=-1, keepdims=True) + NORM_EPS) * g_ref[...]
            return (y * (1.0 + sc_ref[...]) + sh_ref[...]).astype(BF16)

        h_sc[...] = norm_mod(x_ref[...])
        hp_sc[...] = norm_mod(xp_ref[...])

    ua = jnp.dot(h_sc[...], wa_ref[...], preferred_element_type=F32)
    ul = jnp.dot(h_sc[...], wl_ref[...], preferred_element_type=F32)
    prev_sc[...] = jnp.dot(hp_sc[...], wa_ref[...], preferred_element_type=F32)
    tail_sc[...] = jnp.dot(h_sc[tm - halo:tm, :], wa_ref[...], preferred_element_type=F32)
    first = i % tpg == 0
    prev2 = jnp.where(first, buf_ref[0, 0:1, :], prev_sc[halo - 2:halo - 1, :])
    prev1 = jnp.where(first, buf_ref[0, 1:2, :], prev_sc[halo - 1:halo, :])
    u = _conv3(ua, prev2, prev1, cw_ref)
    o_ref[...] = (u * _sigmoid(u) * ul).astype(BF16)
    st_ref[0] = tail_sc[halo - 2:halo, :]


def _ffn_up(x, g, sc, sh, w, cw, buf, grp):
    n, d = x.shape
    l_ = grp["L"]
    tm = _pick_tile(l_, 512)
    tpg = l_ // tm
    tn = 1408
    nj = D_FF // tn
    halo = FFN_HALO
    assert sc.shape[1] == 1 and D_FF % tn == 0 and tm % halo == 0
    vec = lambda r: pl.BlockSpec((None, r, d), lambda i, j: (i // tpg, 0, 0))
    act, tails = pl.pallas_call(
        functools.partial(_ffn_up_kernel, tm=tm, tpg=tpg),
        grid=(n // tm, nj),
        in_specs=[pl.BlockSpec((tm, d), lambda i, j: (i, 0)),
                  pl.BlockSpec((halo, d), lambda i, j: (jnp.maximum(i * (tm // halo) - 1, 0), 0)),
                  pl.BlockSpec((1, d), lambda i, j: (0, 0)),
                  vec(1), vec(1),
                  pl.BlockSpec((d, tn), lambda i, j: (0, j)),
                  pl.BlockSpec((d, tn), lambda i, j: (0, nj + j)),
                  pl.BlockSpec((CONV_WIDTH, tn), lambda i, j: (0, j)),
                  pl.BlockSpec((1, CONV_WIDTH - 1, tn), lambda i, j: (i // tpg, 0, j))],
        out_specs=[pl.BlockSpec((tm, tn), lambda i, j: (i, j)),
                   pl.BlockSpec((1, CONV_WIDTH - 1, tn), lambda i, j: (i, 0, j))],
        out_shape=[jax.ShapeDtypeStruct((n, D_FF), BF16),
                   jax.ShapeDtypeStruct((n // tm, CONV_WIDTH - 1, D_FF), F32)],
        scratch_shapes=[pltpu.VMEM((tm, d), BF16), pltpu.VMEM((halo, d), BF16),
                        pltpu.VMEM((halo, tn), F32), pltpu.VMEM((halo, tn), F32)],
        compiler_params=_cparams("parallel", "arbitrary"),
        name="ffn_up",
    )(x, x, g, sc, sh, w, w, cw, buf)
    return act, tails[tpg - 1::tpg]


def _mm_res_kernel(a_ref, w_ref, x_ref, gt_ref, o_ref):
    o_ref[...] = x_ref[...] + gt_ref[...] * jnp.dot(a_ref[...], w_ref[...], preferred_element_type=F32)


def _mm_res(a, w, x, gt, grp):
    n, d = x.shape
    kdim = a.shape[1]
    tm, tpg, r = min(512, grp["tm"]), grp["tpg"] * (grp["tm"] // min(512, grp["tm"])), gt.shape[1]
    if r > 1:
        gt = gt.reshape(r // tm, tm, d)
        gt_spec = pl.BlockSpec((None, tm, d), lambda i: (i, 0, 0))
    else:
        gt_spec = pl.BlockSpec((None, 1, d), lambda i: (i // tpg, 0, 0))
    return pl.pallas_call(
        _mm_res_kernel,
        grid=(n // tm,),
        in_specs=[pl.BlockSpec((tm, kdim), lambda i: (i, 0)),
                  pl.BlockSpec((kdim, d), lambda i: (0, 0)),
                  pl.BlockSpec((tm, d), lambda i: (i, 0)),
                  gt_spec],
        out_specs=pl.BlockSpec((tm, d), lambda i: (i, 0)),
        out_shape=jax.ShapeDtypeStruct((n, d), F32),
        compiler_params=_cparams("parallel"),
        name="mm_res",
    )(a, w, x, gt)


def _prep_layer(l, P):
    d = D_MODEL
    w_in = P["w_in"][l].astype(BF16)
    o_mif = 4 * BRANCH
    o_sb = o_mif + 2 * N_HEADS
    o_aq = o_sb + 3 * BRANCH
    o_rw = o_aq + 3 * BRANCH
    o_g = o_rw + RW_COLS
    w_in_p = jnp.concatenate([
        w_in[:, o_g:o_g + 4 * d],
        w_in[:, o_rw:o_g], w_in[:, o_mif:o_sb], jnp.zeros((d, Z_RWKV_W - RW_COLS - 2 * N_HEADS), BF16),
        w_in[:, 0:o_mif], w_in[:, o_sb:o_aq], w_in[:, o_aq:o_rw]], axis=1)
    assert w_in_p.shape[1] == Z_WIDTH
    bias_if = jnp.concatenate([P["m_bi"][l], P["m_bf"][l], jnp.zeros((LANES - 2 * N_HEADS,), F32)])[None, :]
    zw = Z_RWKV_W
    wlr = jnp.zeros((RW_LORA, 3 * BRANCH), F32)
    wlr = wlr.at[0:32, 0:BRANCH].set(P["r_wB"][l])
    wlr = wlr.at[32:64, BRANCH:2 * BRANCH].set(P["r_aB"][l])
    wlr = wlr.at[64:128, 2 * BRANCH:].set(P["r_gB"][l])
    row = lambda t: t[None, :]
    rw = dict(mu=jnp.pad(P["r_mu"][l], (0, zw - RW_COLS))[None, :], wlr=wlr.astype(BF16),
              w0=row(P["r_w0"][l]), a0=row(P["r_a0"][l]), kk=row(P["r_kk"][l]), ka=row(P["r_ka"][l]),
              rk=row(P["r_rk"][l]))
    bf = lambda name: P[name][l].astype(BF16)
    return dict(
        w_ada=bf("w_ada"), b_ada=row(P["b_ada"][l]), w_in=w_in_p, bias_if=bias_if,
        norm_mix_g=row(P["norm_mix_g"][l]), norm_ffn_g=row(P["norm_ffn_g"][l]),
        m_norm_g=row(P["m_norm_g"][l]), s_conv=P["s_conv"][l],
        qg=row(jnp.tile(P["a_qnorm"][l], N_HEADS)), kg=row(jnp.tile(P["a_knorm"][l], N_HEADS)),
        rw=rw, r_norm_g=row(P["r_norm_g"][l]),
        projs=(bf("m_proj"), bf("s_proj"), bf("a_proj"), bf("r_proj")), w_out=bf("w_out"),
        f_up=bf("f_up"), f_conv=P["f_conv"][l], f_down=bf("f_down"))


def _layer(x, mod, W, state, tables, attend, grp):
    mc, mn, mm, sbuf, rs, rshift, fbuf = state
    b_, l_ = grp["B"], grp["L"]
    sh_m, sc_m, gt_m, sh_f, sc_f, gt_f = mod
    z = _norm_mod_matmul(x, W["norm_mix_g"], sc_m, sh_m, W["w_in"], grp, Z_WIDTH // 3)
    ml, mc, mn, mm = _mlstm(z, W["bias_if"], W["m_norm_g"], mc, mn, mm, grp)
    sc, sbuf = _sconv(z, W["s_conv"], sbuf, grp)
    q, k = _qk_prep(z, W["qg"], W["kg"], tables, grp)
    v = z[:, Z_MOBA + 2 * BRANCH:Z_MOBA + 3 * BRANCH]
    mb = attend(q, k, z, v)
    zw = Z_RWKV_W
    prev = jnp.pad(rshift, ((0, 0), (0, zw - RW_COLS)))[:, None, :]
    r, w, k2, vv, kk, kka, bonus, g = _rwkv_prep(z, prev, W["rw"], grp)
    y, rs_t = _rwkv_scan(r, w, k2, vv, kk, kka, rs, grp)
    rshift = z.reshape(b_, l_, Z_WIDTH)[:, -1, Z_RWKV:Z_RWKV + RW_COLS]
    x = _merge(ml, sc, mb, y, bonus, g, W["r_norm_g"], z, x, gt_m, W["projs"], W["w_out"], grp)
    if sc_f.shape[1] == 1 and l_ % FFN_HALO == 0:
        act, fbuf = _ffn_up(x, W["norm_ffn_g"], sc_f, sh_f, W["f_up"], W["f_conv"], fbuf, grp)
    else:
        up = _norm_mod_matmul(x, W["norm_ffn_g"], sc_f, sh_f, W["f_up"], grp, 1408)
        act = _ffn_act(up, W["f_conv"], fbuf, grp)
        fbuf = up.reshape(b_, l_, 2 * D_FF)[:, l_ - (CONV_WIDTH - 1):, :D_FF]
    x = _mm_res(act, W["f_down"], x, gt_f, grp)
    return x, (k, v), (mc, mn, mm, sbuf, rs_t, rshift, fbuf)


def kernel(x_prompt, x_sample, c_prompt, c_sample, cache_k, cache_v, page_table, state_mlstm_c, state_mlstm_n, state_mlstm_m, state_conv, state_rwkv, state_rwkv_shift, state_ffn_conv, norm_mix_g, norm_ffn_g, w_ada, b_ada, w_in, m_bi, m_bf, m_norm_g, m_proj, s_conv, s_proj, a_qnorm, a_knorm, a_proj, r_mu, r_w0, r_wB, r_a0, r_aB, r_gB, r_kk, r_ka, r_rk, r_norm_g, r_proj, w_out, f_up, f_conv, f_down):
    P = dict(norm_mix_g=norm_mix_g, norm_ffn_g=norm_ffn_g, w_ada=w_ada, b_ada=b_ada, w_in=w_in,
             m_bi=m_bi, m_bf=m_bf, m_norm_g=m_norm_g, m_proj=m_proj, s_conv=s_conv, s_proj=s_proj,
             a_qnorm=a_qnorm, a_knorm=a_knorm, a_proj=a_proj, r_mu=r_mu, r_w0=r_w0, r_wB=r_wB,
             r_a0=r_a0, r_aB=r_aB, r_gB=r_gB, r_kk=r_kk, r_ka=r_ka, r_rk=r_rk, r_norm_g=r_norm_g,
             r_proj=r_proj, w_out=w_out, f_up=f_up, f_conv=f_conv, f_down=f_down)
    depth = w_in.shape[0]
    bp, lp, d = x_prompt.shape
    bs, ls, _ = x_sample.shape
    n_s = bs * ls
    past = page_table.shape[1] * PAGE_SIZE
    tm_p = _pick_tile(lp, 1024)
    grp_p = dict(B=bp, L=lp, tm=tm_p, tpg=lp // tm_p)
    grp_s = dict(B=bs, L=ls, tm=n_s, tpg=1)
    assert n_s % 8 == 0 and n_s <= 1024

    zeros = lambda *s: jnp.zeros(s, F32)
    st_p = (zeros(bp, N_HEADS, HEAD_DIM, HEAD_DIM), zeros(bp, N_HEADS, HEAD_DIM), zeros(bp, N_HEADS),
            zeros(bp, CONV_WIDTH - 1, BRANCH), zeros(bp, N_HEADS, HEAD_DIM, HEAD_DIM), zeros(bp, RW_COLS),
            zeros(bp, CONV_WIDTH - 1, D_FF))
    tab_p = _rope_tables(jnp.arange(lp, dtype=jnp.int32))
    tab_s = tuple(jnp.tile(t, (bs, 1)) for t in _rope_tables(past + jnp.arange(ls, dtype=jnp.int32)))

    c_all = jnp.concatenate([c_prompt, c_sample], axis=0)
    hp = x_prompt.reshape(bp * lp, d)
    hs = x_sample.reshape(n_s, d)
    kv_p, kv_s, sts_p, sts_s = [], [], [], []
    for l in range(depth):
        W = _prep_layer(l, P)
        mod = _ada(c_all, W["w_ada"], W["b_ada"])
        mods = [mod[:, j * d:(j + 1) * d] for j in range(6)]
        mod_p = [m[:bp][:, None, :] for m in mods]
        mod_s = [jnp.repeat(m[bp:], ls, axis=0)[None] for m in mods]

        attend_p = lambda q, k, z, v: _moba_prompt(q, k, z, grp_p)
        hp, kv, st = _layer(hp, mod_p, W, st_p, tab_p, attend_p, grp_p)
        kv_p.append(kv)
        sts_p.append(st)

        prev = (state_mlstm_c[l], state_mlstm_n[l], state_mlstm_m[l], state_conv[l], state_rwkv[l],
                state_rwkv_shift[l], state_ffn_conv[l])
        attend_s = lambda q, k, z, v: _moba_sample(q, k, v, cache_k, cache_v, page_table, l, grp_s)
        hs, kv, st = _layer(hs, mod_s, W, prev, tab_s, attend_s, grp_s)
        kv_s.append(kv)
        sts_s.append(st)

    heads = lambda t, b_, l_: t.reshape(b_, l_, N_HEADS, HEAD_DIM)
    k_prompt = jnp.stack([heads(k, bp, lp) for k, _ in kv_p])
    v_prompt = jnp.stack([heads(v, bp, lp) for _, v in kv_p])
    k_sample = jnp.stack([heads(k, bs, ls) for k, _ in kv_s])
    v_sample = jnp.stack([heads(v, bs, ls) for _, v in kv_s])
    stack = lambda sts: [jnp.stack(t) for t in zip(*sts)]
    return (hp.reshape(bp, lp, d), hs.reshape(bs, ls, d), k_prompt, v_prompt, k_sample, v_sample,
            *stack(sts_p), *stack(sts_s))
```

```python
import functools
import math

import jax
import jax.numpy as jnp
from jax import lax
from jax.experimental import pallas as pl
from jax.experimental.pallas import tpu as pltpu

F32 = jnp.float32
BF16 = jnp.bfloat16
HIGHEST = lax.Precision.HIGHEST

D_MODEL = 1024
HEAD_DIM = 64
N_HEADS = 4
BRANCH = N_HEADS * HEAD_DIM
CONV_WIDTH = 3
MB_BLOCK = 256
MB_TOPK = 3
PAGE_SIZE = 128
ROPE_DIMS = HEAD_DIM // 4
ROPE_THETA = 500000.0
RW_LORA = 128
RW_COLS = 3 * BRANCH + RW_LORA
D_FF = 11 * D_MODEL // 4
MLSTM_CHUNK = 128
RWKV_CHUNK = 64
FFN_HALO = 16
SCAN_BATCH = 4
RWKV_BATCH = 8
NORM_EPS = 1e-6
GN_EPS = 64e-5
LANES = 128

Z_GATES = 0
Z_RWKV = 4096
Z_RWKV_W = 1024
Z_MIF = Z_RWKV + RW_COLS
Z_MLSTM = 5120
Z_SCONV = 6144
Z_MOBA = 6912
Z_WIDTH = 7680

VMEM_LIMIT = 48 * 1024 * 1024


def _cparams(*sem):
    return pltpu.CompilerParams(dimension_semantics=sem, vmem_limit_bytes=VMEM_LIMIT)


def _sigmoid(x):
    return 0.5 * jnp.tanh(0.5 * x) + 0.5


def _head_blockdiag(n):
    r = lax.broadcasted_iota(jnp.int32, (n, n), 0) // HEAD_DIM
    c = lax.broadcasted_iota(jnp.int32, (n, n), 1) // HEAD_DIM
    return (r == c).astype(F32)


def _head_sum(x, bd):
    hi = x.astype(BF16)
    lo = (x - hi.astype(F32)).astype(BF16)
    bd = bd.astype(BF16)
    return jnp.dot(hi, bd, preferred_element_type=F32) + jnp.dot(lo, bd, preferred_element_type=F32)


def _dot_nt(a, b):
    return lax.dot_general(a, b, (((1,), (1,)), ((), ())), preferred_element_type=F32)


def _dot_tn(a, b):
    return lax.dot_general(a, b, (((0,), (0,)), ((), ())), preferred_element_type=F32)


def _pick_tile(n, cap):
    t = cap
    while n % t:
        t //= 2
    return t


def _ada_kernel(c_ref, w_ref, b_ref, o_ref):
    c = c_ref[...]
    a = (c * _sigmoid(c)).astype(BF16)
    o_ref[...] = jnp.dot(a, w_ref[...], preferred_element_type=F32) + b_ref[...]


def _ada(c_all, w, b):
    m, d = c_all.shape
    n = w.shape[1]
    tn = 1536
    return pl.pallas_call(
        _ada_kernel,
        grid=(n // tn,),
        in_specs=[pl.BlockSpec((m, d), lambda j: (0, 0)),
                  pl.BlockSpec((d, tn), lambda j: (0, j)),
                  pl.BlockSpec((1, tn), lambda j: (0, j))],
        out_specs=pl.BlockSpec((m, tn), lambda j: (0, j)),
        out_shape=jax.ShapeDtypeStruct((m, n), F32),
        compiler_params=_cparams("parallel"),
        name="ada",
    )(c_all, w, b)


def _nmm_kernel(x_ref, g_ref, sc_ref, sh_ref, w_ref, o_ref, h_sc):
    @pl.when(pl.program_id(1) == 0)
    def _():
        x = x_ref[...]
        y = x * lax.rsqrt(jnp.mean(x * x, axis=-1, keepdims=True) + NORM_EPS) * g_ref[...]
        h_sc[...] = (y * (1.0 + sc_ref[...]) + sh_ref[...]).astype(BF16)

    o_ref[...] = jnp.dot(h_sc[...], w_ref[...], preferred_element_type=F32)


def _norm_mod_matmul(x, g, sc, sh, w, grp, tn):
    n, d = x.shape
    nout = w.shape[1]
    tm, tpg, r = grp["tm"], grp["tpg"], sc.shape[1]
    return pl.pallas_call(
        _nmm_kernel,
        grid=(n // tm, nout // tn),
        in_specs=[pl.BlockSpec((tm, d), lambda i, j: (i, 0)),
                  pl.BlockSpec((1, d), lambda i, j: (0, 0)),
                  pl.BlockSpec((None, r, d), lambda i, j: (i // tpg, 0, 0)),
                  pl.BlockSpec((None, r, d), lambda i, j: (i // tpg, 0, 0)),
                  pl.BlockSpec((d, tn), lambda i, j: (0, j))],
        out_specs=pl.BlockSpec((tm, tn), lambda i, j: (i, j)),
        out_shape=jax.ShapeDtypeStruct((n, nout), F32),
        scratch_shapes=[pltpu.VMEM((tm, d), BF16)],
        compiler_params=_cparams("parallel", "arbitrary"),
        name="norm_mod_matmul",
    )(x, g, sc, sh, w)


def _log_sigmoid(x):
    return jnp.minimum(x, 0.0) - jnp.log(1.0 + jnp.exp(-jnp.abs(x)))


def _mlstm_kernel(zq_ref, zif_ref, bias_ref, g_ref, c0_ref, n0_ref, m0_ref,
                  act_ref, c_out, n_out, m_out, c_sc, n_sc, m_sc, *, chunk, bb):
    t_ = chunk
    ci = pl.program_id(1)

    @pl.when(ci == 0)
    def _():
        c_sc[...] = c0_ref[...]
        n_sc[...] = n0_ref[...]
        m_sc[...] = m0_ref[...]

    row = lax.broadcasted_iota(jnp.int32, (t_, t_), 0)
    col = lax.broadcasted_iota(jnp.int32, (t_, t_), 1)
    causal = col <= row
    eye = row == col
    row1 = lax.broadcasted_iota(jnp.int32, (t_, 1), 0)
    chains = [(b, h) for b in range(bb) for h in range(N_HEADS)]
    idx = range(len(chains))
    each = lambda f, *xs: [f(*a) for a in zip(*xs)]
    mmf = lambda a, b: jnp.dot(a, b, preferred_element_type=F32)
    gates = [zif_ref[b] + bias_ref[...] for b in range(bb)]
    logf = each(_log_sigmoid, gates)
    part = lambda j: [zq_ref[b, :, j * BRANCH + h * HEAD_DIM:j * BRANCH + (h + 1) * HEAD_DIM] for b, h in chains]
    q, k, v, o = part(0), [x * (HEAD_DIM ** -0.5) for x in part(1)], part(2), part(3)
    li_col = [gates[b][:, h:h + 1] for b, h in chains]
    lf_col = [logf[b][:, N_HEADS + h:N_HEADS + h + 1] for b, h in chains]
    li_row = each(lambda x: jnp.sum(jnp.where(eye, x, 0.0), axis=0, keepdims=True), li_col)
    lf_row = each(lambda x: jnp.sum(jnp.where(eye, x, 0.0), axis=0, keepdims=True), lf_col)
    b_col = each(lambda x: jnp.sum(jnp.where(causal, x, 0.0), axis=1, keepdims=True), lf_row)
    b_row = each(lambda x: jnp.sum(jnp.where(row <= col, x, 0.0), axis=0, keepdims=True), lf_col)
    m_prev = [m_sc[b, h][:, 0:1] for b, h in chains]
    d = each(lambda bc, br, lr: jnp.where(causal, bc - br + lr, -jnp.inf), b_col, b_row, li_row)
    inter = each(lambda bc, m: bc + m, b_col, m_prev)
    mt = each(lambda x, y: jnp.maximum(jnp.max(x, axis=1, keepdims=True), y), d, inter)
    qb, kb, vb = (each(lambda x: x.astype(BF16), t) for t in (q, k, v))
    w = each(lambda x, m, a, b: jnp.exp(x - m) * _dot_nt(a, b), d, mt, qb, kb)
    a_int = each(lambda x, m: jnp.exp(x - m), inter, mt)
    c = [c_sc[b, h] for b, h in chains]
    n = [n_sc[b, h] for b, h in chains]
    num = each(lambda w_, v_, a, q_, c_: mmf(w_.astype(BF16), v_) + a * mmf(q_, c_.astype(BF16)), w, vb, a_int, qb, c)
    den = each(lambda w_, a, q_, n_: jnp.sum(w_, axis=1, keepdims=True) + a * jnp.sum(q_ * n_, axis=1, keepdims=True),
               w, a_int, q, n)
    hh = each(lambda x, y, m: x / jnp.maximum(jnp.abs(y), jnp.exp(-m)), num, den, mt)
    hn = each(lambda x: x * lax.rsqrt(jnp.mean(x * x, axis=-1, keepdims=True) + NORM_EPS), hh)
    b_end = each(lambda x: jnp.sum(jnp.where(row1 == t_ - 1, x, 0.0), axis=0, keepdims=True), b_col)
    g_col = each(lambda e, bc, l: e - bc + l, b_end, b_col, li_col)
    m_new = each(lambda e, m, g_: jnp.maximum(e + m, jnp.max(g_, axis=0, keepdims=True)), b_end, m_prev, g_col)
    wk = each(lambda g_, m, k_: jnp.exp(g_ - m) * k_, g_col, m_new, k)
    decay = each(lambda e, m, mn: jnp.exp(e + m - mn), b_end, m_prev, m_new)
    c_new = each(lambda dc, c_, wk_, v_: dc * c_ + _dot_tn(wk_.astype(BF16), v_), decay, c, wk, vb)
    for i in idx:
        b, h = chains[i]
        lo = h * HEAD_DIM
        act_ref[b, :, lo:lo + HEAD_DIM] = hn[i] * g_ref[:, lo:lo + HEAD_DIM] * _sigmoid(o[i])
        c_sc[b, h] = c_new[i]
        n_sc[b, h] = decay[i] * n[i] + jnp.sum(wk[i], axis=0, keepdims=True)
        m_sc[b, h] = jnp.broadcast_to(m_new[i], (1, LANES))

    @pl.when(ci == pl.num_programs(1) - 1)
    def _():
        c_out[...] = c_sc[...]
        n_out[...] = n_sc[...]
        m_out[...] = m_sc[...]


def _mlstm(z, bias_if, g, c0, n0, m0, grp):
    b_, l_ = grp["B"], grp["L"]
    t_ = min(MLSTM_CHUNK, l_)
    nc = l_ // t_
    h_ = N_HEADS
    bb = _pick_tile(b_, SCAN_BATCH)
    n0 = n0.reshape(b_, h_, 1, HEAD_DIM)
    m0 = jnp.broadcast_to(m0.reshape(b_, h_, 1, 1), (b_, h_, 1, LANES))
    z3 = z.reshape(b_, l_, Z_WIDTH)
    state_spec = lambda shp: pl.BlockSpec((bb,) + shp, lambda b, c: (b, 0, 0, 0))
    act, c1, n1, m1 = pl.pallas_call(
        functools.partial(_mlstm_kernel, chunk=t_, bb=bb),
        grid=(b_ // bb, nc),
        in_specs=[pl.BlockSpec((bb, t_, 4 * BRANCH), lambda b, c: (b, c, Z_MLSTM // (4 * BRANCH))),
                  pl.BlockSpec((bb, t_, LANES), lambda b, c: (b, c, Z_MIF // LANES)),
                  pl.BlockSpec((1, LANES), lambda b, c: (0, 0)),
                  pl.BlockSpec((1, BRANCH), lambda b, c: (0, 0)),
                  state_spec((h_, HEAD_DIM, HEAD_DIM)),
                  state_spec((h_, 1, HEAD_DIM)),
                  state_spec((h_, 1, LANES))],
        out_specs=[pl.BlockSpec((bb, t_, BRANCH), lambda b, c: (b, c, 0)),
                   state_spec((h_, HEAD_DIM, HEAD_DIM)),
                   state_spec((h_, 1, HEAD_DIM)),
                   state_spec((h_, 1, LANES))],
        out_shape=[jax.ShapeDtypeStruct((b_, l_, BRANCH), F32),
                   jax.ShapeDtypeStruct((b_, h_, HEAD_DIM, HEAD_DIM), F32),
                   jax.ShapeDtypeStruct((b_, h_, 1, HEAD_DIM), F32),
                   jax.ShapeDtypeStruct((b_, h_, 1, LANES), F32)],
        scratch_shapes=[pltpu.VMEM((bb, h_, HEAD_DIM, HEAD_DIM), F32),
                        pltpu.VMEM((bb, h_, 1, HEAD_DIM), F32),
                        pltpu.VMEM((bb, h_, 1, LANES), F32)],
        compiler_params=_cparams("parallel", "arbitrary"),
        name="mlstm",
    )(z3, z3, bias_if, g, c0, n0, m0)
    return act.reshape(b_ * l_, BRANCH), c1, n1.reshape(b_, h_, HEAD_DIM), m1[:, :, 0, 0]


def _conv3(p, prev2, prev1, w_ref):
    row = lax.broadcasted_iota(jnp.int32, p.shape, 0)
    p1 = jnp.where(row == 0, prev1, pltpu.roll(p, 1, axis=0))
    p2 = jnp.where(row == 0, prev2, jnp.where(row == 1, prev1, pltpu.roll(p, 2, axis=0)))
    return w_ref[0:1, :] * p2 + w_ref[1:2, :] * p1 + w_ref[2:3, :] * p


def _sconv_kernel(sb_ref, sc_ref, sh_ref, w_ref, buf_ref, out_ref, st_ref, carry, *, tm):
    @pl.when(pl.program_id(1) == 0)
    def _():
        carry[...] = buf_ref[0]

    p = sc_ref[...] * sh_ref[...]
    u = _conv3(p, carry[0:1, :], carry[1:2, :], w_ref)
    out_ref[...] = sb_ref[...] * u
    new = sc_ref[tm - 2:tm, :] * sh_ref[tm - 2:tm, :]
    carry[...] = new
    st_ref[0] = new


def _sconv(z, w, buf, grp):
    b_, l_ = grp["B"], grp["L"]
    tm = _pick_tile(l_, 512)
    nt = l_ // tm
    cb = Z_SCONV // BRANCH
    zspec = lambda j: pl.BlockSpec((tm, BRANCH), lambda b, i: (b * nt + i, cb + j))
    return pl.pallas_call(
        functools.partial(_sconv_kernel, tm=tm),
        grid=(b_, nt),
        in_specs=[zspec(0), zspec(1), zspec(2),
                  pl.BlockSpec((CONV_WIDTH, BRANCH), lambda b, i: (0, 0)),
                  pl.BlockSpec((1, CONV_WIDTH - 1, BRANCH), lambda b, i: (b, 0, 0))],
        out_specs=[pl.BlockSpec((tm, BRANCH), lambda b, i: (b * nt + i, 0)),
                   pl.BlockSpec((1, CONV_WIDTH - 1, BRANCH), lambda b, i: (b, 0, 0))],
        out_shape=[jax.ShapeDtypeStruct((b_ * l_, BRANCH), F32),
                   jax.ShapeDtypeStruct((b_, CONV_WIDTH - 1, BRANCH), F32)],
        scratch_shapes=[pltpu.VMEM((CONV_WIDTH - 1, BRANCH), F32)],
        compiler_params=_cparams("parallel", "arbitrary"),
        name="sconv",
    )(z, z, z, w, buf)


def _qk_prep_kernel(q_ref, k_ref, qg_ref, kg_ref, cos_ref, sa_ref, sb_ref, qo_ref, ko_ref):
    bd = _head_blockdiag(BRANCH)
    cos, sa, sb = cos_ref[...], sa_ref[...], sb_ref[...]
    half = ROPE_DIMS // 2

    def prep(x, g):
        y = x * lax.rsqrt(_head_sum(x * x, bd) * (1.0 / HEAD_DIM) + NORM_EPS) * g
        return y * cos + pltpu.roll(y, BRANCH - half, axis=1) * sa + pltpu.roll(y, half, axis=1) * sb

    qo_ref[...] = prep(q_ref[...], qg_ref[...])
    ko_ref[...] = prep(k_ref[...], kg_ref[...])


def _rope_tables(pos):
    half = ROPE_DIMS // 2
    inv = jnp.exp(-math.log(ROPE_THETA) * jnp.arange(0, ROPE_DIMS, 2, dtype=F32) / ROPE_DIMS)
    ang = pos.astype(F32)[:, None] * inv[None, :]
    cos, sin = jnp.cos(ang), jnp.sin(ang)
    n = pos.shape[0]
    rest = HEAD_DIM - ROPE_DIMS
    c_h = jnp.concatenate([cos, cos, jnp.ones((n, rest), F32)], axis=1)
    sa_h = jnp.concatenate([-sin, jnp.zeros((n, half + rest), F32)], axis=1)
    sb_h = jnp.concatenate([jnp.zeros((n, half), F32), sin, jnp.zeros((n, rest), F32)], axis=1)
    tile = lambda t: jnp.tile(t, (1, N_HEADS))
    return tile(c_h), tile(sa_h), tile(sb_h)


def _qk_prep(z, qg, kg, tables, grp):
    n = grp["B"] * grp["L"]
    ltab = tables[0].shape[0]
    tm = _pick_tile(ltab, 512)
    npos = ltab // tm
    cb = Z_MOBA // BRANCH
    tspec = pl.BlockSpec((tm, BRANCH), lambda i: (i % npos, 0))
    gspec = pl.BlockSpec((1, BRANCH), lambda i: (0, 0))
    ospec = pl.BlockSpec((tm, BRANCH), lambda i: (i, 0))
    return pl.pallas_call(
        _qk_prep_kernel,
        grid=(n // tm,),
        in_specs=[pl.BlockSpec((tm, BRANCH), lambda i: (i, cb)),
                  pl.BlockSpec((tm, BRANCH), lambda i: (i, cb + 1)),
                  gspec, gspec, tspec, tspec, tspec],
        out_specs=[ospec, ospec],
        out_shape=[jax.ShapeDtypeStruct((n, BRANCH), F32)] * 2,
        compiler_params=_cparams("parallel"),
        name="qk_prep",
    )(z, z, qg, kg, *tables)


def _moba_prompt_kernel(q_ref, k_ref, v_ref, o_ref, kmean, kx, vx, *, nb):
    qi = pl.program_id(1)
    tq = MB_BLOCK
    half = LANES // 2
    masked = float(jnp.finfo(BF16).min)
    lane = lax.broadcasted_iota(jnp.int32, (tq, LANES), 1)
    low = lane < half

    @pl.when(qi == 0)
    def _():
        kmean[...] = jnp.zeros_like(kmean)
        for n in range(nb):
            rows = slice(n * MB_BLOCK, (n + 1) * MB_BLOCK)
            kmean[n:n + 1, :] = jnp.mean(k_ref[rows, :], axis=0, keepdims=True)
            onehot = (lane == half + n).astype(F32)
            for h in range(N_HEADS):
                pair = slice((h // 2) * LANES, (h // 2 + 1) * LANES)
                k2, v2 = k_ref[rows, pair], v_ref[rows, pair]
                if h % 2:
                    k2, v2 = pltpu.roll(k2, half, axis=1), pltpu.roll(v2, half, axis=1)
                kx[h, rows, :] = jnp.where(low, k2, onehot).astype(BF16)
                vx[h, rows, :] = jnp.where(low, v2, 1.0).astype(BF16)

    row = lax.broadcasted_iota(jnp.int32, (tq, tq), 0)
    col = lax.broadcasted_iota(jnp.int32, (tq, tq), 1)
    scale = HEAD_DIM ** -0.5
    own0 = pl.multiple_of(qi * MB_BLOCK, MB_BLOCK)
    heads = range(N_HEADS)
    sub = 8
    assert nb <= sub
    blk = lax.broadcasted_iota(jnp.int32, (sub, tq), 0)
    place = (lax.broadcasted_iota(jnp.int32, (sub, LANES), 1)
             == lax.broadcasted_iota(jnp.int32, (sub, LANES), 0) + half).astype(BF16)
    qx = []
    for h in heads:
        hs = slice(h * HEAD_DIM, (h + 1) * HEAD_DIM)
        s = lax.dot_general(kmean[0:sub, hs], q_ref[:, hs], (((1,), (1,)), ((), ())),
                            precision=HIGHEST, preferred_element_type=F32)
        valid = blk < qi
        s = jnp.where(valid, s, -jnp.inf)
        rank = jnp.zeros((sub, tq), jnp.int32)
        for m in range(nb):
            sm = s[m:m + 1, :]
            rank += ((sm > s) | ((sm == s) & (m < blk))).astype(jnp.int32)
        keep = ((valid & (rank < MB_TOPK)) | (blk == qi)).astype(BF16)
        keep_q = _dot_tn(keep, place)
        q2 = q_ref[:, (h // 2) * LANES:(h // 2 + 1) * LANES] * scale
        if h % 2:
            q2 = pltpu.roll(q2, half, axis=1)
        qx.append(jnp.where(low, q2, jnp.where(keep_q > 0.5, 0.0, masked)).astype(BF16))
    sc = [jnp.where(col <= row, _dot_nt(qx[h], kx[h, pl.ds(own0, MB_BLOCK), :]), -jnp.inf) for h in heads]
    m_i = [jnp.max(sc[h], axis=1, keepdims=True) for h in heads]
    p = [jnp.exp(sc[h] - m_i[h]).astype(BF16) for h in heads]
    acc = [jnp.dot(p[h], vx[h, pl.ds(own0, MB_BLOCK), :], preferred_element_type=F32) for h in heads]

    def body(n, carry):
        m_i, acc = carry
        k0 = pl.multiple_of(n * MB_BLOCK, MB_BLOCK)
        sc = [_dot_nt(qx[h], kx[h, pl.ds(k0, MB_BLOCK), :]) for h in heads]
        m_new = [jnp.maximum(m_i[h], jnp.max(sc[h], axis=1, keepdims=True)) for h in heads]
        p = [jnp.exp(sc[h] - m_new[h]).astype(BF16) for h in heads]
        pv = [jnp.dot(p[h], vx[h, pl.ds(k0, MB_BLOCK), :], preferred_element_type=F32) for h in heads]
        return m_new, [jnp.exp(m_i[h] - m_new[h]) * acc[h] + pv[h] for h in heads]

    m_i, acc = lax.fori_loop(0, qi, body, (m_i, acc))
    for h in heads:
        o_ref[:, h * HEAD_DIM:(h + 1) * HEAD_DIM] = (acc[h] / pltpu.roll(acc[h], half, axis=1))[:, :half]


def _moba_prompt(q, k, z, grp):
    b_, l_ = grp["B"], grp["L"]
    assert l_ % MB_BLOCK == 0 and l_ // MB_BLOCK <= LANES // 2
    nb = l_ // MB_BLOCK
    return pl.pallas_call(
        functools.partial(_moba_prompt_kernel, nb=nb),
        grid=(b_, nb),
        in_specs=[pl.BlockSpec((MB_BLOCK, BRANCH), lambda b, i: (b * nb + i, 0)),
                  pl.BlockSpec((l_, BRANCH), lambda b, i: (b, 0)),
                  pl.BlockSpec((l_, BRANCH), lambda b, i: (b, Z_MOBA // BRANCH + 2))],
        out_specs=pl.BlockSpec((MB_BLOCK, BRANCH), lambda b, i: (b * nb + i, 0)),
        out_shape=jax.ShapeDtypeStruct((b_ * l_, BRANCH), F32),
        scratch_shapes=[pltpu.VMEM((LANES, BRANCH), F32),
                        pltpu.VMEM((N_HEADS, l_, LANES), BF16),
                        pltpu.VMEM((N_HEADS, l_, LANES), BF16)],
        compiler_params=_cparams("parallel", "arbitrary"),
        name="moba_prompt",
    )(q, k, z)


def _moba_sample_kernel(pt_ref, q_ref, kn_ref, vn_ref, *refs, nblk, dec, gsz):
    ppb = MB_BLOCK // PAGE_SIZE
    k_refs, v_refs = refs[:ppb * gsz], refs[ppb * gsz:2 * ppb * gsz]
    o_ref, kmean_s, m_s, l_s, o_s = refs[2 * ppb * gsz:]
    step = pl.program_id(1)
    nq = N_HEADS * dec
    scale = HEAD_DIM ** -0.5
    lane = lax.broadcasted_iota(jnp.int32, (nq, LANES), 1)
    lane_c = lax.broadcasted_iota(jnp.int32, (BRANCH, LANES), 1)

    @pl.when(step == 0)
    def _():
        kmean_s[...] = jnp.zeros_like(kmean_s)
        m_s[...] = jnp.full_like(m_s, -jnp.inf)
        l_s[...] = jnp.zeros_like(l_s)

    row_head = jnp.concatenate([jnp.full((dec, BRANCH), h, jnp.int32) for h in range(N_HEADS)], axis=0)
    lane_head = lax.broadcasted_iota(jnp.int32, (nq, BRANCH), 1) // HEAD_DIM
    qbd = jnp.where(row_head == lane_head, jnp.concatenate([q_ref[...]] * N_HEADS, axis=0), 0.0)
    qbd_b = qbd.astype(BF16)
    kmean, m_all, l_all = kmean_s[...], m_s[...], l_s[...]
    blocks = range(gsz)
    kt = [jnp.concatenate([k_refs[ppb * g + j][...] for j in range(ppb)], axis=1) for g in blocks]
    s = [jnp.dot(qbd_b, kt[g].astype(BF16), preferred_element_type=F32) * scale for g in blocks]
    mx = [jnp.max(s[g], axis=1, keepdims=True) for g in blocks]
    p = [jnp.exp(s[g] - mx[g]) for g in blocks]
    vt = [jnp.concatenate([v_refs[ppb * g + j][...] for j in range(ppb)], axis=1) for g in blocks]
    o = [_dot_nt(p[g].astype(BF16), vt[g].astype(BF16)) for g in blocks]
    for g in blocks:
        blk = step * gsz + g
        o_s[blk] = o[g]
        kmean = jnp.where(lane_c == blk, jnp.mean(kt[g], axis=1, keepdims=True), kmean)
        m_all = jnp.where(lane == blk, mx[g], m_all)
        l_all = jnp.where(lane == blk, jnp.sum(p[g], axis=1, keepdims=True), l_all)
    kmean_s[...] = kmean
    m_s[...] = m_all
    l_s[...] = l_all

    @pl.when(step == pl.num_programs(1) - 1)
    def _():
        zpad = jnp.zeros((LANES - dec, BRANCH), F32)
        knew = jnp.concatenate([kn_ref[...], zpad], axis=0)
        vnew = jnp.concatenate([vn_ref[...], zpad], axis=0)
        rowq = jnp.concatenate([lax.broadcasted_iota(jnp.int32, (dec, LANES), 0)] * N_HEADS, axis=0)
        lane_f = lane.astype(F32)
        s = jnp.dot(qbd, kmean_s[...], precision=HIGHEST, preferred_element_type=F32)
        s = jnp.where(lane < nblk, s, -jnp.inf)
        sel = jnp.zeros((nq, LANES), jnp.bool_)
        for _ in range(MB_TOPK):
            mxv = jnp.max(s, axis=1, keepdims=True)
            idx = jnp.min(jnp.where(s == mxv, lane_f, 2.0 * LANES), axis=1, keepdims=True)
            pick = lane_f == idx
            sel = sel | pick
            s = jnp.where(pick, -jnp.inf, s)
        sel = sel & (lane < nblk)
        s_own = jnp.where(lane <= rowq, _dot_nt(qbd_b, knew.astype(BF16)) * scale, -jnp.inf)
        m_own = jnp.max(s_own, axis=1, keepdims=True)
        p_own = jnp.exp(s_own - m_own)
        l_own = jnp.sum(p_own, axis=1, keepdims=True)
        o_own = jnp.dot(p_own.astype(BF16), vnew.astype(BF16), preferred_element_type=F32)
        mb = m_s[...]
        mtot = jnp.maximum(jnp.max(jnp.where(sel, mb, -jnp.inf), axis=1, keepdims=True), m_own)
        wgt = jnp.where(sel, jnp.exp(mb - mtot), 0.0)
        a_own = jnp.exp(m_own - mtot)
        ltot = jnp.sum(wgt * l_s[...], axis=1, keepdims=True) + l_own * a_own
        acc = o_own * a_own
        for j in range(nblk):
            acc = acc + wgt[:, j:j + 1] * o_s[j]
        res = acc / ltot
        for h in range(N_HEADS):
            hs = slice(h * HEAD_DIM, (h + 1) * HEAD_DIM)
            o_ref[:, hs] = res[h * dec:(h + 1) * dec, hs]


def _moba_sample(q, k, v, pool_k, pool_v, page_table, layer, grp):
    b_, dec = grp["B"], grp["L"]
    n_pages = page_table.shape[1]
    ppb = MB_BLOCK // PAGE_SIZE
    assert n_pages % ppb == 0
    nblk = n_pages // ppb
    assert MB_TOPK <= nblk <= LANES and dec <= LANES and dec % 8 == 0
    gsz = max(g for g in range(1, 33) if nblk % g == 0)
    as_pages = lambda t: t.transpose(0, 1, 3, 4, 2).reshape(t.shape[0], t.shape[1], BRANCH, PAGE_SIZE)
    pool_k, pool_v = as_pages(pool_k), as_pages(pool_v)
    pt = page_table.reshape(-1)

    def page(j):
        return pl.BlockSpec((None, None, BRANCH, PAGE_SIZE),
                            lambda b, s, pt: (layer, pt[b * n_pages + s * gsz * ppb + j], 0, 0))

    pages = [page(j) for j in range(gsz * ppb)]
    new = pl.BlockSpec((dec, BRANCH), lambda b, s, pt: (b, 0))
    nq = N_HEADS * dec
    return pl.pallas_call(
        functools.partial(_moba_sample_kernel, nblk=nblk, dec=dec, gsz=gsz),
        grid_spec=pltpu.PrefetchScalarGridSpec(
            num_scalar_prefetch=1,
            grid=(b_, nblk // gsz),
            in_specs=[new, new, new] + pages + pages,
            out_specs=new,
            scratch_shapes=[pltpu.VMEM((BRANCH, LANES), F32),
                            pltpu.VMEM((nq, LANES), F32),
                            pltpu.VMEM((nq, LANES), F32),
                            pltpu.VMEM((nblk, nq, BRANCH), F32)]),
        out_shape=jax.ShapeDtypeStruct((b_ * dec, BRANCH), F32),
        compiler_params=_cparams("parallel", "arbitrary"),
        name="moba_sample",
    )(pt, q, k, v, *([pool_k] * (gsz * ppb)), *([pool_v] * (gsz * ppb)))


def _rwkv_prep_kernel(z_ref, prev_ref, mu_ref, wlr_ref, w0_ref, a0_ref, kks_ref, kas_ref, rk_ref,
                      r_o, w_o, k_o, v_o, kk_o, kka_o, bonus_o, g_o, carry, *, tm):
    @pl.when(pl.program_id(1) == 0)
    def _():
        carry[...] = prev_ref[0]

    z = z_ref[...]
    row = lax.broadcasted_iota(jnp.int32, z.shape, 0)
    zs = jnp.where(row == 0, carry[...], pltpu.roll(z, 1, axis=0))
    carry[...] = z_ref[tm - 1:tm, :]
    zz = z + mu_ref[...] * (zs - z)
    r = zz[:, 0:BRANCH]
    k = zz[:, BRANCH:2 * BRANCH]
    v = zz[:, 2 * BRANCH:3 * BRANCH]
    lr = zz[:, 3 * BRANCH:3 * BRANCH + RW_LORA]
    lane = lax.broadcasted_iota(jnp.int32, lr.shape, 1)
    lr_in = jnp.where(lane < 32, jnp.tanh(lr), jnp.where(lane < 64, lr, _sigmoid(lr)))
    lo = jnp.dot(lr_in.astype(BF16), wlr_ref[...], preferred_element_type=F32)
    log_decay = -math.exp(-0.5) * _sigmoid(w0_ref[...] + lo[:, 0:BRANCH])
    a = _sigmoid(a0_ref[...] + lo[:, BRANCH:2 * BRANCH])
    bd = _head_blockdiag(BRANCH)
    kk = k * kks_ref[...]
    kk = kk * lax.rsqrt(jnp.maximum(_head_sum(kk * kk, bd), 1e-24))
    k2 = k * (1.0 + (a - 1.0) * kas_ref[...])
    r_o[...] = r
    w_o[...] = log_decay
    k_o[...] = k2
    v_o[...] = v
    kk_o[...] = kk
    kka_o[...] = kk * a
    bonus_o[...] = _head_sum(r * k2 * rk_ref[...], bd) * v
    g_o[...] = lo[:, 2 * BRANCH:3 * BRANCH]


def _rwkv_prep(z, prev, p, grp):
    b_, l_ = grp["B"], grp["L"]
    tm = _pick_tile(l_, 256)
    nt = l_ // tm
    zw = Z_RWKV_W
    vec = pl.BlockSpec((1, BRANCH), lambda b, i: (0, 0))
    ospec = pl.BlockSpec((tm, BRANCH), lambda b, i: (b * nt + i, 0))
    return pl.pallas_call(
        functools.partial(_rwkv_prep_kernel, tm=tm),
        grid=(b_, nt),
        in_specs=[pl.BlockSpec((tm, zw), lambda b, i: (b * nt + i, Z_RWKV // zw)),
                  pl.BlockSpec((1, 1, zw), lambda b, i: (b, 0, 0)),
                  pl.BlockSpec((1, zw), lambda b, i: (0, 0)),
                  pl.BlockSpec((RW_LORA, 3 * BRANCH), lambda b, i: (0, 0)),
                  vec, vec, vec, vec, vec],
        out_specs=[ospec] * 8,
        out_shape=[jax.ShapeDtypeStruct((b_ * l_, BRANCH), F32)] * 8,
        scratch_shapes=[pltpu.VMEM((1, zw), F32)],
        compiler_params=_cparams("parallel", "arbitrary"),
        name="rwkv_prep",
    )(z, prev, p["mu"], p["wlr"], p["w0"], p["a0"], p["kk"], p["ka"], p["rk"])


def _rwkv_chunk_kernel(r_ref, lw_ref, k_ref, v_ref, kk_ref, kka_ref, h0_ref, y_ref, h_out, h_sc, *, chunk, bb):
    t_ = chunk
    ci = pl.program_id(1)

    head_blocks = [slice(h * HEAD_DIM, (h + 1) * HEAD_DIM) for h in range(N_HEADS)]

    @pl.when(ci == 0)
    def _():
        h_sc[...] = jnp.zeros_like(h_sc)
        for b in range(bb):
            for h, hs in enumerate(head_blocks):
                h_sc[b, hs, hs] = h0_ref[b, h]

    t4 = N_HEADS * t_
    rows = list(range(bb))
    each = lambda f, *xs: [f(*a) for a in zip(*xs)]
    tri = (lax.broadcasted_iota(jnp.int32, (t_, t_), 0) >= lax.broadcasted_iota(jnp.int32, (t_, t_), 1)).astype(F32)
    row1 = lax.broadcasted_iota(jnp.int32, (t_, 1), 0)
    head_mask = (jnp.concatenate([jnp.full((t_, BRANCH), h, jnp.int32) for h in range(N_HEADS)], axis=0)
                 == lax.broadcasted_iota(jnp.int32, (t4, BRANCH), 1) // HEAD_DIM)
    t_row = jnp.concatenate([lax.broadcasted_iota(jnp.int32, (t_, t4), 0)] * N_HEADS, axis=0)
    t_col = lax.broadcasted_iota(jnp.int32, (t4, t4), 1) & (t_ - 1)
    strict = t_row > t_col
    lower = t_row >= t_col
    stack = lambda x: jnp.where(head_mask, jnp.concatenate([x] * N_HEADS, axis=0), 0.0).astype(BF16)
    mm = lambda a, b: jnp.dot(a, b, preferred_element_type=F32)

    lw = [lw_ref[b] for b in rows]
    cum = each(lambda x: jnp.dot(tri, x, precision=HIGHEST, preferred_element_type=F32), lw)
    p_in = each(jnp.exp, cum)
    p_inv = each(lambda c: jnp.exp(-c), cum)
    p_end = each(lambda p: jnp.sum(jnp.where(row1 == t_ - 1, p, 0.0), axis=0, keepdims=True), p_in)
    kkm = [stack(kk_ref[b] * jnp.exp(cum[b] - lw[b])) for b in rows]
    rp = [stack(r_ref[b] * p_in[b]) for b in rows]
    kh = [k_ref[b] * p_inv[b] for b in rows]
    ah = [kka_ref[b] * p_inv[b] for b in rows]
    khe = [stack(kh[b] * p_end[b]) for b in rows]
    ahe = [stack(ah[b] * p_end[b]) for b in rows]
    khm, ahm = each(stack, kh), each(stack, ah)
    vm = [stack(v_ref[b]) for b in rows]
    lr = each(lambda a, b: jnp.concatenate([a, b], axis=0), kkm, rp)
    gk = each(_dot_nt, lr, khm)
    ga = each(_dot_nt, lr, ahm)
    ab_k = each(lambda g: jnp.concatenate([jnp.where(strict, g[:t4], 0.0), jnp.where(lower, g[t4:], 0.0)],
                                          axis=0).astype(BF16), gk)
    b_a = each(lambda g: jnp.where(lower, g[t4:], 0.0).astype(BF16), ga)
    e = each(lambda g: jnp.where(strict, -g[:t4], 0.0), ga)
    pw = e
    span = 2
    while span < t_:
        pw = each(lambda p: mm(p.astype(BF16), p.astype(BF16)), pw)
        e = each(lambda x, p: x + p + mm(x.astype(BF16), p.astype(BF16)), e, pw)
        span *= 2
    ht = [h_sc[b] for b in rows]
    x0 = each(lambda a, h: _dot_nt(a, h.astype(BF16)), lr, ht)
    kv = each(mm, ab_k, vm)
    x1 = each(lambda a, b: a[:t4] + b[:t4], x0, kv)
    u = each(lambda x, m: x + mm(m.astype(BF16), x.astype(BF16)), x1, e)
    ub = each(lambda x: x.astype(BF16), u)
    ybd = each(lambda a, b, m, x: a[t4:] + b[t4:] - mm(m, x), x0, kv, b_a, ub)
    for b in rows:
        y = ybd[b][0:t_]
        for h in range(1, N_HEADS):
            y = y + ybd[b][h * t_:(h + 1) * t_]
        y_ref[b] = y
        h_sc[b] = ht[b] * p_end[b] + _dot_tn(vm[b], khe[b]) - _dot_tn(ub[b], ahe[b])

    @pl.when(ci == pl.num_programs(1) - 1)
    def _():
        for b in range(bb):
            for h, hs in enumerate(head_blocks):
                h_out[b, h] = h_sc[b, hs, hs]


def _rwkv_scan(r, lw, k, v, kk, kka, s0, grp):
    b_, l_ = grp["B"], grp["L"]
    chunk = _pick_tile(l_, RWKV_CHUNK)
    nc = l_ // chunk
    bb = _pick_tile(b_, RWKV_BATCH)
    seq = pl.BlockSpec((bb, chunk, BRANCH), lambda b, c: (b, c, 0))
    st = pl.BlockSpec((bb, N_HEADS, HEAD_DIM, HEAD_DIM), lambda b, c: (b, 0, 0, 0))
    as3 = lambda t: t.reshape(b_, l_, BRANCH)
    y, s1 = pl.pallas_call(
        functools.partial(_rwkv_chunk_kernel, chunk=chunk, bb=bb),
        grid=(b_ // bb, nc),
        in_specs=[seq] * 6 + [st],
        out_specs=[seq, st],
        out_shape=[jax.ShapeDtypeStruct((b_, l_, BRANCH), F32),
                   jax.ShapeDtypeStruct((b_, N_HEADS, HEAD_DIM, HEAD_DIM), F32)],
        scratch_shapes=[pltpu.VMEM((bb, BRANCH, BRANCH), F32)],
        compiler_params=_cparams("parallel", "arbitrary"),
        name="rwkv_scan",
    )(as3(r), as3(lw), as3(k), as3(v), as3(kk), as3(kka), s0)
    return y.reshape(b_ * l_, BRANCH), s1


def _merge_kernel(ml_ref, sc_ref, mb_ref, y_ref, bonus_ref, g_ref, gn_ref, gates_ref, x_ref, gt_ref,
                  mp_ref, sp_ref, ap_ref, rp_ref, wo_ref, o_ref):
    bd = _head_blockdiag(BRANCH)
    y = y_ref[...]
    yc = y - _head_sum(y, bd) * (1.0 / HEAD_DIM)
    yn = yc * lax.rsqrt(_head_sum(yc * yc, bd) * (1.0 / HEAD_DIM) + GN_EPS)
    rw = (yn * gn_ref[...] + bonus_ref[...]) * g_ref[...]
    acts = (ml_ref[...], sc_ref[...], mb_ref[...], rw)
    projs = (mp_ref, sp_ref, ap_ref, rp_ref)
    merged = None
    for j in range(4):
        br = jnp.dot(acts[j].astype(BF16), projs[j][...], preferred_element_type=F32)
        term = _sigmoid(gates_ref[:, j * D_MODEL:(j + 1) * D_MODEL]) * br
        merged = term if merged is None else merged + term
    o_ref[...] = x_ref[...] + gt_ref[...] * jnp.dot(merged.astype(BF16), wo_ref[...], preferred_element_type=F32)


def _merge(ml, sc, mb, y, bonus, g, gn, z, x, gt, projs, w_out, grp):
    n, d = x.shape
    tm = min(256, grp["tm"])
    tpg = grp["tpg"] * (grp["tm"] // tm)
    r = gt.shape[1]
    if r > 1:
        assert r == grp["tm"] and tpg == grp["tm"] // tm
        gt = gt.reshape(r // tm, tm, d)
        gt_spec = pl.BlockSpec((None, tm, d), lambda i: (i, 0, 0))
    else:
        gt_spec = pl.BlockSpec((None, 1, d), lambda i: (i // tpg, 0, 0))
    act = pl.BlockSpec((tm, BRANCH), lambda i: (i, 0))
    proj = pl.BlockSpec((BRANCH, d), lambda i: (0, 0))
    return pl.pallas_call(
        _merge_kernel,
        grid=(n // tm,),
        in_specs=[act] * 6 + [pl.BlockSpec((1, BRANCH), lambda i: (0, 0)),
                              pl.BlockSpec((tm, 4 * d), lambda i: (i, 0)),
                              pl.BlockSpec((tm, d), lambda i: (i, 0)),
                              gt_spec, proj, proj, proj, proj,
                              pl.BlockSpec((d, d), lambda i: (0, 0))],
        out_specs=pl.BlockSpec((tm, d), lambda i: (i, 0)),
        out_shape=jax.ShapeDtypeStruct((n, d), F32),
        compiler_params=_cparams("parallel"),
        name="merge",
    )(ml, sc, mb, y, bonus, g, gn, z, x, gt, *projs, w_out)


def _ffn_act_kernel(ua_ref, ul_ref, w_ref, buf_ref, o_ref, carry, *, tm):
    @pl.when(pl.program_id(1) == 0)
    def _():
        carry[...] = buf_ref[0]

    ua = ua_ref[...]
    u = _conv3(ua, carry[0:1, :], carry[1:2, :], w_ref)
    carry[...] = ua_ref[tm - 2:tm, :]
    o_ref[...] = (u * _sigmoid(u) * ul_ref[...]).astype(BF16)


def _ffn_act(up, w, buf, grp):
    b_, l_ = grp["B"], grp["L"]
    tm = _pick_tile(l_, 256)
    nt = l_ // tm
    half = lambda j: pl.BlockSpec((tm, D_FF), lambda b, i: (b * nt + i, j))
    return pl.pallas_call(
        functools.partial(_ffn_act_kernel, tm=tm),
        grid=(b_, nt),
        in_specs=[half(0), half(1),
                  pl.BlockSpec((CONV_WIDTH, D_FF), lambda b, i: (0, 0)),
                  pl.BlockSpec((1, CONV_WIDTH - 1, D_FF), lambda b, i: (b, 0, 0))],
        out_specs=pl.BlockSpec((tm, D_FF), lambda b, i: (b * nt + i, 0)),
        out_shape=jax.ShapeDtypeStruct((b_ * l_, D_FF), BF16),
        scratch_shapes=[pltpu.VMEM((CONV_WIDTH - 1, D_FF), F32)],
        compiler_params=_cparams("parallel", "arbitrary"),
        name="ffn_act",
    )(up, up, w, buf)


def _ffn_up_kernel(x_ref, xp_ref, g_ref, sc_ref, sh_ref, wa_ref, wl_ref, cw_ref, buf_ref, o_ref, st_ref,
                   h_sc, hp_sc, prev_sc, tail_sc, *, tm, tpg):
    halo = FFN_HALO
    i = pl.program_id(0)

    @pl.when(pl.program_id(1) == 0)
    def _():
        def norm_mod(x):
            y = x * lax.rsqrt(jnp.mean(x * x, axis=-1, keepdims=True) + NORM_EPS) * g_ref[...]
            return (y * (1.0 + sc_ref[...]) + sh_ref[...]).astype(BF16)

        h_sc[...] = norm_mod(x_ref[...])
        hp_sc[...] = norm_mod(xp_ref[...])

    ua = jnp.dot(h_sc[...], wa_ref[...], preferred_element_type=F32)
    ul = jnp.dot(h_sc[...], wl_ref[...], preferred_element_type=F32)
    prev_sc[...] = jnp.dot(hp_sc[...], wa_ref[...], preferred_element_type=F32)
    tail_sc[...] = jnp.dot(h_sc[tm - halo:tm, :], wa_ref[...], preferred_element_type=F32)
    first = i % tpg == 0
    prev2 = jnp.where(first, buf_ref[0, 0:1, :], prev_sc[halo - 2:halo - 1, :])
    prev1 = jnp.where(first, buf_ref[0, 1:2, :], prev_sc[halo - 1:halo, :])
    u = _conv3(ua, prev2, prev1, cw_ref)
    o_ref[...] = (u * _sigmoid(u) * ul).astype(BF16)
    st_ref[0] = tail_sc[halo - 2:halo, :]


def _ffn_up(x, g, sc, sh, w, cw, buf, grp):
    n, d = x.shape
    l_ = grp["L"]
    tm = _pick_tile(l_, 512)
    tpg = l_ // tm
    tn = 1408
    nj = D_FF // tn
    halo = FFN_HALO
    assert sc.shape[1] == 1 and D_FF % tn == 0 and tm % halo == 0
    vec = lambda r: pl.BlockSpec((None, r, d), lambda i, j: (i // tpg, 0, 0))
    act, tails = pl.pallas_call(
        functools.partial(_ffn_up_kernel, tm=tm, tpg=tpg),
        grid=(n // tm, nj),
        in_specs=[pl.BlockSpec((tm, d), lambda i, j: (i, 0)),
                  pl.BlockSpec((halo, d), lambda i, j: (jnp.maximum(i * (tm // halo) - 1, 0), 0)),
                  pl.BlockSpec((1, d), lambda i, j: (0, 0)),
                  vec(1), vec(1),
                  pl.BlockSpec((d, tn), lambda i, j: (0, j)),
                  pl.BlockSpec((d, tn), lambda i, j: (0, nj + j)),
                  pl.BlockSpec((CONV_WIDTH, tn), lambda i, j: (0, j)),
                  pl.BlockSpec((1, CONV_WIDTH - 1, tn), lambda i, j: (i // tpg, 0, j))],
        out_specs=[pl.BlockSpec((tm, tn), lambda i, j: (i, j)),
                   pl.BlockSpec((1, CONV_WIDTH - 1, tn), lambda i, j: (i, 0, j))],
        out_shape=[jax.ShapeDtypeStruct((n, D_FF), BF16),
                   jax.ShapeDtypeStruct((n // tm, CONV_WIDTH - 1, D_FF), F32)],
        scratch_shapes=[pltpu.VMEM((tm, d), BF16), pltpu.VMEM((halo, d), BF16),
                        pltpu.VMEM((halo, tn), F32), pltpu.VMEM((halo, tn), F32)],
        compiler_params=_cparams("parallel", "arbitrary"),
        name="ffn_up",
    )(x, x, g, sc, sh, w, w, cw, buf)
    return act, tails[tpg - 1::tpg]


def _mm_res_kernel(a_ref, w_ref, x_ref, gt_ref, o_ref):
    o_ref[...] = x_ref[...] + gt_ref[...] * jnp.dot(a_ref[...], w_ref[...], preferred_element_type=F32)


def _mm_res(a, w, x, gt, grp):
    n, d = x.shape
    kdim = a.shape[1]
    tm, tpg, r = min(512, grp["tm"]), grp["tpg"] * (grp["tm"] // min(512, grp["tm"])), gt.shape[1]
    if r > 1:
        gt = gt.reshape(r // tm, tm, d)
        gt_spec = pl.BlockSpec((None, tm, d), lambda i: (i, 0, 0))
    else:
        gt_spec = pl.BlockSpec((None, 1, d), lambda i: (i // tpg, 0, 0))
    return pl.pallas_call(
        _mm_res_kernel,
        grid=(n // tm,),
        in_specs=[pl.BlockSpec((tm, kdim), lambda i: (i, 0)),
                  pl.BlockSpec((kdim, d), lambda i: (0, 0)),
                  pl.BlockSpec((tm, d), lambda i: (i, 0)),
                  gt_spec],
        out_specs=pl.BlockSpec((tm, d), lambda i: (i, 0)),
        out_shape=jax.ShapeDtypeStruct((n, d), F32),
        compiler_params=_cparams("parallel"),
        name="mm_res",
    )(a, w, x, gt)


def _prep_layer(l, P):
    d = D_MODEL
    w_in = P["w_in"][l].astype(BF16)
    o_mif = 4 * BRANCH
    o_sb = o_mif + 2 * N_HEADS
    o_aq = o_sb + 3 * BRANCH
    o_rw = o_aq + 3 * BRANCH
    o_g = o_rw + RW_COLS
    w_in_p = jnp.concatenate([
        w_in[:, o_g:o_g + 4 * d],
        w_in[:, o_rw:o_g], w_in[:, o_mif:o_sb], jnp.zeros((d, Z_RWKV_W - RW_COLS - 2 * N_HEADS), BF16),
        w_in[:, 0:o_mif], w_in[:, o_sb:o_aq], w_in[:, o_aq:o_rw]], axis=1)
    assert w_in_p.shape[1] == Z_WIDTH
    bias_if = jnp.concatenate([P["m_bi"][l], P["m_bf"][l], jnp.zeros((LANES - 2 * N_HEADS,), F32)])[None, :]
    zw = Z_RWKV_W
    wlr = jnp.zeros((RW_LORA, 3 * BRANCH), F32)
    wlr = wlr.at[0:32, 0:BRANCH].set(P["r_wB"][l])
    wlr = wlr.at[32:64, BRANCH:2 * BRANCH].set(P["r_aB"][l])
    wlr = wlr.at[64:128, 2 * BRANCH:].set(P["r_gB"][l])
    row = lambda t: t[None, :]
    rw = dict(mu=jnp.pad(P["r_mu"][l], (0, zw - RW_COLS))[None, :], wlr=wlr.astype(BF16),
              w0=row(P["r_w0"][l]), a0=row(P["r_a0"][l]), kk=row(P["r_kk"][l]), ka=row(P["r_ka"][l]),
              rk=row(P["r_rk"][l]))
    bf = lambda name: P[name][l].astype(BF16)
    return dict(
        w_ada=bf("w_ada"), b_ada=row(P["b_ada"][l]), w_in=w_in_p, bias_if=bias_if,
        norm_mix_g=row(P["norm_mix_g"][l]), norm_ffn_g=row(P["norm_ffn_g"][l]),
        m_norm_g=row(P["m_norm_g"][l]), s_conv=P["s_conv"][l],
        qg=row(jnp.tile(P["a_qnorm"][l], N_HEADS)), kg=row(jnp.tile(P["a_knorm"][l], N_HEADS)),
        rw=rw, r_norm_g=row(P["r_norm_g"][l]),
        projs=(bf("m_proj"), bf("s_proj"), bf("a_proj"), bf("r_proj")), w_out=bf("w_out"),
        f_up=bf("f_up"), f_conv=P["f_conv"][l], f_down=bf("f_down"))


def _layer(x, mod, W, state, tables, attend, grp):
    mc, mn, mm, sbuf, rs, rshift, fbuf = state
    b_, l_ = grp["B"], grp["L"]
    sh_m, sc_m, gt_m, sh_f, sc_f, gt_f = mod
    z = _norm_mod_matmul(x, W["norm_mix_g"], sc_m, sh_m, W["w_in"], grp, Z_WIDTH // 3)
    ml, mc, mn, mm = _mlstm(z, W["bias_if"], W["m_norm_g"], mc, mn, mm, grp)
    sc, sbuf = _sconv(z, W["s_conv"], sbuf, grp)
    q, k = _qk_prep(z, W["qg"], W["kg"], tables, grp)
    v = z[:, Z_MOBA + 2 * BRANCH:Z_MOBA + 3 * BRANCH]
    mb = attend(q, k, z, v)
    zw = Z_RWKV_W
    prev = jnp.pad(rshift, ((0, 0), (0, zw - RW_COLS)))[:, None, :]
    r, w, k2, vv, kk, kka, bonus, g = _rwkv_prep(z, prev, W["rw"], grp)
    y, rs_t = _rwkv_scan(r, w, k2, vv, kk, kka, rs, grp)
    rshift = z.reshape(b_, l_, Z_WIDTH)[:, -1, Z_RWKV:Z_RWKV + RW_COLS]
    x = _merge(ml, sc, mb, y, bonus, g, W["r_norm_g"], z, x, gt_m, W["projs"], W["w_out"], grp)
    if sc_f.shape[1] == 1 and l_ % FFN_HALO == 0:
        act, fbuf = _ffn_up(x, W["norm_ffn_g"], sc_f, sh_f, W["f_up"], W["f_conv"], fbuf, grp)
    else:
        up = _norm_mod_matmul(x, W["norm_ffn_g"], sc_f, sh_f, W["f_up"], grp, 1408)
        act = _ffn_act(up, W["f_conv"], fbuf, grp)
        fbuf = up.reshape(b_, l_, 2 * D_FF)[:, l_ - (CONV_WIDTH - 1):, :D_FF]
    x = _mm_res(act, W["f_down"], x, gt_f, grp)
    return x, (k, v), (mc, mn, mm, sbuf, rs_t, rshift, fbuf)


def kernel(x_prompt, x_sample, c_prompt, c_sample, cache_k, cache_v, page_table, state_mlstm_c, state_mlstm_n, state_mlstm_m, state_conv, state_rwkv, state_rwkv_shift, state_ffn_conv, norm_mix_g, norm_ffn_g, w_ada, b_ada, w_in, m_bi, m_bf, m_norm_g, m_proj, s_conv, s_proj, a_qnorm, a_knorm, a_proj, r_mu, r_w0, r_wB, r_a0, r_aB, r_gB, r_kk, r_ka, r_rk, r_norm_g, r_proj, w_out, f_up, f_conv, f_down):
    P = dict(norm_mix_g=norm_mix_g, norm_ffn_g=norm_ffn_g, w_ada=w_ada, b_ada=b_ada, w_in=w_in,
             m_bi=m_bi, m_bf=m_bf, m_norm_g=m_norm_g, m_proj=m_proj, s_conv=s_conv, s_proj=s_proj,
             a_qnorm=a_qnorm, a_knorm=a_knorm, a_proj=a_proj, r_mu=r_mu, r_w0=r_w0, r_wB=r_wB,
             r_a0=r_a0, r_aB=r_aB, r_gB=r_gB, r_kk=r_kk, r_ka=r_ka, r_rk=r_rk, r_norm_g=r_norm_g,
             r_proj=r_proj, w_out=w_out, f_up=f_up, f_conv=f_conv, f_down=f_down)
    depth = w_in.shape[0]
    bp, lp, d = x_prompt.shape
    bs, ls, _ = x_sample.shape
    n_s = bs * ls
    past = page_table.shape[1] * PAGE_SIZE
    tm_p = _pick_tile(lp, 1024)
    grp_p = dict(B=bp, L=lp, tm=tm_p, tpg=lp // tm_p)
    grp_s = dict(B=bs, L=ls, tm=n_s, tpg=1)
    assert n_s % 8 == 0 and n_s <= 1024

    zeros = lambda *s: jnp.zeros(s, F32)
    st_p = (zeros(bp, N_HEADS, HEAD_DIM, HEAD_DIM), zeros(bp, N_HEADS, HEAD_DIM), zeros(bp, N_HEADS),
            zeros(bp, CONV_WIDTH - 1, BRANCH), zeros(bp, N_HEADS, HEAD_DIM, HEAD_DIM), zeros(bp, RW_COLS),
            zeros(bp, CONV_WIDTH - 1, D_FF))
    tab_p = _rope_tables(jnp.arange(lp, dtype=jnp.int32))
    tab_s = tuple(jnp.tile(t, (bs, 1)) for t in _rope_tables(past + jnp.arange(ls, dtype=jnp.int32)))

    c_all = jnp.concatenate([c_prompt, c_sample], axis=0)
    hp = x_prompt.reshape(bp * lp, d)
    hs = x_sample.reshape(n_s, d)
    kv_p, kv_s, sts_p, sts_s = [], [], [], []
    for l in range(depth):
        W = _prep_layer(l, P)
        mod = _ada(c_all, W["w_ada"], W["b_ada"])
        mods = [mod[:, j * d:(j + 1) * d] for j in range(6)]
        mod_p = [m[:bp][:, None, :] for m in mods]
        mod_s = [jnp.repeat(m[bp:], ls, axis=0)[None] for m in mods]

        attend_p = lambda q, k, z, v: _moba_prompt(q, k, z, grp_p)
        hp, kv, st = _layer(hp, mod_p, W, st_p, tab_p, attend_p, grp_p)
        kv_p.append(kv)
        sts_p.append(st)

        prev = (state_mlstm_c[l], state_mlstm_n[l], state_mlstm_m[l], state_conv[l], state_rwkv[l],
                state_rwkv_shift[l], state_ffn_conv[l])
        attend_s = lambda q, k, z, v: _moba_sample(q, k, v, cache_k, cache_v, page_table, l, grp_s)
        hs, kv, st = _layer(hs, mod_s, W, prev, tab_s, attend_s, grp_s)
        kv_s.append(kv)
        sts_s.append(st)

    heads = lambda t, b_, l_: t.reshape(b_, l_, N_HEADS, HEAD_DIM)
    k_prompt = jnp.stack([heads(k, bp, lp) for k, _ in kv_p])
    v_prompt = jnp.stack([heads(v, bp, lp) for _, v in kv_p])
    k_sample = jnp.stack([heads(k, bs, ls) for k, _ in kv_s])
    v_sample = jnp.stack([heads(v, bs, ls) for _, v in kv_s])
    stack = lambda sts: [jnp.stack(t) for t in zip(*sts)]
    return (hp.reshape(bp, lp, d), hs.reshape(bs, ls, d), k_prompt, v_prompt, k_sample, v_sample,
            *stack(sts_p), *stack(sts_s))
```
